```python
import math
import jax, jax.numpy as jnp
from jax import lax
import numpy as np

D_MODEL = 2048
BATCH = 4
SEQ = 4096
DEPTH = 4

N_EVEN = (DEPTH + 1) // 2
N_ODD = DEPTH // 2

CONV_WIDTH = 3
CONV_GROUPS = 8
D_CONV = D_MODEL // 2

NSA_HEADS = 8
NSA_KV_GROUPS = 2
NSA_HPG = NSA_HEADS // NSA_KV_GROUPS
HEAD_DIM = 128
D_NSA = NSA_HEADS * HEAD_DIM
D_KV = NSA_KV_GROUPS * HEAD_DIM
CMP_BLOCK = 32
CMP_STRIDE = 16
SEL_BLOCK = 64
N_SELECT = 16
WINDOW = 512
N_BRANCH = 3
Q_BLOCK = 128

EVEN_SPLITS = (D_CONV, D_CONV, D_CONV, D_CONV, D_NSA, D_KV, D_KV, D_KV, D_KV, D_KV, D_KV, N_BRANCH * NSA_HEADS, D_NSA)
EVEN_IN = sum(EVEN_SPLITS)
D_EVEN_MIX = D_CONV + D_NSA

D_SGU = D_MODEL
SGU_GROUPS = 8
SGU_CHUNK = 128
SGU_GROUP_DIM = D_SGU // SGU_GROUPS

REL_BUCKETS = 32
REL_MAX_DIST = 128

DEEPNORM_ALPHA = (2 * DEPTH) ** 0.25
DEEPNORM_BETA = (8 * DEPTH) ** -0.25
LN_EPS = 1e-5
NEG_INF = -1e30
FORCED_SCORE = 1e9

kernel_name = "hybrid_conv_nsa_sgu_deepnorm"


def layer_norm(x, g, b):
    xf = x.astype(jnp.float32)
    mu = jnp.mean(xf, axis=-1, keepdims=True)
    var = jnp.mean(jnp.square(xf - mu), axis=-1, keepdims=True)
    return ((xf - mu) * lax.rsqrt(var + LN_EPS) * g + b).astype(x.dtype)


def t5_bucket(dist):
    n = jnp.maximum(dist, 0)
    max_exact = REL_BUCKETS // 2
    large = max_exact + (jnp.log(jnp.maximum(n, 1).astype(jnp.float32) / max_exact)
                         / math.log(REL_MAX_DIST / max_exact) * (REL_BUCKETS - max_exact)).astype(jnp.int32)
    large = jnp.minimum(large, REL_BUCKETS - 1)
    return jnp.where(n < max_exact, n, large)


def causal_short_conv(h, w):
    s = h.shape[1]
    hp = jnp.pad(h, ((0, 0), (CONV_WIDTH - 1, 0), (0, 0)))
    out = w[0] * hp[:, 0:s]
    for k in range(1, CONV_WIDTH):
        out = out + w[k] * hp[:, k:k + s]
    return out


def compress_blocks(tok, pos, w1, w2):
    s = tok.shape[2]
    n_cmp = (s - CMP_BLOCK) // CMP_STRIDE + 1
    idx = np.arange(n_cmp)[:, None] * CMP_STRIDE + np.arange(CMP_BLOCK)[None, :]
    blocks = tok[:, :, idx] + pos
    flat = blocks.reshape(blocks.shape[0], blocks.shape[1], n_cmp, CMP_BLOCK * HEAD_DIM)
    return jax.nn.silu(flat @ w1) @ w2


def nsa_attention(q, k_c, v_c, k_s, v_s, k_w, v_w, gates, rel_table, cmp_pos, cmp_w1, cmp_w2):
    bsz, s, _ = q.shape
    g_n, hpg, dh = NSA_KV_GROUPS, NSA_HPG, HEAD_DIM
    scale = dh ** -0.5
    q = q.reshape(bsz, s, g_n, hpg, dh).transpose(0, 2, 3, 1, 4)

    def kv(t):
        return t.reshape(bsz, s, g_n, dh).transpose(0, 2, 1, 3)

    table_g = rel_table.reshape(REL_BUCKETS, g_n, hpg)
    table_gt = jnp.transpose(table_g, (1, 0, 2))

    kc = compress_blocks(kv(k_c), cmp_pos[0], cmp_w1[0], cmp_w2[0])
    vc = compress_blocks(kv(v_c), cmp_pos[1], cmp_w1[1], cmp_w2[1])
    n_cmp = kc.shape[2]
    t = jnp.arange(s)
    cmp_end = jnp.arange(n_cmp) * CMP_STRIDE + CMP_BLOCK - 1
    dist_c = t[:, None] - cmp_end[None, :]
    valid_c = dist_c >= 0
    bias_c = jnp.transpose(table_g[t5_bucket(dist_c)], (2, 3, 0, 1))
    s_c = jnp.einsum('bghqd,bgkd->bghqk', q, kc).astype(jnp.float32) * scale + bias_c
    p_c = jax.nn.softmax(jnp.where(valid_c, s_c, NEG_INF), axis=-1)
    p_c = jnp.where(valid_c, p_c, 0.0)
    o_c = jnp.einsum('bghqk,bgkd->bghqd', p_c.astype(vc.dtype), vc)

    n_sel = s // SEL_BLOCK
    cstart = np.arange(n_cmp)[:, None] * CMP_STRIDE
    sstart = np.arange(n_sel)[None, :] * SEL_BLOCK
    overlap = jnp.asarray(((cstart < sstart + SEL_BLOCK) & (cstart + CMP_BLOCK > sstart)).astype(np.float32))
    imp = jnp.einsum('bghqk,kj->bgqj', p_c, overlap)
    blk = jnp.arange(n_sel)[None, :]
    cur = (t // SEL_BLOCK)[:, None]
    forced = (blk == 0) | (blk == cur) | (blk == cur - 1)
    imp = jnp.where(blk > cur, -1.0, jnp.where(forced, FORCED_SCORE, imp))
    n_top = min(N_SELECT, n_sel)
    _, sel_idx = lax.top_k(imp, n_top)
    n_keys = n_top * SEL_BLOCK

    k_sel_blocks = kv(k_s).reshape(bsz, g_n, n_sel, SEL_BLOCK, dh)
    v_sel_blocks = kv(v_s).reshape(bsz, g_n, n_sel, SEL_BLOCK, dh)
    gather = jax.vmap(jax.vmap(lambda blocks, idx: blocks[idx]))
    g_idx = jnp.arange(g_n)[None, :, None, None]

    k_win_pad = jnp.pad(kv(k_w), ((0, 0), (0, 0), (WINDOW, 0), (0, 0)))
    v_win_pad = jnp.pad(kv(v_w), ((0, 0), (0, 0), (WINDOW, 0), (0, 0)))
    qi = jnp.arange(Q_BLOCK)[:, None]
    kj = jnp.arange(WINDOW + Q_BLOCK)[None, :]
    rel_w = WINDOW + qi - kj
    band = (rel_w >= 0) & (rel_w < WINDOW)
    bias_w = jnp.transpose(table_g[t5_bucket(rel_w)], (2, 3, 0, 1))

    def block_step(q0):
        qb = lax.dynamic_slice_in_dim(q, q0, Q_BLOCK, axis=3)
        tq = q0 + jnp.arange(Q_BLOCK)
        idx = lax.dynamic_slice_in_dim(sel_idx, q0, Q_BLOCK, axis=2)
        ks = gather(k_sel_blocks, idx).reshape(bsz, g_n, Q_BLOCK, n_keys, dh)
        vs = gather(v_sel_blocks, idx).reshape(bsz, g_n, Q_BLOCK, n_keys, dh)
        kpos = (idx[..., None] * SEL_BLOCK + jnp.arange(SEL_BLOCK)).reshape(bsz, g_n, Q_BLOCK, n_keys)
        dist = tq[:, None] - kpos
        bias = jnp.moveaxis(table_gt[g_idx, t5_bucket(dist)], -1, 2)
        sc = jnp.einsum('bghqd,bgqkd->bghqk', qb, ks).astype(jnp.float32) * scale + bias
        p = jax.nn.softmax(jnp.where((dist >= 0)[:, :, None], sc, NEG_INF), axis=-1)
        o_s = jnp.einsum('bghqk,bgqkd->bghqd', p.astype(vs.dtype), vs)
        kw = lax.dynamic_slice_in_dim(k_win_pad, q0, WINDOW + Q_BLOCK, axis=2)
        vw = lax.dynamic_slice_in_dim(v_win_pad, q0, WINDOW + Q_BLOCK, axis=2)
        valid = band & (kj >= WINDOW - q0)
        sw = jnp.einsum('bghqd,bgkd->bghqk', qb, kw).astype(jnp.float32) * scale + bias_w
        pw = jax.nn.softmax(jnp.where(valid, sw, NEG_INF), axis=-1)
        o_w = jnp.einsum('bghqk,bgkd->bghqd', pw.astype(vw.dtype), vw)
        return o_s, o_w

    o_s, o_w = lax.map(block_step, jnp.arange(s // Q_BLOCK) * Q_BLOCK)
    o_s = jnp.moveaxis(o_s, 0, 3).reshape(bsz, g_n, hpg, s, dh)
    o_w = jnp.moveaxis(o_w, 0, 3).reshape(bsz, g_n, hpg, s, dh)

    gate = jax.nn.sigmoid(gates.astype(jnp.float32)).reshape(bsz, s, N_BRANCH, g_n, hpg)
    gate = jnp.transpose(gate, (2, 0, 3, 4, 1))[..., None].astype(o_c.dtype)
    o = gate[0] * o_c + gate[1] * o_s + gate[2] * o_w
    return o.transpose(0, 3, 1, 2, 4).reshape(bsz, s, D_NSA)


def even_layer(x, w_in, conv_w, cmp_pos, cmp_w1, cmp_w2, w_out, rel_table):
    h = x @ w_in
    offs = [int(o) for o in np.cumsum(EVEN_SPLITS)[:-1]]
    (a_h, a_b, a_c, a_z, q, k_c, v_c, k_s, v_s, k_w, v_w, gates, b_z) = jnp.split(h, offs, axis=-1)
    y_a = a_b * causal_short_conv(a_c * a_h, conv_w) * jax.nn.silu(a_z)
    y_b = nsa_attention(q, k_c, v_c, k_s, v_s, k_w, v_w, gates, rel_table, cmp_pos, cmp_w1, cmp_w2) * jax.nn.silu(b_z)
    return jnp.concatenate([y_a, y_b], axis=-1) @ w_out


def odd_layer(x, w_in, ln_g, ln_b, sgu_w, sgu_b, w_out):
    bsz, s, _ = x.shape
    h = x @ w_in
    uv = jax.nn.gelu(h[..., :2 * D_SGU])
    z = h[..., 2 * D_SGU:]
    u, v = jnp.split(uv, 2, axis=-1)
    v = layer_norm(v, ln_g, ln_b).reshape(bsz, s // SGU_CHUNK, SGU_CHUNK, SGU_GROUPS, SGU_GROUP_DIM)
    w_causal = sgu_w * jnp.tril(jnp.ones((SGU_CHUNK, SGU_CHUNK), sgu_w.dtype))
    mixed = jnp.einsum('gts,bnsgc->bntgc', w_causal, v) + sgu_b.T[:, :, None]
    y = u * mixed.reshape(bsz, s, D_SGU) * jax.nn.silu(z)
    return y @ w_out


def setup_inputs(seed: int = 0) -> dict:
    key = jax.random.key(seed)
    ks = jax.random.split(key, 16)

    def nrm(k, shape, scale):
        return jax.random.normal(k, shape, jnp.float32) * scale

    return {
        "x": nrm(ks[0], (BATCH, SEQ, D_MODEL), 1.0),
        "rel_bias_table": nrm(ks[1], (REL_BUCKETS, NSA_HEADS), 0.5),
        "ln_g": 1.0 + nrm(ks[2], (DEPTH, D_MODEL), 0.02),
        "ln_b": nrm(ks[3], (DEPTH, D_MODEL), 0.02),
        "ev_w_in": nrm(ks[4], (N_EVEN, D_MODEL, EVEN_IN), D_MODEL ** -0.5),
        "ev_conv_w": nrm(ks[5], (N_EVEN, CONV_WIDTH, D_CONV), CONV_WIDTH ** -0.5),
        "ev_cmp_pos": nrm(ks[6], (N_EVEN, 2, CMP_BLOCK, HEAD_DIM), 0.5),
        "ev_cmp_w1": nrm(ks[7], (N_EVEN, 2, CMP_BLOCK * HEAD_DIM, HEAD_DIM), (CMP_BLOCK * HEAD_DIM) ** -0.5),
        "ev_cmp_w2": nrm(ks[8], (N_EVEN, 2, HEAD_DIM, HEAD_DIM), HEAD_DIM ** -0.5),
        "ev_w_out": nrm(ks[9], (N_EVEN, D_EVEN_MIX, D_MODEL), D_EVEN_MIX ** -0.5 * DEEPNORM_BETA),
        "od_w_in": nrm(ks[10], (N_ODD, D_MODEL, 3 * D_SGU), D_MODEL ** -0.5),
        "od_ln_g": 1.0 + nrm(ks[11], (N_ODD, D_SGU), 0.02),
        "od_ln_b": nrm(ks[12], (N_ODD, D_SGU), 0.02),
        "od_sgu_w": nrm(ks[13], (N_ODD, SGU_GROUPS, SGU_CHUNK, SGU_CHUNK), SGU_CHUNK ** -0.5),
        "od_sgu_b": 1.0 + nrm(ks[14], (N_ODD, SGU_GROUPS, SGU_CHUNK), 0.02),
        "od_w_out": nrm(ks[15], (N_ODD, D_SGU, D_MODEL), D_SGU ** -0.5 * DEEPNORM_BETA),
    }


def reference(x, rel_bias_table, ln_g, ln_b, ev_w_in, ev_conv_w, ev_cmp_pos, ev_cmp_w1, ev_cmp_w2, ev_w_out,
              od_w_in, od_ln_g, od_ln_b, od_sgu_w, od_sgu_b, od_w_out):
    for layer in range(DEPTH):
        i = layer // 2
        if layer % 2 == 0:
            y = even_layer(x, ev_w_in[i], ev_conv_w[i], ev_cmp_pos[i], ev_cmp_w1[i], ev_cmp_w2[i],
                           ev_w_out[i], rel_bias_table)
        else:
            y = odd_layer(x, od_w_in[i], od_ln_g[i], od_ln_b[i], od_sgu_w[i], od_sgu_b[i], od_w_out[i])
        x = layer_norm(DEEPNORM_ALPHA * x + y, ln_g[layer], ln_b[layer])
    return x
```

```python
import functools
import math

import numpy as np
import jax
import jax.numpy as jnp
from jax import lax
from jax.experimental import pallas as pl
from jax.experimental.pallas import tpu as pltpu

F32 = jnp.float32
BF16 = jnp.bfloat16

D_MODEL = 2048
CONV_WIDTH = 3
D_CONV = 1024
NSA_HEADS = 8
NSA_KV_GROUPS = 2
NSA_HPG = NSA_HEADS // NSA_KV_GROUPS
HEAD_DIM = 128
D_NSA = NSA_HEADS * HEAD_DIM
D_KV = NSA_KV_GROUPS * HEAD_DIM
CMP_BLOCK = 32
CMP_STRIDE = 16
SEL_BLOCK = 64
N_SELECT = 16
WINDOW = 512
N_BRANCH = 3
D_SGU = D_MODEL
SGU_GROUPS = 8
SGU_CHUNK = 128
SGU_GROUP_DIM = D_SGU // SGU_GROUPS
REL_BUCKETS = 32
REL_MAX_DIST = 128
LN_EPS = 1e-5
NEG_INF = -1e30
FORCED_SCORE = 1e9
GROUP_W = NSA_HPG * HEAD_DIM

LANE = 128
SUBLANE = 8
VMEM_LIMIT = 56 * 1024 * 1024

HA_AH, HA_AB, HA_AC, HA_AZ, HA_BZ, HA_KC = 0, 1024, 2048, 3072, 4096, 5120
HA_WIDTH = 5632
HB_Q, HB_KS, HB_VS, HB_KW, HB_VW = 0, 1024, 1280, 1536, 1792
HB_WIDTH = 2048

ATT_TILE = 256
N_BIAS_KINDS = 4


def _cparams(*sem):
    return pltpu.CompilerParams(dimension_semantics=sem, vmem_limit_bytes=VMEM_LIMIT)


def _sigmoid(x):
    return 1.0 / (1.0 + jnp.exp(-x))


def _silu(x):
    return x * _sigmoid(x)


def _gelu_tanh(x):
    c = math.sqrt(2.0 / math.pi)
    return x * (0.5 * (1.0 + jnp.tanh(c * (x + 0.044715 * (x * x * x)))))


def _dot_nt(a, b):
    return lax.dot_general(a, b, (((1,), (1,)), ((), ())), preferred_element_type=F32)


def _dot(a, b):
    return jnp.dot(a, b, preferred_element_type=F32)


def _matmul_kernel(x_ref, w_ref, o_ref):
    o_ref[...] = _dot(x_ref[...], w_ref[...]).astype(o_ref.dtype)


def _matmul(x, w, out_dtype, tm, tn, name):
    m, k = x.shape
    n = w.shape[1]
    assert m % tm == 0 and n % tn == 0
    return pl.pallas_call(
        _matmul_kernel,
        out_shape=jax.ShapeDtypeStruct((m, n), out_dtype),
        grid=(m // tm, n // tn),
        in_specs=[pl.BlockSpec((tm, k), lambda i, j: (i, 0)),
                  pl.BlockSpec((k, tn), lambda i, j: (0, j))],
        out_specs=pl.BlockSpec((tm, tn), lambda i, j: (i, j)),
        compiler_params=_cparams("parallel", "arbitrary"),
        name=name,
    )(x, w)


def _outproj_kernel(y1_ref, y2_ref, w1_ref, w2_ref, x_ref, g_ref, b_ref, o_ref, ob_ref, *, alpha):
    y = _dot(y1_ref[...], w1_ref[...]) + _dot(y2_ref[...], w2_ref[...])
    z = alpha * x_ref[...] + y
    mu = jnp.mean(z, axis=-1, keepdims=True)
    zc = z - mu
    var = jnp.mean(zc * zc, axis=-1, keepdims=True)
    out = zc * lax.rsqrt(var + LN_EPS) * g_ref[...] + b_ref[...]
    o_ref[...] = out
    ob_ref[...] = out.astype(BF16)


def _outproj_ln(y1, y1_col, y2, y2_col, w_out, x, g, b, alpha, tm, name):
    m, d = x.shape
    kh = w_out.shape[0] // 2
    return pl.pallas_call(
        functools.partial(_outproj_kernel, alpha=alpha),
        out_shape=(jax.ShapeDtypeStruct((m, d), F32), jax.ShapeDtypeStruct((m, d), BF16)),
        grid=(m // tm,),
        in_specs=[pl.BlockSpec((tm, kh), lambda i: (i, y1_col)),
                  pl.BlockSpec((tm, kh), lambda i: (i, y2_col)),
                  pl.BlockSpec((kh, d), lambda i: (0, 0)),
                  pl.BlockSpec((kh, d), lambda i: (1, 0)),
                  pl.BlockSpec((tm, d), lambda i: (i, 0)),
                  pl.BlockSpec((1, d), lambda i: (0, 0)),
                  pl.BlockSpec((1, d), lambda i: (0, 0))],
        out_specs=(pl.BlockSpec((tm, d), lambda i: (i, 0)),
                   pl.BlockSpec((tm, d), lambda i: (i, 0))),
        compiler_params=_cparams("parallel"),
        name=name,
    )(y1, y2, w_out, w_out, x, g, b)


def _sgu_kernel(u_ref, v_ref, z_ref, g_ref, b_ref, w_ref, sb_ref, o_ref, *, tm):
    v = _gelu_tanh(v_ref[...])
    mu = jnp.mean(v, axis=-1, keepdims=True)
    vc = v - mu
    var = jnp.mean(vc * vc, axis=-1, keepdims=True)
    vn = (vc * lax.rsqrt(var + LN_EPS) * g_ref[...] + b_ref[...]).astype(BF16)
    row = lax.broadcasted_iota(jnp.int32, (SGU_CHUNK, SGU_CHUNK), 0)
    col = lax.broadcasted_iota(jnp.int32, (SGU_CHUNK, SGU_CHUNK), 1)
    tril = col <= row
    for g in range(SGU_GROUPS):
        wg = jnp.where(tril, w_ref[g], 0.0).astype(BF16)
        bias = sb_ref[:, g:g + 1]
        cs = slice(g * SGU_GROUP_DIM, (g + 1) * SGU_GROUP_DIM)
        for c in range(tm // SGU_CHUNK):
            rs = slice(c * SGU_CHUNK, (c + 1) * SGU_CHUNK)
            mixed = _dot(wg, vn[rs, cs]) + bias
            u = _gelu_tanh(u_ref[rs, cs])
            o_ref[rs, cs] = (u * mixed * _silu(z_ref[rs, cs])).astype(BF16)


def _sgu(h, ln_g, ln_b, sgu_w, sgu_bt, tm):
    m = h.shape[0]
    return pl.pallas_call(
        functools.partial(_sgu_kernel, tm=tm),
        out_shape=jax.ShapeDtypeStruct((m, D_SGU), BF16),
        grid=(m // tm,),
        in_specs=[pl.BlockSpec((tm, D_SGU), lambda i: (i, 0)),
                  pl.BlockSpec((tm, D_SGU), lambda i: (i, 1)),
                  pl.BlockSpec((tm, D_SGU), lambda i: (i, 2)),
                  pl.BlockSpec((1, D_SGU), lambda i: (0, 0)),
                  pl.BlockSpec((1, D_SGU), lambda i: (0, 0)),
                  pl.BlockSpec((SGU_GROUPS, SGU_CHUNK, SGU_CHUNK), lambda i: (0, 0, 0)),
                  pl.BlockSpec((SGU_CHUNK, SGU_GROUPS), lambda i: (0, 0))],
        out_specs=pl.BlockSpec((tm, D_SGU), lambda i: (i, 0)),
        compiler_params=_cparams("parallel"),
        name="sgu_mix",
    )(h, h, h, ln_g, ln_b, sgu_w, sgu_bt)


def _conv_kernel(ah_ref, ab_ref, ac_ref, az_ref, ahp_ref, acp_ref, w_ref, o_ref, u_ref, *, ts, tiles_per_seq):
    first = (pl.program_id(0) % tiles_per_seq) == 0
    u_ref[0:SUBLANE, :] = jnp.where(first, 0.0, acp_ref[...] * ahp_ref[...])
    u = ac_ref[...] * ah_ref[...]
    u_ref[SUBLANE:SUBLANE + ts, :] = u
    conv = w_ref[2:3, :] * u
    for k in range(CONV_WIDTH - 1):
        shift = CONV_WIDTH - 1 - k
        conv = conv + w_ref[k:k + 1, :] * u_ref[SUBLANE - shift:SUBLANE - shift + ts, :]
    o_ref[...] = (ab_ref[...] * conv * _silu(az_ref[...])).astype(BF16)


def _gated_conv(ha, conv_w, seq, ts, tc):
    m = ha.shape[0]
    ncb = D_CONV // tc
    tiles_per_seq = seq // ts
    rb = ts // SUBLANE

    def col(base):
        return lambda i, j: (i, base // tc + j)

    def prev(base):
        return lambda i, j: (jnp.maximum(i * rb - 1, 0), base // tc + j)

    return pl.pallas_call(
        functools.partial(_conv_kernel, ts=ts, tiles_per_seq=tiles_per_seq),
        out_shape=jax.ShapeDtypeStruct((m, D_CONV), BF16),
        grid=(m // ts, ncb),
        in_specs=[pl.BlockSpec((ts, tc), col(HA_AH)),
                  pl.BlockSpec((ts, tc), col(HA_AB)),
                  pl.BlockSpec((ts, tc), col(HA_AC)),
                  pl.BlockSpec((ts, tc), col(HA_AZ)),
                  pl.BlockSpec((SUBLANE, tc), prev(HA_AH)),
                  pl.BlockSpec((SUBLANE, tc), prev(HA_AC)),
                  pl.BlockSpec((CONV_WIDTH, tc), lambda i, j: (0, j))],
        out_specs=pl.BlockSpec((ts, tc), lambda i, j: (i, j)),
        scratch_shapes=[pltpu.VMEM((ts + SUBLANE, tc), F32)],
        compiler_params=_cparams("parallel", "arbitrary"),
        name="gated_conv",
    )(ha, ha, ha, ha, ha, ha, conv_w)


def _compress_kernel(tok_ref, w1_ref, w2_ref, pos_ref, o_ref, b_ref, *, rows):
    half = CMP_STRIDE * HEAD_DIM
    x2 = jnp.concatenate(
        [tok_ref[pl.ds(l, rows, stride=CMP_STRIDE), :] for l in range(CMP_STRIDE)], axis=1).astype(BF16)
    lo = _dot(x2, w1_ref[0, 0:half, :])
    hi = _dot(x2, w1_ref[0, half:2 * half, :])
    b_ref[0:rows, :] = hi
    b_ref[rows:rows + SUBLANE, :] = jnp.zeros((SUBLANE, HEAD_DIM), F32)
    posb = _dot(jnp.broadcast_to(pos_ref[0], (SUBLANE, 2 * half)), w1_ref[0])[0:1, :]
    pre = lo + b_ref[1:rows + 1, :] + posb
    o_ref[0, 0] = _dot(_silu(pre).astype(BF16), w2_ref[0]).astype(BF16)


def _compress(ha, w1, w2, pos, batch, seq):
    rows = seq // CMP_STRIDE
    nkv = 2 * NSA_KV_GROUPS
    kc_block = HA_KC // HEAD_DIM
    return pl.pallas_call(
        functools.partial(_compress_kernel, rows=rows),
        out_shape=jax.ShapeDtypeStruct((batch, nkv, rows, HEAD_DIM), BF16),
        grid=(batch, nkv),
        in_specs=[pl.BlockSpec((seq, HEAD_DIM), lambda b, c: (b, kc_block + c)),
                  pl.BlockSpec((1, CMP_BLOCK * HEAD_DIM, HEAD_DIM), lambda b, c: (c // NSA_KV_GROUPS, 0, 0)),
                  pl.BlockSpec((1, HEAD_DIM, HEAD_DIM), lambda b, c: (c // NSA_KV_GROUPS, 0, 0)),
                  pl.BlockSpec((1, 1, CMP_BLOCK * HEAD_DIM), lambda b, c: (c // NSA_KV_GROUPS, 0, 0))],
        out_specs=pl.BlockSpec((1, 1, rows, HEAD_DIM), lambda b, c: (b, c, 0, 0)),
        scratch_shapes=[pltpu.VMEM((rows + SUBLANE, HEAD_DIM), F32)],
        compiler_params=_cparams("parallel", "arbitrary"),
        name="cmp_blocks",
    )(ha, w1, w2, pos)


def _t5_bucket(dist):
    n = jnp.maximum(dist, 0)
    max_exact = REL_BUCKETS // 2
    large = max_exact + (jnp.log(jnp.maximum(n, 1).astype(F32) / max_exact)
                         / math.log(REL_MAX_DIST / max_exact) * (REL_BUCKETS - max_exact)).astype(jnp.int32)
    large = jnp.minimum(large, REL_BUCKETS - 1)
    return jnp.where(n < max_exact, n, large)


def _rel_bias(dist, valid, tab_ref, head):
    bkt = _t5_bucket(dist)
    acc = jnp.zeros(dist.shape, F32)
    for b in range(REL_BUCKETS):
        acc = jnp.where(bkt == b, tab_ref[b, head], acc)
    return jnp.where(valid, acc, NEG_INF)


def _bias_cmp_kernel(tab_ref, o_ref, *, tb, rows, n_cmp):
    head = pl.program_id(0)
    t = pl.program_id(1) * tb + lax.broadcasted_iota(jnp.int32, (tb, rows), 0)
    n = lax.broadcasted_iota(jnp.int32, (tb, rows), 1)
    dist = t - (n * CMP_STRIDE + CMP_BLOCK - 1)
    o_ref[0] = _rel_bias(dist, (dist >= 0) & (n < n_cmp), tab_ref, head)


def _bias_cmp(table, seq, tb):
    rows = seq // CMP_STRIDE
    n_cmp = (seq - CMP_BLOCK) // CMP_STRIDE + 1
    return pl.pallas_call(
        functools.partial(_bias_cmp_kernel, tb=tb, rows=rows, n_cmp=n_cmp),
        out_shape=jax.ShapeDtypeStruct((NSA_HEADS, seq, rows), F32),
        grid=(NSA_HEADS, seq // tb),
        in_specs=[pl.BlockSpec(memory_space=pltpu.SMEM)],
        out_specs=pl.BlockSpec((1, tb, rows), lambda h, i: (h, i, 0)),
        compiler_params=_cparams("parallel", "arbitrary"),
        name="bias_cmp",
    )(table)


def _bias_tiles_kernel(tab_ref, o_ref, *, tq):
    head = pl.program_id(0)
    kind = pl.program_id(1)
    ij = (lax.broadcasted_iota(jnp.int32, (tq, tq), 0) - lax.broadcasted_iota(jnp.int32, (tq, tq), 1))
    far = jnp.full((tq, tq), 2 * tq, jnp.int32)
    dist = jnp.where(kind == 0, ij, jnp.where(kind == 1, tq + ij, jnp.where(kind == 2, far, WINDOW + ij)))
    lo = jnp.where(kind == 0, 0, -tq)
    hi = jnp.where(kind == 3, 0, tq)
    o_ref[0, 0] = _rel_bias(dist, (ij >= lo) & (ij < hi), tab_ref, head)


def _bias_tiles(table, tq):
    assert 2 * tq <= WINDOW and WINDOW % tq == 0 and tq + 1 >= REL_MAX_DIST
    return pl.pallas_call(
        functools.partial(_bias_tiles_kernel, tq=tq),
        out_shape=jax.ShapeDtypeStruct((NSA_KV_GROUPS, N_BIAS_KINDS, NSA_HPG * tq, tq), F32),
        grid=(NSA_HEADS, N_BIAS_KINDS),
        in_specs=[pl.BlockSpec(memory_space=pltpu.SMEM)],
        out_specs=pl.BlockSpec((1, 1, tq, tq), lambda h, k: (h // NSA_HPG, k, h % NSA_HPG, 0)),
        compiler_params=_cparams("parallel", "arbitrary"),
        name="bias_tiles",
    )(table)


def _cmp_attn_kernel(q_ref, kc_ref, vc_ref, bias_ref, ov_ref, oc_ref, sel_ref, *, tq, n_sel, n_top):
    scale = HEAD_DIM ** -0.5
    kc = kc_ref[0, 0]
    vc = vc_ref[0, 0]
    psum = None
    for j in range(NSA_HPG):
        hs = slice(j * HEAD_DIM, (j + 1) * HEAD_DIM)
        bias = bias_ref[j]
        s = _dot_nt(q_ref[:, hs], kc) * scale + bias
        m = jnp.max(s, axis=-1, keepdims=True)
        e = jnp.exp(s - m)
        p = e / jnp.sum(e, axis=-1, keepdims=True)
        p = jnp.where(bias > 0.5 * NEG_INF, p, 0.0)
        oc_ref[:, hs] = _dot(p.astype(BF16), vc)
        psum = p if psum is None else psum + p
    ov = ov_ref[...]
    imp = None
    rem = psum
    for _ in range(3):
        piece = rem.astype(BF16)
        part = _dot_nt(ov, piece)
        imp = part if imp is None else imp + part
        rem = rem - piece.astype(F32)
    t = pl.program_id(1) * tq + lax.broadcasted_iota(jnp.int32, (n_sel, tq), 1)
    cur = jnp.right_shift(t, int(math.log2(SEL_BLOCK)))
    blk = lax.broadcasted_iota(jnp.int32, (n_sel, tq), 0)
    forced = (blk == 0) | (blk == cur) | (blk == cur - 1)
    imp = jnp.where(blk > cur, -1.0, jnp.where(forced, FORCED_SCORE, imp))
    rank = jnp.zeros((n_sel, tq), jnp.int32)
    for i in range(n_sel):
        row = imp[i:i + 1, :]
        ahead = (row > imp) | ((row == imp) & (blk > i))
        rank = rank + ahead.astype(jnp.int32)
    sel_t = jnp.where(rank < n_top, 0.0, NEG_INF)
    if n_sel < LANE:
        sel_t = jnp.concatenate([sel_t, jnp.zeros((LANE - n_sel, tq), F32)], axis=0)
    sel_ref[0, 0] = sel_t.T.astype(BF16)


def _cmp_attn(hb, kcv, bias_c, ov, batch, seq, tq):
    rows = seq // CMP_STRIDE
    n_sel = seq // SEL_BLOCK
    n_top = min(N_SELECT, n_sel)
    assert n_sel <= LANE and n_sel % SUBLANE == 0
    nq = seq // tq
    g_n = NSA_KV_GROUPS
    return pl.pallas_call(
        functools.partial(_cmp_attn_kernel, tq=tq, n_sel=n_sel, n_top=n_top),
        out_shape=(jax.ShapeDtypeStruct((batch * seq, D_NSA), F32),
                   jax.ShapeDtypeStruct((batch, g_n, seq, LANE), BF16)),
        grid=(g_n, nq, batch),
        in_specs=[pl.BlockSpec((tq, GROUP_W), lambda g, i, b: (b * nq + i, HB_Q // GROUP_W + g)),
                  pl.BlockSpec((1, 1, rows, HEAD_DIM), lambda g, i, b: (b, g, 0, 0)),
                  pl.BlockSpec((1, 1, rows, HEAD_DIM), lambda g, i, b: (b, g_n + g, 0, 0)),
                  pl.BlockSpec((NSA_HPG, tq, rows), lambda g, i, b: (g, i, 0)),
                  pl.BlockSpec((n_sel, rows), lambda g, i, b: (0, 0))],
        out_specs=(pl.BlockSpec((tq, GROUP_W), lambda g, i, b: (b * nq + i, g)),
                   pl.BlockSpec((1, 1, tq, LANE), lambda g, i, b: (b, g, i, 0))),
        compiler_params=_cparams("parallel", "parallel", "arbitrary"),
        name="cmp_attn",
    )(hb, kcv, kcv, bias_c, ov)


def _flash_step(q, k, v, bias, carry, scale):
    m, l, acc = carry
    s = _dot_nt(q, k) * scale + bias
    m_new = jnp.maximum(m, jnp.max(s, axis=-1, keepdims=True))
    alpha = jnp.exp(m - m_new)
    p = jnp.exp(s - m_new)
    l = alpha * l + jnp.sum(p, axis=-1, keepdims=True)
    acc = alpha * acc + _dot(p.astype(BF16), v)
    return m_new, l, acc


def _selwin_kernel(q_ref, ks_ref, vs_ref, kw_ref, vw_ref, sel_ref, bias_ref, oc_ref, gt_ref, bz_ref, o_ref,
                   *, tq):
    scale = HEAD_DIM ** -0.5
    qi = pl.program_id(2)
    rows = NSA_HPG * tq
    q4 = jnp.concatenate([q_ref[:, j * HEAD_DIM:(j + 1) * HEAD_DIM] for j in range(NSA_HPG)], axis=0)
    sel4 = jnp.concatenate([sel_ref[0, 0]] * NSA_HPG, axis=0)
    qa = jnp.concatenate([q4, sel4], axis=1)
    krow = lax.broadcasted_iota(jnp.int32, (tq, LANE), 0)
    kcol = lax.broadcasted_iota(jnp.int32, (tq, LANE), 1)
    shift = int(math.log2(SEL_BLOCK))
    init = (jnp.full((rows, 1), -3e38, F32), jnp.zeros((rows, 1), F32), jnp.zeros((rows, HEAD_DIM), F32))

    def sel_body(kt, carry):
        off = pl.multiple_of(kt * tq, tq)
        onehot = jnp.where(jnp.right_shift(off + krow, shift) == kcol, 1.0, 0.0).astype(BF16)
        ka = jnp.concatenate([ks_ref[pl.ds(off, tq), :], onehot], axis=1)
        bias = bias_ref[0, jnp.minimum(qi - kt, 2)]
        return _flash_step(qa, ka, vs_ref[pl.ds(off, tq), :], bias, carry, scale)

    _, l_s, acc_s = lax.fori_loop(0, qi + 1, sel_body, init)

    nwin = WINDOW // tq

    def win_body(kt, carry):
        off = pl.multiple_of(kt * tq, tq)
        d = qi - kt
        bias = bias_ref[0, jnp.where(d == nwin, 3, jnp.minimum(d, 2))]
        return _flash_step(q4, kw_ref[pl.ds(off, tq), :], vw_ref[pl.ds(off, tq), :], bias, carry, scale)

    _, l_w, acc_w = lax.fori_loop(jnp.maximum(qi - nwin, 0), qi + 1, win_body, init)

    o_s = acc_s / l_s
    o_w = acc_w / l_w
    gate = _sigmoid(gt_ref[...])
    for j in range(NSA_HPG):
        hs = slice(j * HEAD_DIM, (j + 1) * HEAD_DIM)
        rs = slice(j * tq, (j + 1) * tq)
        o = (gate[:, j:j + 1] * oc_ref[:, hs]
             + gate[:, NSA_HPG + j:NSA_HPG + j + 1] * o_s[rs, :]
             + gate[:, 2 * NSA_HPG + j:2 * NSA_HPG + j + 1] * o_w[rs, :])
        o_ref[:, hs] = (o * _silu(bz_ref[:, hs])).astype(BF16)


def _selwin_attn(hb, ha, hg, sel, bias_t, o_c, batch, seq, tq):
    nq = seq // tq
    kv_block = lambda base: (lambda b, g, i: (b, base // HEAD_DIM + g))
    row_g = lambda b, g, i: (b * nq + i, g)
    return pl.pallas_call(
        functools.partial(_selwin_kernel, tq=tq),
        out_shape=jax.ShapeDtypeStruct((batch * seq, D_NSA), BF16),
        grid=(batch, NSA_KV_GROUPS, nq),
        in_specs=[pl.BlockSpec((tq, GROUP_W), lambda b, g, i: (b * nq + i, HB_Q // GROUP_W + g)),
                  pl.BlockSpec((seq, HEAD_DIM), kv_block(HB_KS)),
                  pl.BlockSpec((seq, HEAD_DIM), kv_block(HB_VS)),
                  pl.BlockSpec((seq, HEAD_DIM), kv_block(HB_KW)),
                  pl.BlockSpec((seq, HEAD_DIM), kv_block(HB_VW)),
                  pl.BlockSpec((1, 1, tq, LANE), lambda b, g, i: (b, g, i, 0)),
                  pl.BlockSpec((1, N_BIAS_KINDS, NSA_HPG * tq, tq), lambda b, g, i: (g, 0, 0, 0)),
                  pl.BlockSpec((tq, GROUP_W), row_g),
                  pl.BlockSpec((tq, LANE), row_g),
                  pl.BlockSpec((tq, GROUP_W), lambda b, g, i: (b * nq + i, HA_BZ // GROUP_W + g))],
        out_specs=pl.BlockSpec((tq, GROUP_W), row_g),
        compiler_params=_cparams("parallel", "parallel", "arbitrary"),
        name="selwin_attn",
    )(hb, hb, hb, hb, hb, sel, bias_t, o_c, hg, ha)


def _split_even_w_in(w):
    d_a = 4 * D_CONV
    o_q = d_a
    o_kc = o_q + D_NSA
    o_ks = o_kc + 2 * D_KV
    o_gt = o_ks + 4 * D_KV
    o_bz = o_gt + N_BRANCH * NSA_HEADS
    w_a = jnp.concatenate([w[:, :d_a], w[:, o_bz:], w[:, o_kc:o_ks]], axis=1).astype(BF16)
    w_b = jnp.concatenate([w[:, o_q:o_kc], w[:, o_ks:o_gt]], axis=1).astype(BF16)
    w_g = w[:, o_gt:o_bz].reshape(D_MODEL, N_BRANCH, NSA_KV_GROUPS, NSA_HPG)
    w_g = jnp.transpose(w_g, (0, 2, 1, 3)).reshape(D_MODEL, NSA_KV_GROUPS, N_BRANCH * NSA_HPG)
    w_g = jnp.pad(w_g, ((0, 0), (0, 0), (0, LANE - N_BRANCH * NSA_HPG)))
    return w_a, w_b, w_g.reshape(D_MODEL, NSA_KV_GROUPS * LANE).astype(BF16)


def _overlap_matrix(seq):
    rows = seq // CMP_STRIDE
    n_cmp = (seq - CMP_BLOCK) // CMP_STRIDE + 1
    n_sel = seq // SEL_BLOCK
    cstart = np.arange(rows)[None, :] * CMP_STRIDE
    sstart = np.arange(n_sel)[:, None] * SEL_BLOCK
    ov = (cstart < sstart + SEL_BLOCK) & (cstart + CMP_BLOCK > sstart) & (np.arange(rows)[None, :] < n_cmp)
    return jnp.asarray(ov.astype(np.float32), dtype=BF16)


def kernel(x, rel_bias_table, ln_g, ln_b, ev_w_in, ev_conv_w, ev_cmp_pos, ev_cmp_w1, ev_cmp_w2, ev_w_out,
           od_w_in, od_ln_g, od_ln_b, od_sgu_w, od_sgu_b, od_w_out):
    batch, seq, d = x.shape
    depth = ln_g.shape[0]
    alpha = (2 * depth) ** 0.25
    m = batch * seq
    tq = ATT_TILE
    assert d == D_MODEL and seq % (2 * tq) == 0

    xf = x.reshape(m, d)
    xb = xf.astype(BF16)
    bias_c = _bias_cmp(rel_bias_table, seq, tb=256)
    bias_t = _bias_tiles(rel_bias_table, tq)
    ov = _overlap_matrix(seq)

    for layer in range(depth):
        i = layer // 2
        g = ln_g[layer].reshape(1, d)
        b = ln_b[layer].reshape(1, d)
        if layer % 2 == 0:
            w_a, w_b, w_g = _split_even_w_in(ev_w_in[i])
            ha = _matmul(xb, w_a, F32, 512, 512, "proj_even_a")
            hb = _matmul(xb, w_b, BF16, 512, 512, "proj_even_b")
            hg = _matmul(xb, w_g, F32, 512, NSA_KV_GROUPS * LANE, "proj_even_gates")
            y_a = _gated_conv(ha, ev_conv_w[i], seq, ts=512, tc=512)
            kcv = _compress(ha, ev_cmp_w1[i].astype(BF16), ev_cmp_w2[i].astype(BF16),
                            ev_cmp_pos[i].reshape(2, 1, CMP_BLOCK * HEAD_DIM).astype(BF16), batch, seq)
            o_c, sel = _cmp_attn(hb, kcv, bias_c, ov, batch, seq, tq)
            y_b = _selwin_attn(hb, ha, hg, sel, bias_t, o_c, batch, seq, tq)
            xf, xb = _outproj_ln(y_a, 0, y_b, 0, ev_w_out[i].astype(BF16), xf, g, b, alpha, 256, "outproj_even")
        else:
            h = _matmul(xb, od_w_in[i].astype(BF16), F32, 512, 512, "proj_odd")
            y = _sgu(h, od_ln_g[i].reshape(1, d), od_ln_b[i].reshape(1, d), od_sgu_w[i],
                     jnp.transpose(od_sgu_b[i]), tm=256)
            xf, xb = _outproj_ln(y, 0, y, 1, od_w_out[i].astype(BF16), xf, g, b, alpha, 256, "outproj_odd")
    return xf.reshape(batch, seq, d)
```

```python
import functools
import math

import numpy as np
import jax
import jax.numpy as jnp
from jax import lax
from jax.experimental import pallas as pl
from jax.experimental.pallas import tpu as pltpu

F32 = jnp.float32
BF16 = jnp.bfloat16

D_MODEL = 2048
CONV_WIDTH = 3
D_CONV = 1024
NSA_HEADS = 8
NSA_KV_GROUPS = 2
NSA_HPG = NSA_HEADS // NSA_KV_GROUPS
HEAD_DIM = 128
D_NSA = NSA_HEADS * HEAD_DIM
D_KV = NSA_KV_GROUPS * HEAD_DIM
CMP_BLOCK = 32
CMP_STRIDE = 16
SEL_BLOCK = 64
N_SELECT = 16
WINDOW = 512
N_BRANCH = 3
D_SGU = D_MODEL
SGU_GROUPS = 8
SGU_CHUNK = 128
SGU_GROUP_DIM = D_SGU // SGU_GROUPS
REL_BUCKETS = 32
REL_MAX_DIST = 128
LN_EPS = 1e-5
NEG_INF = -1e30
FORCED_SCORE = 1e9
GROUP_W = NSA_HPG * HEAD_DIM
ATTN_SCALE = HEAD_DIM ** -0.5

LANE = 128
SUBLANE = 8
VMEM_LIMIT = 56 * 1024 * 1024

EV_A = 0
EV_Q = 4 * D_CONV
EV_KC = EV_Q + D_NSA
EV_KS = EV_KC + 2 * D_KV
EV_GT = EV_KS + 4 * D_KV
EV_BZ = EV_GT + N_BRANCH * NSA_HEADS
EV_END = EV_BZ + D_NSA

HB_Q, HB_KS, HB_VS, HB_KW, HB_VW = 0, 1024, 1280, 1536, 1792
HA_KC, HA_BZ, HA_GT = 0, 512, 1536
SLAB_W = 2048
PROJ_CHUNK = 512

ATT_TILE = 256
N_BIAS_KINDS = 4


def _cparams(*sem):
    return pltpu.CompilerParams(dimension_semantics=sem, vmem_limit_bytes=VMEM_LIMIT)


def _sigmoid(x):
    return 1.0 / (1.0 + jnp.exp(-x))


def _silu(x):
    return x * _sigmoid(x)


def _gelu_tanh(x):
    c = math.sqrt(2.0 / math.pi)
    return x * (0.5 * (1.0 + jnp.tanh(c * (x + 0.044715 * (x * x * x)))))


def _dot_nt(a, b):
    return lax.dot_general(a, b, (((1,), (1,)), ((), ())), preferred_element_type=F32)


def _dot(a, b):
    return jnp.dot(a, b, preferred_element_type=F32)


def _layer_norm(z, g, b):
    mu = jnp.mean(z, axis=-1, keepdims=True)
    zc = z - mu
    var = jnp.mean(zc * zc, axis=-1, keepdims=True)
    return zc * lax.rsqrt(var + LN_EPS) * g + b


def _outproj_kernel(y1_ref, y2_ref, w1_ref, w2_ref, x_ref, g_ref, b_ref, o_ref, ob_ref, *, alpha):
    y = _dot(y1_ref[...], w1_ref[...]) + _dot(y2_ref[...], w2_ref[...])
    out = _layer_norm(alpha * x_ref[...] + y, g_ref[...], b_ref[...])
    o_ref[...] = out
    ob_ref[...] = out.astype(BF16)


def _outproj_ln(y1, y1_col, y2, y2_col, w_out, x, g, b, alpha, tm, name):
    m, d = x.shape
    kh = w_out.shape[0] // 2
    return pl.pallas_call(
        functools.partial(_outproj_kernel, alpha=alpha),
        out_shape=(jax.ShapeDtypeStruct((m, d), F32), jax.ShapeDtypeStruct((m, d), BF16)),
        grid=(m // tm,),
        in_specs=[pl.BlockSpec((tm, kh), lambda i: (i, y1_col)),
                  pl.BlockSpec((tm, kh), lambda i: (i, y2_col)),
                  pl.BlockSpec((kh, d), lambda i: (0, 0)),
                  pl.BlockSpec((kh, d), lambda i: (1, 0)),
                  pl.BlockSpec((tm, d), lambda i: (i, 0)),
                  pl.BlockSpec((1, d), lambda i: (0, 0)),
                  pl.BlockSpec((1, d), lambda i: (0, 0))],
        out_specs=(pl.BlockSpec((tm, d), lambda i: (i, 0)),
                   pl.BlockSpec((tm, d), lambda i: (i, 0))),
        compiler_params=_cparams("parallel"),
        name=name,
    )(y1, y2, w_out, w_out, x, g, b)


def _odd_kernel(x_ref, wv_ref, wu_ref, wz_ref, g_ref, b_ref, sw_ref, sb_ref, o_ref, vs_ref, vn_ref, *, tm):
    x = x_ref[...]

    @pl.when(pl.program_id(1) == 0)
    def _():
        for c in range(D_SGU // PROJ_CHUNK):
            cs = slice(c * PROJ_CHUNK, (c + 1) * PROJ_CHUNK)
            vs_ref[:, cs] = _gelu_tanh(_dot(x, wv_ref[:, cs]))
        vn = _layer_norm(vs_ref[...], g_ref[...], b_ref[...]).astype(BF16)
        for g in range(SGU_GROUPS):
            vn_ref[g] = vn[:, g * SGU_GROUP_DIM:(g + 1) * SGU_GROUP_DIM]

    u = _gelu_tanh(_dot(x, wu_ref[...]))
    z = _dot(x, wz_ref[...])
    row = lax.broadcasted_iota(jnp.int32, (SGU_CHUNK, SGU_CHUNK), 0)
    col = lax.broadcasted_iota(jnp.int32, (SGU_CHUNK, SGU_CHUNK), 1)
    wg = jnp.where(col <= row, sw_ref[0], 0.0).astype(BF16)
    bias = sb_ref[0]
    grp = pl.program_id(1)
    for c in range(tm // SGU_CHUNK):
        rs = slice(c * SGU_CHUNK, (c + 1) * SGU_CHUNK)
        mixed = _dot(wg, vn_ref[grp, rs, :]) + bias
        o_ref[rs, :] = (u[rs] * mixed * _silu(z[rs])).astype(BF16)


def _odd_mixer(xb, w_in, ln_g, ln_b, sgu_w, sgu_b, tm):
    m = xb.shape[0]
    gw = SGU_GROUP_DIM
    return pl.pallas_call(
        functools.partial(_odd_kernel, tm=tm),
        out_shape=jax.ShapeDtypeStruct((m, D_SGU), BF16),
        grid=(m // tm, SGU_GROUPS),
        in_specs=[pl.BlockSpec((tm, D_MODEL), lambda i, g: (i, 0)),
                  pl.BlockSpec((D_MODEL, D_SGU), lambda i, g: (0, 1)),
                  pl.BlockSpec((D_MODEL, gw), lambda i, g: (0, g)),
                  pl.BlockSpec((D_MODEL, gw), lambda i, g: (0, 2 * SGU_GROUPS + g)),
                  pl.BlockSpec((1, D_SGU), lambda i, g: (0, 0)),
                  pl.BlockSpec((1, D_SGU), lambda i, g: (0, 0)),
                  pl.BlockSpec((1, SGU_CHUNK, SGU_CHUNK), lambda i, g: (g, 0, 0)),
                  pl.BlockSpec((1, SGU_CHUNK, 1), lambda i, g: (g, 0, 0))],
        out_specs=pl.BlockSpec((tm, gw), lambda i, g: (i, g)),
        scratch_shapes=[pltpu.VMEM((tm, D_SGU), F32),
                        pltpu.VMEM((SGU_GROUPS, tm, gw), BF16)],
        compiler_params=_cparams("parallel", "arbitrary"),
        name="odd_mixer",
    )(xb, w_in, w_in, w_in, ln_g, ln_b, sgu_w, sgu_b)


def _conv_proj_kernel(x_ref, wh_ref, wb_ref, wc_ref, wz_ref, cw_ref, o_ref, w_ref, u_ref, *, tm, tiles_per_seq):
    i = pl.program_id(1)

    @pl.when(i == 0)
    def _():
        for k, src in enumerate((wh_ref, wb_ref, wc_ref, wz_ref)):
            w_ref[k] = src[...].astype(BF16)

    x = x_ref[...]
    u = _dot(x, w_ref[2]) * _dot(x, w_ref[0])

    @pl.when(i % tiles_per_seq == 0)
    def _():
        u_ref[0:SUBLANE, :] = jnp.zeros((SUBLANE, u_ref.shape[1]), F32)

    u_ref[SUBLANE:SUBLANE + tm, :] = u
    conv = cw_ref[CONV_WIDTH - 1:CONV_WIDTH, :] * u
    for k in range(CONV_WIDTH - 1):
        shift = CONV_WIDTH - 1 - k
        conv = conv + cw_ref[k:k + 1, :] * u_ref[SUBLANE - shift:SUBLANE - shift + tm, :]
    u_ref[0:SUBLANE, :] = u_ref[tm:tm + SUBLANE, :]
    o_ref[...] = (_dot(x, w_ref[1]) * conv * _silu(_dot(x, w_ref[3]))).astype(BF16)


def _conv_proj(xb, w_in, layer, conv_w, seq, tm, tc):
    m = xb.shape[0]
    nct = D_CONV // tc

    def wspec(part):
        return pl.BlockSpec((None, D_MODEL, tc), lambda j, i: (layer, 0, (EV_A + part * D_CONV) // tc + j))

    return pl.pallas_call(
        functools.partial(_conv_proj_kernel, tm=tm, tiles_per_seq=seq // tm),
        out_shape=jax.ShapeDtypeStruct((m, D_CONV), BF16),
        grid=(nct, m // tm),
        in_specs=[pl.BlockSpec((tm, D_MODEL), lambda j, i: (i, 0)),
                  wspec(0), wspec(1), wspec(2), wspec(3),
                  pl.BlockSpec((CONV_WIDTH, tc), lambda j, i: (0, j))],
        out_specs=pl.BlockSpec((tm, tc), lambda j, i: (i, j)),
        scratch_shapes=[pltpu.VMEM((4, D_MODEL, tc), BF16),
                        pltpu.VMEM((tm + SUBLANE, tc), F32)],
        compiler_params=_cparams("parallel", "arbitrary"),
        name="conv_proj",
    )(xb, w_in, w_in, w_in, w_in, conv_w)


def _nsa_proj_kernel(x_ref, w_ref, hb_ref, ha_ref):
    x = x_ref[...]
    nb = SLAB_W // PROJ_CHUNK
    for c in range(nb):
        cs = slice(c * PROJ_CHUNK, (c + 1) * PROJ_CHUNK)
        acc = _dot(x, w_ref[:, cs])
        if (c + 1) * PROJ_CHUNK <= HB_KS:
            acc = acc * ATTN_SCALE
        hb_ref[:, cs] = acc.astype(BF16)
    for c in range(nb):
        cs = slice(c * PROJ_CHUNK, (c + 1) * PROJ_CHUNK)
        ha_ref[:, cs] = _dot(x, w_ref[:, SLAB_W + c * PROJ_CHUNK:SLAB_W + (c + 1) * PROJ_CHUNK])


def _nsa_proj(xb, w, tm):
    m = xb.shape[0]
    return pl.pallas_call(
        _nsa_proj_kernel,
        out_shape=(jax.ShapeDtypeStruct((m, SLAB_W), BF16), jax.ShapeDtypeStruct((m, SLAB_W), F32)),
        grid=(m // tm,),
        in_specs=[pl.BlockSpec((tm, D_MODEL), lambda i: (i, 0)),
                  pl.BlockSpec((D_MODEL, 2 * SLAB_W), lambda i: (0, 0), pipeline_mode=pl.Buffered(1))],
        out_specs=(pl.BlockSpec((tm, SLAB_W), lambda i: (i, 0)),
                   pl.BlockSpec((tm, SLAB_W), lambda i: (i, 0))),
        compiler_params=_cparams("parallel"),
        name="nsa_proj",
    )(xb, w)


def _compress_kernel(tok_ref, w1_ref, w2_ref, pos_ref, o_ref, b_ref, *, rows):
    half = CMP_STRIDE * HEAD_DIM
    x2 = jnp.concatenate(
        [tok_ref[pl.ds(l, rows, stride=CMP_STRIDE), :] for l in range(CMP_STRIDE)], axis=1).astype(BF16)
    lo = _dot(x2, w1_ref[0, 0:half, :])
    hi = _dot(x2, w1_ref[0, half:2 * half, :])
    b_ref[0:rows, :] = hi
    b_ref[rows:rows + SUBLANE, :] = jnp.zeros((SUBLANE, HEAD_DIM), F32)
    posb = _dot(jnp.broadcast_to(pos_ref[0], (SUBLANE, 2 * half)), w1_ref[0])[0:1, :]
    pre = lo + b_ref[1:rows + 1, :] + posb
    o_ref[0, 0] = _dot(_silu(pre).astype(BF16), w2_ref[0]).astype(BF16)


def _compress(ha, w1, w2, pos, batch, seq):
    rows = seq // CMP_STRIDE
    nkv = 2 * NSA_KV_GROUPS
    kc_block = HA_KC // HEAD_DIM
    return pl.pallas_call(
        functools.partial(_compress_kernel, rows=rows),
        out_shape=jax.ShapeDtypeStruct((batch, nkv, rows, HEAD_DIM), BF16),
        grid=(batch, nkv),
        in_specs=[pl.BlockSpec((seq, HEAD_DIM), lambda b, c: (b, kc_block + c)),
                  pl.BlockSpec((1, CMP_BLOCK * HEAD_DIM, HEAD_DIM), lambda b, c: (c // NSA_KV_GROUPS, 0, 0)),
                  pl.BlockSpec((1, HEAD_DIM, HEAD_DIM), lambda b, c: (c // NSA_KV_GROUPS, 0, 0)),
                  pl.BlockSpec((1, 1, CMP_BLOCK * HEAD_DIM), lambda b, c: (c // NSA_KV_GROUPS, 0, 0))],
        out_specs=pl.BlockSpec((1, 1, rows, HEAD_DIM), lambda b, c: (b, c, 0, 0)),
        scratch_shapes=[pltpu.VMEM((rows + SUBLANE, HEAD_DIM), F32)],
        compiler_params=_cparams("parallel", "arbitrary"),
        name="cmp_blocks",
    )(ha, w1, w2, pos)


def _t5_bucket(dist):
    n = jnp.maximum(dist, 0)
    max_exact = REL_BUCKETS // 2
    large = max_exact + (jnp.log(jnp.maximum(n, 1).astype(F32) / max_exact)
                         / math.log(REL_MAX_DIST / max_exact) * (REL_BUCKETS - max_exact)).astype(jnp.int32)
    large = jnp.minimum(large, REL_BUCKETS - 1)
    return jnp.where(n < max_exact, n, large)


def _rel_bias(dist, valid, tab_ref, head):
    bkt = _t5_bucket(dist)
    acc = jnp.zeros(dist.shape, F32)
    for b in range(REL_BUCKETS):
        acc = jnp.where(bkt == b, tab_ref[b, head], acc)
    return jnp.where(valid, acc, NEG_INF)


def _bias_cmp_kernel(tab_ref, o_ref, *, tb, rows, n_cmp):
    head = pl.program_id(0)
    t = pl.program_id(1) * tb + lax.broadcasted_iota(jnp.int32, (tb, rows), 0)
    n = lax.broadcasted_iota(jnp.int32, (tb, rows), 1)
    dist = t - (n * CMP_STRIDE + CMP_BLOCK - 1)
    o_ref[0] = _rel_bias(dist, (dist >= 0) & (n < n_cmp), tab_ref, head)


def _bias_cmp(table, seq, tb):
    rows = seq // CMP_STRIDE
    n_cmp = (seq - CMP_BLOCK) // CMP_STRIDE + 1
    return pl.pallas_call(
        functools.partial(_bias_cmp_kernel, tb=tb, rows=rows, n_cmp=n_cmp),
        out_shape=jax.ShapeDtypeStruct((NSA_HEADS, seq, rows), F32),
        grid=(NSA_HEADS, seq // tb),
        in_specs=[pl.BlockSpec(memory_space=pltpu.SMEM)],
        out_specs=pl.BlockSpec((1, tb, rows), lambda h, i: (h, i, 0)),
        compiler_params=_cparams("parallel", "arbitrary"),
        name="bias_cmp",
    )(table)


def _bias_tiles_kernel(tab_ref, o_ref, *, tq):
    head = pl.program_id(0)
    kind = pl.program_id(1)
    ij = (lax.broadcasted_iota(jnp.int32, (tq, tq), 0) - lax.broadcasted_iota(jnp.int32, (tq, tq), 1))
    far = jnp.full((tq, tq), 2 * tq, jnp.int32)
    dist = jnp.where(kind == 0, ij, jnp.where(kind == 1, tq + ij, jnp.where(kind == 2, far, WINDOW + ij)))
    lo = jnp.where(kind == 0, 0, -tq)
    hi = jnp.where(kind == 3, 0, tq)
    o_ref[0, 0] = _rel_bias(dist, (ij >= lo) & (ij < hi), tab_ref, head)


def _bias_tiles(table, tq):
    assert 2 * tq <= WINDOW and WINDOW % tq == 0 and tq + 1 >= REL_MAX_DIST
    return pl.pallas_call(
        functools.partial(_bias_tiles_kernel, tq=tq),
        out_shape=jax.ShapeDtypeStruct((NSA_KV_GROUPS, N_BIAS_KINDS, NSA_HPG * tq, tq), F32),
        grid=(NSA_HEADS, N_BIAS_KINDS),
        in_specs=[pl.BlockSpec(memory_space=pltpu.SMEM)],
        out_specs=pl.BlockSpec((1, 1, tq, tq), lambda h, k: (h // NSA_HPG, k, h % NSA_HPG, 0)),
        compiler_params=_cparams("parallel", "arbitrary"),
        name="bias_tiles",
    )(table)


def _cmp_attn_kernel(q_ref, kc_ref, vc_ref, bias_ref, ov_ref, oc_ref, sel_ref, *, tq, n_sel, n_top):
    kc = kc_ref[0, 0]
    vc = vc_ref[0, 0]
    psum = None
    for j in range(NSA_HPG):
        hs = slice(j * HEAD_DIM, (j + 1) * HEAD_DIM)
        bias = bias_ref[j]
        s = _dot_nt(q_ref[:, hs], kc) + bias
        m = jnp.max(s, axis=-1, keepdims=True)
        e = jnp.exp(s - m)
        p = e / jnp.sum(e, axis=-1, keepdims=True)
        p = jnp.where(bias > 0.5 * NEG_INF, p, 0.0)
        oc_ref[:, hs] = _dot(p.astype(BF16), vc)
        psum = p if psum is None else psum + p
    ov = ov_ref[...]
    imp = None
    rem = psum
    for _ in range(3):
        piece = rem.astype(BF16)
        part = _dot_nt(ov, piece)
        imp = part if imp is None else imp + part
        rem = rem - piece.astype(F32)
    t = pl.program_id(1) * tq + lax.broadcasted_iota(jnp.int32, (n_sel, tq), 1)
    cur = jnp.right_shift(t, int(math.log2(SEL_BLOCK)))
    blk = lax.broadcasted_iota(jnp.int32, (n_sel, tq), 0)
    forced = (blk == 0) | (blk == cur) | (blk == cur - 1)
    imp = jnp.where(blk > cur, -1.0, jnp.where(forced, FORCED_SCORE, imp))
    rank = jnp.zeros((n_sel, tq), jnp.int32)
    for i in range(n_sel):
        row = imp[i:i + 1, :]
        ahead = (row > imp) | ((row == imp) & (blk > i))
        rank = rank + ahead.astype(jnp.int32)
    sel_t = jnp.where(rank < n_top, 0.0, NEG_INF)
    if n_sel < LANE:
        sel_t = jnp.concatenate([sel_t, jnp.zeros((LANE - n_sel, tq), F32)], axis=0)
    sel_ref[0, 0] = sel_t.T.astype(BF16)


def _cmp_attn(hb, kcv, bias_c, ov, batch, seq, tq):
    rows = seq // CMP_STRIDE
    n_sel = seq // SEL_BLOCK
    n_top = min(N_SELECT, n_sel)
    assert n_sel <= LANE and n_sel % SUBLANE == 0
    nq = seq // tq
    g_n = NSA_KV_GROUPS
    return pl.pallas_call(
        functools.partial(_cmp_attn_kernel, tq=tq, n_sel=n_sel, n_top=n_top),
        out_shape=(jax.ShapeDtypeStruct((batch * seq, D_NSA), F32),
                   jax.ShapeDtypeStruct((batch, g_n, seq, LANE), BF16)),
        grid=(g_n, nq, batch),
        in_specs=[pl.BlockSpec((tq, GROUP_W), lambda g, i, b: (b * nq + i, HB_Q // GROUP_W + g)),
                  pl.BlockSpec((1, 1, rows, HEAD_DIM), lambda g, i, b: (b, g, 0, 0)),
                  pl.BlockSpec((1, 1, rows, HEAD_DIM), lambda g, i, b: (b, g_n + g, 0, 0)),
                  pl.BlockSpec((NSA_HPG, tq, rows), lambda g, i, b: (g, i, 0)),
                  pl.BlockSpec((n_sel, rows), lambda g, i, b: (0, 0))],
        out_specs=(pl.BlockSpec((tq, GROUP_W), lambda g, i, b: (b * nq + i, g)),
                   pl.BlockSpec((1, 1, tq, LANE), lambda g, i, b: (b, g, i, 0))),
        compiler_params=_cparams("parallel", "parallel", "arbitrary"),
        name="cmp_attn",
    )(hb, kcv, kcv, bias_c, ov)


def _flash_step(q, k, v, bias, carry):
    m, l, acc = carry
    s = _dot_nt(q, k) + bias
    m_new = jnp.maximum(m, jnp.max(s, axis=-1, keepdims=True))
    alpha = jnp.exp(m - m_new)
    p = jnp.exp(s - m_new)
    l = alpha * l + jnp.sum(p, axis=-1, keepdims=True)
    acc = alpha * acc + _dot(p.astype(BF16), v)
    return m_new, l, acc


def _selwin_kernel(q_ref, ks_ref, vs_ref, kw_ref, vw_ref, sel_ref, bias_ref, oc_ref, gt_ref, bz_ref, o_ref,
                   *, tq):
    qi = pl.program_id(2)
    rows = NSA_HPG * tq
    q4 = jnp.concatenate([q_ref[:, j * HEAD_DIM:(j + 1) * HEAD_DIM] for j in range(NSA_HPG)], axis=0)
    sel4 = jnp.concatenate([sel_ref[0, 0]] * NSA_HPG, axis=0)
    qa = jnp.concatenate([q4, sel4], axis=1)
    krow = lax.broadcasted_iota(jnp.int32, (tq, LANE), 0)
    kcol = lax.broadcasted_iota(jnp.int32, (tq, LANE), 1)
    shift = int(math.log2(SEL_BLOCK))
    init = (jnp.full((rows, 1), -3e38, F32), jnp.zeros((rows, 1), F32), jnp.zeros((rows, HEAD_DIM), F32))

    def sel_body(kt, carry):
        off = pl.multiple_of(kt * tq, tq)
        onehot = jnp.where(jnp.right_shift(off + krow, shift) == kcol, 1.0, 0.0).astype(BF16)
        ka = jnp.concatenate([ks_ref[pl.ds(off, tq), :], onehot], axis=1)
        bias = bias_ref[0, jnp.minimum(qi - kt, 2)]
        return _flash_step(qa, ka, vs_ref[pl.ds(off, tq), :], bias, carry)

    _, l_s, acc_s = lax.fori_loop(0, qi + 1, sel_body, init)

    nwin = WINDOW // tq

    def win_body(kt, carry):
        off = pl.multiple_of(kt * tq, tq)
        d = qi - kt
        bias = bias_ref[0, jnp.where(d == nwin, 3, jnp.minimum(d, 2))]
        return _flash_step(q4, kw_ref[pl.ds(off, tq), :], vw_ref[pl.ds(off, tq), :], bias, carry)

    _, l_w, acc_w = lax.fori_loop(jnp.maximum(qi - nwin, 0), qi + 1, win_body, init)

    o_s = acc_s / l_s
    o_w = acc_w / l_w
    gate = _sigmoid(gt_ref[...])
    for j in range(NSA_HPG):
        hs = slice(j * HEAD_DIM, (j + 1) * HEAD_DIM)
        rs = slice(j * tq, (j + 1) * tq)
        o = (gate[:, j:j + 1] * oc_ref[:, hs]
             + gate[:, NSA_HPG + j:NSA_HPG + j + 1] * o_s[rs, :]
             + gate[:, 2 * NSA_HPG + j:2 * NSA_HPG + j + 1] * o_w[rs, :])
        o_ref[:, hs] = (o * _silu(bz_ref[:, hs])).astype(BF16)


def _selwin_attn(hb, ha, sel, bias_t, o_c, batch, seq, tq):
    nq = seq // tq
    kv_block = lambda base: (lambda b, g, i: (b, base // HEAD_DIM + g))
    row_g = lambda b, g, i: (b * nq + i, g)
    return pl.pallas_call(
        functools.partial(_selwin_kernel, tq=tq),
        out_shape=jax.ShapeDtypeStruct((batch * seq, D_NSA), BF16),
        grid=(batch, NSA_KV_GROUPS, nq),
        in_specs=[pl.BlockSpec((tq, GROUP_W), lambda b, g, i: (b * nq + i, HB_Q // GROUP_W + g)),
                  pl.BlockSpec((seq, HEAD_DIM), kv_block(HB_KS)),
                  pl.BlockSpec((seq, HEAD_DIM), kv_block(HB_VS)),
                  pl.BlockSpec((seq, HEAD_DIM), kv_block(HB_KW)),
                  pl.BlockSpec((seq, HEAD_DIM), kv_block(HB_VW)),
                  pl.BlockSpec((1, 1, tq, LANE), lambda b, g, i: (b, g, i, 0)),
                  pl.BlockSpec((1, N_BIAS_KINDS, NSA_HPG * tq, tq), lambda b, g, i: (g, 0, 0, 0)),
                  pl.BlockSpec((tq, GROUP_W), row_g),
                  pl.BlockSpec((tq, LANE), lambda b, g, i: (b * nq + i, HA_GT // LANE + g)),
                  pl.BlockSpec((tq, GROUP_W), lambda b, g, i: (b * nq + i, HA_BZ // GROUP_W + g))],
        out_specs=pl.BlockSpec((tq, GROUP_W), row_g),
        compiler_params=_cparams("parallel", "parallel", "arbitrary"),
        name="selwin_attn",
    )(hb, hb, hb, hb, hb, sel, bias_t, o_c, ha, ha)


def _nsa_proj_weights(w):
    w_g = w[:, EV_GT:EV_BZ].reshape(D_MODEL, N_BRANCH, NSA_KV_GROUPS, NSA_HPG)
    w_g = jnp.transpose(w_g, (0, 2, 1, 3)).reshape(D_MODEL, NSA_KV_GROUPS, N_BRANCH * NSA_HPG)
    w_g = jnp.pad(w_g, ((0, 0), (0, 0), (0, LANE - N_BRANCH * NSA_HPG))).reshape(D_MODEL, NSA_KV_GROUPS * LANE)
    w_g = jnp.pad(w_g, ((0, 0), (0, SLAB_W - HA_GT - NSA_KV_GROUPS * LANE)))
    return jnp.concatenate([w[:, EV_Q:EV_KC], w[:, EV_KS:EV_GT], w[:, EV_KC:EV_KS], w[:, EV_BZ:EV_END], w_g],
                           axis=1).astype(BF16)


def _overlap_matrix(seq):
    rows = seq // CMP_STRIDE
    n_cmp = (seq - CMP_BLOCK) // CMP_STRIDE + 1
    n_sel = seq // SEL_BLOCK
    cstart = np.arange(rows)[None, :] * CMP_STRIDE
    sstart = np.arange(n_sel)[:, None] * SEL_BLOCK
    ov = (cstart < sstart + SEL_BLOCK) & (cstart + CMP_BLOCK > sstart) & (np.arange(rows)[None, :] < n_cmp)
    return jnp.asarray(ov.astype(np.float32), dtype=BF16)


def kernel(x, rel_bias_table, ln_g, ln_b, ev_w_in, ev_conv_w, ev_cmp_pos, ev_cmp_w1, ev_cmp_w2, ev_w_out,
           od_w_in, od_ln_g, od_ln_b, od_sgu_w, od_sgu_b, od_w_out):
    batch, seq, d = x.shape
    depth = ln_g.shape[0]
    alpha = (2 * depth) ** 0.25
    m = batch * seq
    tq = ATT_TILE
    assert d == D_MODEL and seq % 1024 == 0 and ev_w_in.shape[-1] == EV_END

    xf = x.reshape(m, d)
    xb = xf.astype(BF16)
    bias_c = _bias_cmp(rel_bias_table, seq, tb=256)
    bias_t = _bias_tiles(rel_bias_table, tq)
    ov = _overlap_matrix(seq)

    for layer in range(depth):
        i = layer // 2
        g = ln_g[layer].reshape(1, d)
        b = ln_b[layer].reshape(1, d)
        if layer % 2 == 0:
            y_a = _conv_proj(xb, ev_w_in, i, ev_conv_w[i], seq, tm=1024, tc=256)
            hb, ha = _nsa_proj(xb, _nsa_proj_weights(ev_w_in[i]), tm=512)
            kcv = _compress(ha, ev_cmp_w1[i].astype(BF16), ev_cmp_w2[i].astype(BF16),
                            ev_cmp_pos[i].reshape(2, 1, CMP_BLOCK * HEAD_DIM).astype(BF16), batch, seq)
            o_c, sel = _cmp_attn(hb, kcv, bias_c, ov, batch, seq, tq)
            y_b = _selwin_attn(hb, ha, sel, bias_t, o_c, batch, seq, tq)
            xf, xb = _outproj_ln(y_a, 0, y_b, 0, ev_w_out[i].astype(BF16), xf, g, b, alpha, 256, "outproj_even")
        else:
            y = _odd_mixer(xb, od_w_in[i].astype(BF16), od_ln_g[i].reshape(1, d), od_ln_b[i].reshape(1, d),
                           od_sgu_w[i], od_sgu_b[i].reshape(SGU_GROUPS, SGU_CHUNK, 1), tm=1024)
            xf, xb = _outproj_ln(y, 0, y, 1, od_w_out[i].astype(BF16), xf, g, b, alpha, 256, "outproj_odd")
    return xf.reshape(batch, seq, d)
```

```python
import functools
import math

import numpy as np
import jax
import jax.numpy as jnp
from jax import lax
from jax.experimental import pallas as pl
from jax.experimental.pallas import tpu as pltpu

F32 = jnp.float32
BF16 = jnp.bfloat16

D_MODEL = 2048
CONV_WIDTH = 3
D_CONV = 1024
NSA_HEADS = 8
NSA_KV_GROUPS = 2
NSA_HPG = NSA_HEADS // NSA_KV_GROUPS
HEAD_DIM = 128
D_NSA = NSA_HEADS * HEAD_DIM
D_KV = NSA_KV_GROUPS * HEAD_DIM
CMP_BLOCK = 32
CMP_STRIDE = 16
SEL_BLOCK = 64
N_SELECT = 16
WINDOW = 512
N_BRANCH = 3
D_SGU = D_MODEL
SGU_GROUPS = 8
SGU_CHUNK = 128
SGU_GROUP_DIM = D_SGU // SGU_GROUPS
REL_BUCKETS = 32
REL_MAX_DIST = 128
LN_EPS = 1e-5
NEG_INF = -1e30
FORCED_SCORE = 1e9
GROUP_W = NSA_HPG * HEAD_DIM
LOG2E = math.log2(math.e)
Q_SCALE = HEAD_DIM ** -0.5 * LOG2E

LANE = 128
SUBLANE = 8
VMEM_LIMIT = 56 * 1024 * 1024

EV_A = 0
EV_Q = 4 * D_CONV
EV_KC = EV_Q + D_NSA
EV_KS = EV_KC + 2 * D_KV
EV_GT = EV_KS + 4 * D_KV
EV_BZ = EV_GT + N_BRANCH * NSA_HEADS
EV_END = EV_BZ + D_NSA

HB_Q, HB_KS, HB_VS, HB_KW, HB_VW = 0, 1024, 1280, 1536, 1792
HA_KC, HA_BZ, HA_GT = 0, 512, 1536
SLAB_W = 2048
PROJ_CHUNK = 512

CMP_TILE = 256
ATT_TILE = 512
KIND_DIAG, KIND_SUB, KIND_CORNER = 0, 1, 2
N_BIAS_KINDS = 3


def _cparams(*sem):
    return pltpu.CompilerParams(dimension_semantics=sem, vmem_limit_bytes=VMEM_LIMIT)


def _sigmoid(x):
    return 1.0 / (1.0 + jnp.exp(-x))


def _silu(x):
    return x * _sigmoid(x)


def _gelu_tanh(x):
    c = math.sqrt(2.0 / math.pi)
    return x * (0.5 * (1.0 + jnp.tanh(c * (x + 0.044715 * (x * x * x)))))


def _dot_nt(a, b):
    return lax.dot_general(a, b, (((1,), (1,)), ((), ())), preferred_element_type=F32)


def _dot(a, b):
    return jnp.dot(a, b, preferred_element_type=F32)


def _layer_norm(z, g, b):
    mu = jnp.mean(z, axis=-1, keepdims=True)
    zc = z - mu
    var = jnp.mean(zc * zc, axis=-1, keepdims=True)
    return zc * lax.rsqrt(var + LN_EPS) * g + b


def _outproj_kernel(y1_ref, y2_ref, w1_ref, w2_ref, x_ref, g_ref, b_ref, o_ref, ob_ref, *, alpha):
    y = _dot(y1_ref[...], w1_ref[...]) + _dot(y2_ref[...], w2_ref[...])
    out = _layer_norm(alpha * x_ref[...] + y, g_ref[...], b_ref[...])
    o_ref[...] = out
    ob_ref[...] = out.astype(BF16)


def _outproj_ln(y1, y1_col, y2, y2_col, w_out, x, g, b, alpha, tm, name):
    m, d = x.shape
    kh = w_out.shape[0] // 2
    return pl.pallas_call(
        functools.partial(_outproj_kernel, alpha=alpha),
        out_shape=(jax.ShapeDtypeStruct((m, d), F32), jax.ShapeDtypeStruct((m, d), BF16)),
        grid=(m // tm,),
        in_specs=[pl.BlockSpec((tm, kh), lambda i: (i, y1_col)),
                  pl.BlockSpec((tm, kh), lambda i: (i, y2_col)),
                  pl.BlockSpec((kh, d), lambda i: (0, 0)),
                  pl.BlockSpec((kh, d), lambda i: (1, 0)),
                  pl.BlockSpec((tm, d), lambda i: (i, 0)),
                  pl.BlockSpec((1, d), lambda i: (0, 0)),
                  pl.BlockSpec((1, d), lambda i: (0, 0))],
        out_specs=(pl.BlockSpec((tm, d), lambda i: (i, 0)),
                   pl.BlockSpec((tm, d), lambda i: (i, 0))),
        compiler_params=_cparams("parallel"),
        name=name,
    )(y1, y2, w_out, w_out, x, g, b)


def _odd_kernel(x_ref, wv_ref, wu_ref, wz_ref, g_ref, b_ref, sw_ref, sb_ref, o_ref, vs_ref, vn_ref, *, tm):
    x = x_ref[...]

    @pl.when(pl.program_id(1) == 0)
    def _():
        for c in range(D_SGU // PROJ_CHUNK):
            cs = slice(c * PROJ_CHUNK, (c + 1) * PROJ_CHUNK)
            vs_ref[:, cs] = _gelu_tanh(_dot(x, wv_ref[:, cs]))
        vn = _layer_norm(vs_ref[...], g_ref[...], b_ref[...]).astype(BF16)
        for g in range(SGU_GROUPS):
            vn_ref[g] = vn[:, g * SGU_GROUP_DIM:(g + 1) * SGU_GROUP_DIM]

    u = _gelu_tanh(_dot(x, wu_ref[...]))
    z = _dot(x, wz_ref[...])
    row = lax.broadcasted_iota(jnp.int32, (SGU_CHUNK, SGU_CHUNK), 0)
    col = lax.broadcasted_iota(jnp.int32, (SGU_CHUNK, SGU_CHUNK), 1)
    wg = jnp.where(col <= row, sw_ref[0], 0.0).astype(BF16)
    bias = sb_ref[0]
    grp = pl.program_id(1)
    for c in range(tm // SGU_CHUNK):
        rs = slice(c * SGU_CHUNK, (c + 1) * SGU_CHUNK)
        mixed = _dot(wg, vn_ref[grp, rs, :]) + bias
        o_ref[rs, :] = (u[rs] * mixed * _silu(z[rs])).astype(BF16)


def _odd_mixer(xb, w_in, ln_g, ln_b, sgu_w, sgu_b, tm):
    m = xb.shape[0]
    gw = SGU_GROUP_DIM
    return pl.pallas_call(
        functools.partial(_odd_kernel, tm=tm),
        out_shape=jax.ShapeDtypeStruct((m, D_SGU), BF16),
        grid=(m // tm, SGU_GROUPS),
        in_specs=[pl.BlockSpec((tm, D_MODEL), lambda i, g: (i, 0)),
                  pl.BlockSpec((D_MODEL, D_SGU), lambda i, g: (0, 1)),
                  pl.BlockSpec((D_MODEL, gw), lambda i, g: (0, g)),
                  pl.BlockSpec((D_MODEL, gw), lambda i, g: (0, 2 * SGU_GROUPS + g)),
                  pl.BlockSpec((1, D_SGU), lambda i, g: (0, 0)),
                  pl.BlockSpec((1, D_SGU), lambda i, g: (0, 0)),
                  pl.BlockSpec((1, SGU_CHUNK, SGU_CHUNK), lambda i, g: (g, 0, 0)),
                  pl.BlockSpec((1, SGU_CHUNK, 1), lambda i, g: (g, 0, 0))],
        out_specs=pl.BlockSpec((tm, gw), lambda i, g: (i, g)),
        scratch_shapes=[pltpu.VMEM((tm, D_SGU), F32),
                        pltpu.VMEM((SGU_GROUPS, tm, gw), BF16)],
        compiler_params=_cparams("parallel", "arbitrary"),
        name="odd_mixer",
    )(xb, w_in, w_in, w_in, ln_g, ln_b, sgu_w, sgu_b)


def _conv_proj_kernel(x_ref, wh_ref, wb_ref, wc_ref, wz_ref, cw_ref, o_ref, u_ref, *, tm, tiles_per_seq):
    i = pl.program_id(1)
    x = x_ref[...]
    u = _dot(x, wc_ref[...]) * _dot(x, wh_ref[...])

    @pl.when(i % tiles_per_seq == 0)
    def _():
        u_ref[0:SUBLANE, :] = jnp.zeros((SUBLANE, u_ref.shape[1]), F32)

    u_ref[SUBLANE:SUBLANE + tm, :] = u
    conv = cw_ref[CONV_WIDTH - 1:CONV_WIDTH, :] * u
    for k in range(CONV_WIDTH - 1):
        shift = CONV_WIDTH - 1 - k
        conv = conv + cw_ref[k:k + 1, :] * u_ref[SUBLANE - shift:SUBLANE - shift + tm, :]
    u_ref[0:SUBLANE, :] = u_ref[tm:tm + SUBLANE, :]
    o_ref[...] = (_dot(x, wb_ref[...]) * conv * _silu(_dot(x, wz_ref[...]))).astype(BF16)


def _conv_proj(xb, w_a, conv_w, seq, tm, tc):
    m = xb.shape[0]
    nct = D_CONV // tc

    def wspec(part):
        return pl.BlockSpec((D_MODEL, tc), lambda j, i: (0, part * nct + j))

    return pl.pallas_call(
        functools.partial(_conv_proj_kernel, tm=tm, tiles_per_seq=seq // tm),
        out_shape=jax.ShapeDtypeStruct((m, D_CONV), BF16),
        grid=(nct, m // tm),
        in_specs=[pl.BlockSpec((tm, D_MODEL), lambda j, i: (i, 0)),
                  wspec(0), wspec(1), wspec(2), wspec(3),
                  pl.BlockSpec((CONV_WIDTH, tc), lambda j, i: (0, j))],
        out_specs=pl.BlockSpec((tm, tc), lambda j, i: (i, j)),
        scratch_shapes=[pltpu.VMEM((tm + SUBLANE, tc), F32)],
        compiler_params=_cparams("parallel", "arbitrary"),
        name="conv_proj",
    )(xb, w_a, w_a, w_a, w_a, conv_w)


def _nsa_proj_kernel(x_ref, w_ref, hb_ref, ha_ref):
    x = x_ref[...]
    nb = SLAB_W // PROJ_CHUNK
    for c in range(nb):
        cs = slice(c * PROJ_CHUNK, (c + 1) * PROJ_CHUNK)
        acc = _dot(x, w_ref[:, cs])
        if (c + 1) * PROJ_CHUNK <= HB_KS:
            acc = acc * Q_SCALE
        hb_ref[:, cs] = acc.astype(BF16)
    for c in range(nb):
        cs = slice(c * PROJ_CHUNK, (c + 1) * PROJ_CHUNK)
        ha_ref[:, cs] = _dot(x, w_ref[:, SLAB_W + c * PROJ_CHUNK:SLAB_W + (c + 1) * PROJ_CHUNK])


def _nsa_proj(xb, w, tm):
    m = xb.shape[0]
    return pl.pallas_call(
        _nsa_proj_kernel,
        out_shape=(jax.ShapeDtypeStruct((m, SLAB_W), BF16), jax.ShapeDtypeStruct((m, SLAB_W), F32)),
        grid=(m // tm,),
        in_specs=[pl.BlockSpec((tm, D_MODEL), lambda i: (i, 0)),
                  pl.BlockSpec((D_MODEL, 2 * SLAB_W), lambda i: (0, 0), pipeline_mode=pl.Buffered(1))],
        out_specs=(pl.BlockSpec((tm, SLAB_W), lambda i: (i, 0)),
                   pl.BlockSpec((tm, SLAB_W), lambda i: (i, 0))),
        compiler_params=_cparams("parallel"),
        name="nsa_proj",
    )(xb, w)


def _compress_kernel(tok_ref, w1_ref, w2_ref, pos_ref, o_ref, b_ref, *, rows):
    half = CMP_STRIDE * HEAD_DIM
    x2 = jnp.concatenate(
        [tok_ref[pl.ds(l, rows, stride=CMP_STRIDE), :] for l in range(CMP_STRIDE)], axis=1).astype(BF16)
    lo = _dot(x2, w1_ref[0, 0:half, :])
    hi = _dot(x2, w1_ref[0, half:2 * half, :])
    b_ref[0:rows, :] = hi
    b_ref[rows:rows + SUBLANE, :] = jnp.zeros((SUBLANE, HEAD_DIM), F32)
    posb = _dot(jnp.broadcast_to(pos_ref[0], (SUBLANE, 2 * half)), w1_ref[0])[0:1, :]
    pre = lo + b_ref[1:rows + 1, :] + posb
    o_ref[0, 0] = _dot(_silu(pre).astype(BF16), w2_ref[0]).astype(BF16)


def _compress(ha, w1, w2, pos, batch, seq):
    rows = seq // CMP_STRIDE
    nkv = 2 * NSA_KV_GROUPS
    kc_block = HA_KC // HEAD_DIM
    return pl.pallas_call(
        functools.partial(_compress_kernel, rows=rows),
        out_shape=jax.ShapeDtypeStruct((batch, nkv, rows, HEAD_DIM), BF16),
        grid=(batch, nkv),
        in_specs=[pl.BlockSpec((seq, HEAD_DIM), lambda b, c: (b, kc_block + c)),
                  pl.BlockSpec((1, CMP_BLOCK * HEAD_DIM, HEAD_DIM), lambda b, c: (c // NSA_KV_GROUPS, 0, 0)),
                  pl.BlockSpec((1, HEAD_DIM, HEAD_DIM), lambda b, c: (c // NSA_KV_GROUPS, 0, 0)),
                  pl.BlockSpec((1, 1, CMP_BLOCK * HEAD_DIM), lambda b, c: (c // NSA_KV_GROUPS, 0, 0))],
        out_specs=pl.BlockSpec((1, 1, rows, HEAD_DIM), lambda b, c: (b, c, 0, 0)),
        scratch_shapes=[pltpu.VMEM((rows + SUBLANE, HEAD_DIM), F32)],
        compiler_params=_cparams("parallel", "arbitrary"),
        name="cmp_blocks",
    )(ha, w1, w2, pos)


def _t5_bucket(dist):
    n = jnp.maximum(dist, 0)
    max_exact = REL_BUCKETS // 2
    large = max_exact + (jnp.log(jnp.maximum(n, 1).astype(F32) / max_exact)
                         / math.log(REL_MAX_DIST / max_exact) * (REL_BUCKETS - max_exact)).astype(jnp.int32)
    large = jnp.minimum(large, REL_BUCKETS - 1)
    return jnp.where(n < max_exact, n, large)


def _rel_bias(dist, valid, tab_ref, head):
    bkt = _t5_bucket(dist)
    acc = jnp.zeros(dist.shape, F32)
    for b in range(REL_BUCKETS):
        acc = jnp.where(bkt == b, tab_ref[b, head], acc)
    return jnp.where(valid, acc, NEG_INF)


def _bias_cmp_kernel(tab_ref, o_ref, *, tb, rows, n_cmp):
    head = pl.program_id(0)
    t = pl.program_id(1) * tb + lax.broadcasted_iota(jnp.int32, (tb, rows), 0)
    n = lax.broadcasted_iota(jnp.int32, (tb, rows), 1)
    dist = t - (n * CMP_STRIDE + CMP_BLOCK - 1)
    o_ref[0] = _rel_bias(dist, (dist >= 0) & (n < n_cmp), tab_ref, head) * LOG2E


def _bias_cmp(table, seq, tb):
    rows = seq // CMP_STRIDE
    n_cmp = (seq - CMP_BLOCK) // CMP_STRIDE + 1
    return pl.pallas_call(
        functools.partial(_bias_cmp_kernel, tb=tb, rows=rows, n_cmp=n_cmp),
        out_shape=jax.ShapeDtypeStruct((NSA_HEADS, seq, rows), F32),
        grid=(NSA_HEADS, seq // tb),
        in_specs=[pl.BlockSpec(memory_space=pltpu.SMEM)],
        out_specs=pl.BlockSpec((1, tb, rows), lambda h, i: (h, i, 0)),
        compiler_params=_cparams("parallel", "arbitrary"),
        name="bias_cmp",
    )(table)


def _bias_tiles_kernel(tab_ref, o_ref, *, tq):
    head = pl.program_id(0)
    kind = pl.program_id(1)
    ij = (lax.broadcasted_iota(jnp.int32, (tq, tq), 0) - lax.broadcasted_iota(jnp.int32, (tq, tq), 1))
    dist = jnp.where(kind == KIND_DIAG, ij, jnp.where(kind == KIND_SUB, tq + ij, WINDOW + ij))
    lo = jnp.where(kind == KIND_DIAG, 0, -tq)
    hi = jnp.where(kind == KIND_CORNER, 0, tq)
    far_dist = jnp.full((SUBLANE, LANE), tq + 1, jnp.int32)
    far = _rel_bias(far_dist, far_dist > 0, tab_ref, head)[0:1, 0:1]
    o_ref[0, 0] = (_rel_bias(dist, (ij >= lo) & (ij < hi), tab_ref, head) - far) * LOG2E


def _bias_tiles(table, tq):
    assert WINDOW % tq == 0 and tq + 1 >= REL_MAX_DIST
    return pl.pallas_call(
        functools.partial(_bias_tiles_kernel, tq=tq),
        out_shape=jax.ShapeDtypeStruct((NSA_KV_GROUPS, N_BIAS_KINDS, NSA_HPG * tq, tq), F32),
        grid=(NSA_HEADS, N_BIAS_KINDS),
        in_specs=[pl.BlockSpec(memory_space=pltpu.SMEM)],
        out_specs=pl.BlockSpec((1, 1, tq, tq), lambda h, k: (h // NSA_HPG, k, h % NSA_HPG, 0)),
        compiler_params=_cparams("parallel", "arbitrary"),
        name="bias_tiles",
    )(table)


def _cmp_attn_kernel(q_ref, kc_ref, vc_ref, bias_ref, ov_ref, oc_ref, sel_ref, *, tq, n_sel, n_top):
    kc = kc_ref[0, 0]
    vc = vc_ref[0, 0]
    psum = None
    for j in range(NSA_HPG):
        hs = slice(j * HEAD_DIM, (j + 1) * HEAD_DIM)
        bias = bias_ref[j]
        s = _dot_nt(q_ref[:, hs], kc) + bias
        m = jnp.max(s, axis=-1, keepdims=True)
        e = jnp.exp2(s - m)
        p = e / jnp.sum(e, axis=-1, keepdims=True)
        p = jnp.where(bias > 0.5 * NEG_INF, p, 0.0)
        oc_ref[:, hs] = _dot(p.astype(BF16), vc)
        psum = p if psum is None else psum + p
    ov = ov_ref[...]
    imp = None
    rem = psum
    for _ in range(3):
        piece = rem.astype(BF16)
        part = _dot_nt(ov, piece)
        imp = part if imp is None else imp + part
        rem = rem - piece.astype(F32)
    t = pl.program_id(1) * tq + lax.broadcasted_iota(jnp.int32, (n_sel, tq), 1)
    cur = jnp.right_shift(t, int(math.log2(SEL_BLOCK)))
    blk = lax.broadcasted_iota(jnp.int32, (n_sel, tq), 0)
    forced = (blk == 0) | (blk == cur) | (blk == cur - 1)
    imp = jnp.where(blk > cur, -1.0, jnp.where(forced, FORCED_SCORE, imp))
    groups = [imp[g * SUBLANE:(g + 1) * SUBLANE, :] for g in range(n_sel // SUBLANE)]
    ranks = [jnp.zeros((SUBLANE, tq), jnp.int32) for _ in groups]
    sub = lax.broadcasted_iota(jnp.int32, (SUBLANE, tq), 0)
    for i in range(n_sel):
        row = imp[i:i + 1, :]
        for g, x in enumerate(groups):
            if g * SUBLANE > i:
                ahead = row >= x
            elif (g + 1) * SUBLANE - 1 < i:
                ahead = row > x
            else:
                ahead = (row > x) | ((row == x) & (sub > i - g * SUBLANE))
            ranks[g] = ranks[g] + ahead.astype(jnp.int32)
    sel_t = jnp.where(jnp.concatenate(ranks, axis=0) < n_top, 0.0, NEG_INF)
    if n_sel < LANE:
        sel_t = jnp.concatenate([sel_t, jnp.zeros((LANE - n_sel, tq), F32)], axis=0)
    sel_ref[0, 0] = sel_t.T.astype(BF16)


def _cmp_attn(hb, kcv, bias_c, ov, batch, seq, tq):
    rows = seq // CMP_STRIDE
    n_sel = seq // SEL_BLOCK
    n_top = min(N_SELECT, n_sel)
    assert n_sel <= LANE and n_sel % SUBLANE == 0
    nq = seq // tq
    g_n = NSA_KV_GROUPS
    return pl.pallas_call(
        functools.partial(_cmp_attn_kernel, tq=tq, n_sel=n_sel, n_top=n_top),
        out_shape=(jax.ShapeDtypeStruct((batch * seq, D_NSA), F32),
                   jax.ShapeDtypeStruct((batch, g_n, seq, LANE), BF16)),
        grid=(g_n, nq, batch),
        in_specs=[pl.BlockSpec((tq, GROUP_W), lambda g, i, b: (b * nq + i, HB_Q // GROUP_W + g)),
                  pl.BlockSpec((1, 1, rows, HEAD_DIM), lambda g, i, b: (b, g, 0, 0)),
                  pl.BlockSpec((1, 1, rows, HEAD_DIM), lambda g, i, b: (b, g_n + g, 0, 0)),
                  pl.BlockSpec((NSA_HPG, tq, rows), lambda g, i, b: (g, i, 0)),
                  pl.BlockSpec((n_sel, rows), lambda g, i, b: (0, 0))],
        out_specs=(pl.BlockSpec((tq, GROUP_W), lambda g, i, b: (b * nq + i, g)),
                   pl.BlockSpec((1, 1, tq, LANE), lambda g, i, b: (b, g, i, 0))),
        compiler_params=_cparams("parallel", "parallel", "arbitrary"),
        name="cmp_attn",
    )(hb, kcv, kcv, bias_c, ov)


FLASH_ROWS = 128


def _flash_init(state):
    _, _, m_ref, l_ref, _, acc_ref = state
    m_ref[...] = jnp.full(m_ref.shape, -3e38, F32)
    l_ref[...] = jnp.zeros(l_ref.shape, F32)
    acc_ref[...] = jnp.zeros(acc_ref.shape, F32)


def _flash_tile(qa_ref, kdim, k, v, bias_ref, kind, state, tq):
    s_ref, p_ref, m_ref, l_ref, a_ref, acc_ref = state
    tk = s_ref.shape[1]
    for h in range(NSA_HPG):
        hr = slice(h * tq, (h + 1) * tq)
        s_ref[hr, :] = _dot_nt(qa_ref[hr, 0:kdim], k)
    for h in range(NSA_HPG):
        hr = slice(h * tq, (h + 1) * tq)
        for r in range(tq // FLASH_ROWS):
            rs = slice(h * tq + r * FLASH_ROWS, h * tq + (r + 1) * FLASH_ROWS)
            s = s_ref[rs, :]
            if kind is not None:
                s = s + bias_ref[0, kind, rs, :]
            m_old = m_ref[rs, :]
            m_new = jnp.maximum(m_old, jnp.max(s, axis=-1, keepdims=True))
            p = jnp.exp2(s - jnp.tile(m_new, (1, tk // LANE)))
            alpha = jnp.exp2(m_old - m_new)
            l_ref[rs, :] = alpha * l_ref[rs, :] + jnp.sum(p, axis=-1, keepdims=True)
            m_ref[rs, :] = m_new
            a_ref[rs, :] = alpha
            p_ref[rs, :] = p.astype(BF16)
        acc_ref[hr, :] = a_ref[hr, :] * acc_ref[hr, :] + _dot(p_ref[hr, :], v)


def _selwin_kernel(q_ref, ks_ref, vs_ref, kw_ref, vw_ref, oh_ref, sel_ref, bias_ref, oc_ref, gt_ref, bz_ref,
                   o_ref, qa_ref, s_ref, p_ref, m_ref, l_ref, a_ref, acc_ref, os_ref, *, tq):
    qi = pl.program_id(2)
    state = (s_ref, p_ref, m_ref, l_ref, a_ref, acc_ref)
    aug = 2 * HEAD_DIM
    for j in range(NSA_HPG):
        qa_ref[j * tq:(j + 1) * tq, 0:HEAD_DIM] = q_ref[:, j * HEAD_DIM:(j + 1) * HEAD_DIM]
        qa_ref[j * tq:(j + 1) * tq, HEAD_DIM:aug] = sel_ref[0, 0]

    def sel_tile(kt, kind):
        off = pl.multiple_of(kt * tq, tq)
        ka = jnp.concatenate([ks_ref[pl.ds(off, tq), :], oh_ref[pl.ds(off, tq), :]], axis=1)
        _flash_tile(qa_ref, aug, ka, vs_ref[pl.ds(off, tq), :], bias_ref, kind, state, tq)

    def far_body(kt, carry):
        sel_tile(kt, None)
        return carry

    _flash_init(state)
    lax.fori_loop(0, jnp.maximum(qi - 1, 0), far_body, 0)

    @pl.when(qi >= 1)
    def _():
        sel_tile(qi - 1, KIND_SUB)

    sel_tile(qi, KIND_DIAG)
    os_ref[...] = acc_ref[...] / l_ref[...]

    nwin = WINDOW // tq
    _flash_init(state)
    for d in range(nwin, -1, -1):
        kind = KIND_CORNER if d == nwin else (KIND_DIAG if d == 0 else (KIND_SUB if d == 1 else None))

        def win_tile(d=d, kind=kind):
            off = pl.multiple_of((qi - d) * tq, tq)
            _flash_tile(qa_ref, HEAD_DIM, kw_ref[pl.ds(off, tq), :], vw_ref[pl.ds(off, tq), :], bias_ref, kind,
                        state, tq)

        if d == 0:
            win_tile()
        else:
            pl.when(qi >= d)(win_tile)

    gate = _sigmoid(gt_ref[...])
    for j in range(NSA_HPG):
        hs = slice(j * HEAD_DIM, (j + 1) * HEAD_DIM)
        rs = slice(j * tq, (j + 1) * tq)
        o = (gate[:, j:j + 1] * oc_ref[:, hs]
             + gate[:, NSA_HPG + j:NSA_HPG + j + 1] * os_ref[rs, :]
             + gate[:, 2 * NSA_HPG + j:2 * NSA_HPG + j + 1] * (acc_ref[rs, :] / l_ref[rs, :]))
        o_ref[:, hs] = (o * _silu(bz_ref[:, hs])).astype(BF16)


def _selwin_attn(hb, ha, onehot, sel, bias_t, o_c, batch, seq, tq):
    nq = seq // tq
    rows = NSA_HPG * tq
    assert 0 < WINDOW // tq and WINDOW % tq == 0
    once = pl.Buffered(1)
    kv_spec = lambda base: pl.BlockSpec((seq, HEAD_DIM), lambda b, g, i: (b, base // HEAD_DIM + g),
                                        pipeline_mode=once)
    row_g = lambda b, g, i: (b * nq + i, g)
    return pl.pallas_call(
        functools.partial(_selwin_kernel, tq=tq),
        out_shape=jax.ShapeDtypeStruct((batch * seq, D_NSA), BF16),
        grid=(batch, NSA_KV_GROUPS, nq),
        in_specs=[pl.BlockSpec((tq, GROUP_W), lambda b, g, i: (b * nq + i, HB_Q // GROUP_W + g)),
                  kv_spec(HB_KS), kv_spec(HB_VS), kv_spec(HB_KW), kv_spec(HB_VW),
                  pl.BlockSpec((seq, LANE), lambda b, g, i: (0, 0), pipeline_mode=once),
                  pl.BlockSpec((1, 1, tq, LANE), lambda b, g, i: (b, g, i, 0)),
                  pl.BlockSpec((1, N_BIAS_KINDS, rows, tq), lambda b, g, i: (g, 0, 0, 0), pipeline_mode=once),
                  pl.BlockSpec((tq, GROUP_W), row_g),
                  pl.BlockSpec((tq, LANE), lambda b, g, i: (b * nq + i, HA_GT // LANE + g)),
                  pl.BlockSpec((tq, GROUP_W), lambda b, g, i: (b * nq + i, HA_BZ // GROUP_W + g))],
        out_specs=pl.BlockSpec((tq, GROUP_W), row_g),
        scratch_shapes=[pltpu.VMEM((rows, 2 * HEAD_DIM), BF16),
                        pltpu.VMEM((rows, tq), F32),
                        pltpu.VMEM((rows, tq), BF16),
                        pltpu.VMEM((rows, LANE), F32),
                        pltpu.VMEM((rows, LANE), F32),
                        pltpu.VMEM((rows, LANE), F32),
                        pltpu.VMEM((rows, HEAD_DIM), F32),
                        pltpu.VMEM((rows, HEAD_DIM), F32)],
        compiler_params=_cparams("parallel", "parallel", "arbitrary"),
        name="selwin_attn",
    )(hb, hb, hb, hb, hb, onehot, sel, bias_t, o_c, ha, ha)


def _nsa_proj_weights(w):
    w_g = w[:, EV_GT:EV_BZ].reshape(D_MODEL, N_BRANCH, NSA_KV_GROUPS, NSA_HPG)
    w_g = jnp.transpose(w_g, (0, 2, 1, 3)).reshape(D_MODEL, NSA_KV_GROUPS, N_BRANCH * NSA_HPG)
    w_g = jnp.pad(w_g, ((0, 0), (0, 0), (0, LANE - N_BRANCH * NSA_HPG))).reshape(D_MODEL, NSA_KV_GROUPS * LANE)
    w_g = jnp.pad(w_g, ((0, 0), (0, SLAB_W - HA_GT - NSA_KV_GROUPS * LANE)))
    return jnp.concatenate([w[:, EV_Q:EV_KC], w[:, EV_KS:EV_GT], w[:, EV_KC:EV_KS], w[:, EV_BZ:EV_END], w_g],
                           axis=1).astype(BF16)


def _overlap_matrix(seq):
    rows = seq // CMP_STRIDE
    n_cmp = (seq - CMP_BLOCK) // CMP_STRIDE + 1
    n_sel = seq // SEL_BLOCK
    cstart = np.arange(rows)[None, :] * CMP_STRIDE
    sstart = np.arange(n_sel)[:, None] * SEL_BLOCK
    ov = (cstart < sstart + SEL_BLOCK) & (cstart + CMP_BLOCK > sstart) & (np.arange(rows)[None, :] < n_cmp)
    return jnp.asarray(ov.astype(np.float32), dtype=BF16)


def _block_onehot(seq):
    blk = np.arange(seq)[:, None] // SEL_BLOCK
    return jnp.asarray((blk == np.arange(LANE)[None, :]).astype(np.float32), dtype=BF16)


def kernel(x, rel_bias_table, ln_g, ln_b, ev_w_in, ev_conv_w, ev_cmp_pos, ev_cmp_w1, ev_cmp_w2, ev_w_out,
           od_w_in, od_ln_g, od_ln_b, od_sgu_w, od_sgu_b, od_w_out):
    batch, seq, d = x.shape
    depth = ln_g.shape[0]
    alpha = (2 * depth) ** 0.25
    m = batch * seq
    tq = ATT_TILE
    assert d == D_MODEL and seq % 1024 == 0 and ev_w_in.shape[-1] == EV_END

    xf = x.reshape(m, d)
    xb = xf.astype(BF16)
    bias_c = _bias_cmp(rel_bias_table, seq, tb=256)
    bias_t = _bias_tiles(rel_bias_table, tq)
    ov = _overlap_matrix(seq)
    onehot = _block_onehot(seq)

    for layer in range(depth):
        i = layer // 2
        g = ln_g[layer].reshape(1, d)
        b = ln_b[layer].reshape(1, d)
        if layer % 2 == 0:
            y_a = _conv_proj(xb, ev_w_in[i, :, EV_A:EV_Q].astype(BF16), ev_conv_w[i], seq, tm=1024, tc=256)
            hb, ha = _nsa_proj(xb, _nsa_proj_weights(ev_w_in[i]), tm=512)
            kcv = _compress(ha, ev_cmp_w1[i].astype(BF16), ev_cmp_w2[i].astype(BF16),
                            ev_cmp_pos[i].reshape(2, 1, CMP_BLOCK * HEAD_DIM).astype(BF16), batch, seq)
            o_c, sel = _cmp_attn(hb, kcv, bias_c, ov, batch, seq, CMP_TILE)
            y_b = _selwin_attn(hb, ha, onehot, sel, bias_t, o_c, batch, seq, tq)
            xf, xb = _outproj_ln(y_a, 0, y_b, 0, ev_w_out[i].astype(BF16), xf, g, b, alpha, 512, "outproj_even")
        else:
            y = _odd_mixer(xb, od_w_in[i].astype(BF16), od_ln_g[i].reshape(1, d), od_ln_b[i].reshape(1, d),
                           od_sgu_w[i], od_sgu_b[i].reshape(SGU_GROUPS, SGU_CHUNK, 1), tm=1024)
            xf, xb = _outproj_ln(y, 0, y, 1, od_w_out[i].astype(BF16), xf, g, b, alpha, 512, "outproj_odd")
    return xf.reshape(batch, seq, d)
```

```python
import functools
import math

import numpy as np
import jax
import jax.numpy as jnp
from jax import lax
from jax.experimental import pallas as pl
from jax.experimental.pallas import tpu as pltpu

F32 = jnp.float32
BF16 = jnp.bfloat16

D_MODEL = 2048
CONV_WIDTH = 3
D_CONV = 1024
NSA_HEADS = 8
NSA_KV_GROUPS = 2
NSA_HPG = NSA_HEADS // NSA_KV_GROUPS
HEAD_DIM = 128
D_NSA = NSA_HEADS * HEAD_DIM
D_KV = NSA_KV_GROUPS * HEAD_DIM
CMP_BLOCK = 32
CMP_STRIDE = 16
SEL_BLOCK = 64
N_SELECT = 16
WINDOW = 512
N_BRANCH = 3
D_SGU = D_MODEL
SGU_GROUPS = 8
SGU_CHUNK = 128
SGU_GROUP_DIM = D_SGU // SGU_GROUPS
REL_BUCKETS = 32
REL_MAX_DIST = 128
LN_EPS = 1e-5
NEG_INF = -1e30
FORCED_SCORE = 1e9
GROUP_W = NSA_HPG * HEAD_DIM
LOG2E = math.log2(math.e)
Q_SCALE = HEAD_DIM ** -0.5 * LOG2E

LANE = 128
SUBLANE = 8
VMEM_LIMIT = 56 * 1024 * 1024

EV_A = 0
EV_Q = 4 * D_CONV
EV_KC = EV_Q + D_NSA
EV_KS = EV_KC + 2 * D_KV
EV_GT = EV_KS + 4 * D_KV
EV_BZ = EV_GT + N_BRANCH * NSA_HEADS
EV_END = EV_BZ + D_NSA

HB_Q, HB_KS, HB_VS, HB_KW, HB_VW = 0, 1024, 1280, 1536, 1792
HA_KC, HA_BZ, HA_GT = 0, 512, 1536
SLAB_W = 2048
PROJ_CHUNK = 512

CMP_TILE = 512
CONV_SUB = 256
ATT_TILE = 512
KIND_DIAG, KIND_SUB, KIND_CORNER = 0, 1, 2
N_BIAS_KINDS = 3


def _cparams(*sem):
    return pltpu.CompilerParams(dimension_semantics=sem, vmem_limit_bytes=VMEM_LIMIT)


def _sigmoid(x):
    return 1.0 / (1.0 + jnp.exp(-x))


def _silu(x):
    return x * _sigmoid(x)


def _gelu_tanh(x):
    c = math.sqrt(2.0 / math.pi)
    return x * (0.5 * (1.0 + jnp.tanh(c * (x + 0.044715 * (x * x * x)))))


def _dot_nt(a, b):
    return lax.dot_general(a, b, (((1,), (1,)), ((), ())), preferred_element_type=F32)


def _dot(a, b):
    return jnp.dot(a, b, preferred_element_type=F32)


def _layer_norm(z, g, b):
    mu = jnp.mean(z, axis=-1, keepdims=True)
    zc = z - mu
    var = jnp.mean(zc * zc, axis=-1, keepdims=True)
    return zc * lax.rsqrt(var + LN_EPS) * g + b


def _outproj_kernel(y1_ref, y2_ref, w1_ref, w2_ref, x_ref, g_ref, b_ref, o_ref, ob_ref, *, alpha):
    y = _dot(y1_ref[...], w1_ref[...]) + _dot(y2_ref[...], w2_ref[...])
    out = _layer_norm(alpha * x_ref[...] + y, g_ref[...], b_ref[...])
    o_ref[...] = out
    ob_ref[...] = out.astype(BF16)


def _outproj_ln(y1, y1_col, y2, y2_col, w_out, x, g, b, alpha, tm, name):
    m, d = x.shape
    kh = w_out.shape[0] // 2
    return pl.pallas_call(
        functools.partial(_outproj_kernel, alpha=alpha),
        out_shape=(jax.ShapeDtypeStruct((m, d), F32), jax.ShapeDtypeStruct((m, d), BF16)),
        grid=(m // tm,),
        in_specs=[pl.BlockSpec((tm, kh), lambda i: (i, y1_col)),
                  pl.BlockSpec((tm, kh), lambda i: (i, y2_col)),
                  pl.BlockSpec((kh, d), lambda i: (0, 0)),
                  pl.BlockSpec((kh, d), lambda i: (1, 0)),
                  pl.BlockSpec((tm, d), lambda i: (i, 0)),
                  pl.BlockSpec((1, d), lambda i: (0, 0)),
                  pl.BlockSpec((1, d), lambda i: (0, 0))],
        out_specs=(pl.BlockSpec((tm, d), lambda i: (i, 0)),
                   pl.BlockSpec((tm, d), lambda i: (i, 0))),
        compiler_params=_cparams("parallel"),
        name=name,
    )(y1, y2, w_out, w_out, x, g, b)


def _odd_kernel(x_ref, wv_ref, wu_ref, wz_ref, g_ref, b_ref, sw_ref, sb_ref, o_ref, vs_ref, mu_ref, rstd_ref,
                *, tm):
    step = pl.program_id(1)
    nchunk = D_SGU // PROJ_CHUNK

    @pl.when(step == 0)
    def _():
        tot = None
        for c in range(nchunk):
            v = _gelu_tanh(_dot(x_ref[...], wv_ref[:, c * PROJ_CHUNK:(c + 1) * PROJ_CHUNK]))
            vs_ref[c] = v
            part = jnp.sum(v, axis=-1, keepdims=True)
            tot = part if tot is None else tot + part
        mu = tot * (1.0 / D_SGU)
        sq = None
        for c in range(nchunk):
            vc = vs_ref[c] - mu
            part = jnp.sum(vc * vc, axis=-1, keepdims=True)
            sq = part if sq is None else sq + part
        mu_ref[...] = mu
        rstd_ref[...] = lax.rsqrt(sq * (1.0 / D_SGU) + LN_EPS)

    x = x_ref[...]
    u = _gelu_tanh(_dot(x, wu_ref[...]))
    z = _dot(x, wz_ref[...])
    vn = ((vs_ref[step] - mu_ref[...]) * rstd_ref[...] * g_ref[...] + b_ref[...]).astype(BF16)
    row = lax.broadcasted_iota(jnp.int32, (SGU_CHUNK, SGU_CHUNK), 0)
    col = lax.broadcasted_iota(jnp.int32, (SGU_CHUNK, SGU_CHUNK), 1)
    for g in range(PROJ_CHUNK // SGU_GROUP_DIM):
        cs = slice(g * SGU_GROUP_DIM, (g + 1) * SGU_GROUP_DIM)
        wg = jnp.where(col <= row, sw_ref[g], 0.0).astype(BF16)
        bias = sb_ref[g]
        for c in range(tm // SGU_CHUNK):
            rs = slice(c * SGU_CHUNK, (c + 1) * SGU_CHUNK)
            mixed = _dot(wg, vn[rs, cs]) + bias
            o_ref[rs, cs] = (u[rs, cs] * mixed * _silu(z[rs, cs])).astype(BF16)


def _odd_mixer(xb, w_in, ln_g, ln_b, sgu_w, sgu_b, tm):
    m = xb.shape[0]
    cw = PROJ_CHUNK
    nstep = D_SGU // cw
    gps = cw // SGU_GROUP_DIM
    return pl.pallas_call(
        functools.partial(_odd_kernel, tm=tm),
        out_shape=jax.ShapeDtypeStruct((m, D_SGU), BF16),
        grid=(m // tm, nstep),
        in_specs=[pl.BlockSpec((tm, D_MODEL), lambda i, s: (i, 0)),
                  pl.BlockSpec((D_MODEL, D_SGU), lambda i, s: (0, 1)),
                  pl.BlockSpec((D_MODEL, cw), lambda i, s: (0, s)),
                  pl.BlockSpec((D_MODEL, cw), lambda i, s: (0, 2 * nstep + s)),
                  pl.BlockSpec((1, cw), lambda i, s: (0, s)),
                  pl.BlockSpec((1, cw), lambda i, s: (0, s)),
                  pl.BlockSpec((gps, SGU_CHUNK, SGU_CHUNK), lambda i, s: (s, 0, 0)),
                  pl.BlockSpec((gps, SGU_CHUNK, 1), lambda i, s: (s, 0, 0))],
        out_specs=pl.BlockSpec((tm, cw), lambda i, s: (i, s)),
        scratch_shapes=[pltpu.VMEM((nstep, tm, cw), F32),
                        pltpu.VMEM((tm, 1), F32),
                        pltpu.VMEM((tm, 1), F32)],
        compiler_params=_cparams("parallel", "arbitrary"),
        name="odd_mixer",
    )(xb, w_in, w_in, w_in, ln_g, ln_b, sgu_w, sgu_b)


def _conv_proj_kernel(x_ref, wh_ref, wb_ref, wc_ref, wz_ref, cw_ref, o_ref, u_ref, *, tm, tiles_per_seq):
    i = pl.program_id(1)
    nsub = tm // CONV_SUB

    @pl.when(i % tiles_per_seq == 0)
    def _():
        u_ref[0:SUBLANE, :] = jnp.zeros((SUBLANE, u_ref.shape[1]), F32)

    for r in range(nsub):
        rs = slice(r * CONV_SUB, (r + 1) * CONV_SUB)
        x = x_ref[rs, :]
        u_ref[SUBLANE + r * CONV_SUB:SUBLANE + (r + 1) * CONV_SUB, :] = _dot(x, wc_ref[...]) * _dot(x, wh_ref[...])
    for r in range(nsub):
        rs = slice(r * CONV_SUB, (r + 1) * CONV_SUB)
        x = x_ref[rs, :]
        conv = cw_ref[CONV_WIDTH - 1:CONV_WIDTH, :] * u_ref[SUBLANE + r * CONV_SUB:SUBLANE + (r + 1) * CONV_SUB, :]
        for k in range(CONV_WIDTH - 1):
            lo = SUBLANE + r * CONV_SUB - (CONV_WIDTH - 1 - k)
            conv = conv + cw_ref[k:k + 1, :] * u_ref[lo:lo + CONV_SUB, :]
        o_ref[rs, :] = (_dot(x, wb_ref[...]) * conv * _silu(_dot(x, wz_ref[...]))).astype(BF16)
    u_ref[0:SUBLANE, :] = u_ref[tm:tm + SUBLANE, :]


def _conv_proj(xb, w_a, conv_w, seq, tm, tc):
    m = xb.shape[0]
    nct = D_CONV // tc

    def wspec(part):
        return pl.BlockSpec((D_MODEL, tc), lambda j, i: (0, part * nct + j))

    return pl.pallas_call(
        functools.partial(_conv_proj_kernel, tm=tm, tiles_per_seq=seq // tm),
        out_shape=jax.ShapeDtypeStruct((m, D_CONV), BF16),
        grid=(nct, m // tm),
        in_specs=[pl.BlockSpec((tm, D_MODEL), lambda j, i: (i, 0)),
                  wspec(0), wspec(1), wspec(2), wspec(3),
                  pl.BlockSpec((CONV_WIDTH, tc), lambda j, i: (0, j))],
        out_specs=pl.BlockSpec((tm, tc), lambda j, i: (i, j)),
        scratch_shapes=[pltpu.VMEM((tm + SUBLANE, tc), F32)],
        compiler_params=_cparams("parallel", "arbitrary"),
        name="conv_proj",
    )(xb, w_a, w_a, w_a, w_a, conv_w)


def _nsa_proj_kernel(x_ref, w_ref, hb_ref, ha_ref):
    x = x_ref[...]
    nb = SLAB_W // PROJ_CHUNK
    for c in range(nb):
        cs = slice(c * PROJ_CHUNK, (c + 1) * PROJ_CHUNK)
        acc = _dot(x, w_ref[:, cs])
        if (c + 1) * PROJ_CHUNK <= HB_KS:
            acc = acc * Q_SCALE
        hb_ref[:, cs] = acc.astype(BF16)
    for c in range(nb):
        cs = slice(c * PROJ_CHUNK, (c + 1) * PROJ_CHUNK)
        ha_ref[:, cs] = _dot(x, w_ref[:, SLAB_W + c * PROJ_CHUNK:SLAB_W + (c + 1) * PROJ_CHUNK])


def _nsa_proj(xb, w, tm):
    m = xb.shape[0]
    return pl.pallas_call(
        _nsa_proj_kernel,
        out_shape=(jax.ShapeDtypeStruct((m, SLAB_W), BF16), jax.ShapeDtypeStruct((m, SLAB_W), F32)),
        grid=(m // tm,),
        in_specs=[pl.BlockSpec((tm, D_MODEL), lambda i: (i, 0)),
                  pl.BlockSpec((D_MODEL, 2 * SLAB_W), lambda i: (0, 0), pipeline_mode=pl.Buffered(1))],
        out_specs=(pl.BlockSpec((tm, SLAB_W), lambda i: (i, 0)),
                   pl.BlockSpec((tm, SLAB_W), lambda i: (i, 0))),
        compiler_params=_cparams("parallel"),
        name="nsa_proj",
    )(xb, w)


def _compress_kernel(tok_ref, w1_ref, w2_ref, pos_ref, o_ref, b_ref, *, rows):
    half = CMP_STRIDE * HEAD_DIM
    x2 = jnp.concatenate(
        [tok_ref[pl.ds(l, rows, stride=CMP_STRIDE), :] for l in range(CMP_STRIDE)], axis=1).astype(BF16)
    lo = _dot(x2, w1_ref[0, 0:half, :])
    hi = _dot(x2, w1_ref[0, half:2 * half, :])
    b_ref[0:rows, :] = hi
    b_ref[rows:rows + SUBLANE, :] = jnp.zeros((SUBLANE, HEAD_DIM), F32)
    posb = _dot(jnp.broadcast_to(pos_ref[0], (SUBLANE, 2 * half)), w1_ref[0])[0:1, :]
    pre = lo + b_ref[1:rows + 1, :] + posb
    o_ref[0, 0] = _dot(_silu(pre).astype(BF16), w2_ref[0]).astype(BF16)


def _compress(ha, w1, w2, pos, batch, seq):
    rows = seq // CMP_STRIDE
    nkv = 2 * NSA_KV_GROUPS
    kc_block = HA_KC // HEAD_DIM
    return pl.pallas_call(
        functools.partial(_compress_kernel, rows=rows),
        out_shape=jax.ShapeDtypeStruct((batch, nkv, rows, HEAD_DIM), BF16),
        grid=(batch, nkv),
        in_specs=[pl.BlockSpec((seq, HEAD_DIM), lambda b, c: (b, kc_block + c)),
                  pl.BlockSpec((1, CMP_BLOCK * HEAD_DIM, HEAD_DIM), lambda b, c: (c // NSA_KV_GROUPS, 0, 0)),
                  pl.BlockSpec((1, HEAD_DIM, HEAD_DIM), lambda b, c: (c // NSA_KV_GROUPS, 0, 0)),
                  pl.BlockSpec((1, 1, CMP_BLOCK * HEAD_DIM), lambda b, c: (c // NSA_KV_GROUPS, 0, 0))],
        out_specs=pl.BlockSpec((1, 1, rows, HEAD_DIM), lambda b, c: (b, c, 0, 0)),
        scratch_shapes=[pltpu.VMEM((rows + SUBLANE, HEAD_DIM), F32)],
        compiler_params=_cparams("parallel", "arbitrary"),
        name="cmp_blocks",
    )(ha, w1, w2, pos)


def _t5_bucket(dist):
    n = jnp.maximum(dist, 0)
    max_exact = REL_BUCKETS // 2
    large = max_exact + (jnp.log(jnp.maximum(n, 1).astype(F32) / max_exact)
                         / math.log(REL_MAX_DIST / max_exact) * (REL_BUCKETS - max_exact)).astype(jnp.int32)
    large = jnp.minimum(large, REL_BUCKETS - 1)
    return jnp.where(n < max_exact, n, large)


def _rel_bias(dist, valid, tab_ref, head):
    bkt = _t5_bucket(dist)
    acc = jnp.zeros(dist.shape, F32)
    for b in range(REL_BUCKETS):
        acc = jnp.where(bkt == b, tab_ref[b, head], acc)
    return jnp.where(valid, acc, NEG_INF)


def _bias_cmp_kernel(tab_ref, o_ref, *, tb, rows, n_cmp):
    head = pl.program_id(0)
    t = pl.program_id(1) * tb + lax.broadcasted_iota(jnp.int32, (tb, rows), 0)
    n = lax.broadcasted_iota(jnp.int32, (tb, rows), 1)
    dist = t - (n * CMP_STRIDE + CMP_BLOCK - 1)
    o_ref[0] = _rel_bias(dist, (dist >= 0) & (n < n_cmp), tab_ref, head) * LOG2E


def _bias_cmp(table, seq, tb):
    rows = seq // CMP_STRIDE
    n_cmp = (seq - CMP_BLOCK) // CMP_STRIDE + 1
    return pl.pallas_call(
        functools.partial(_bias_cmp_kernel, tb=tb, rows=rows, n_cmp=n_cmp),
        out_shape=jax.ShapeDtypeStruct((NSA_HEADS, seq, rows), F32),
        grid=(NSA_HEADS, seq // tb),
        in_specs=[pl.BlockSpec(memory_space=pltpu.SMEM)],
        out_specs=pl.BlockSpec((1, tb, rows), lambda h, i: (h, i, 0)),
        compiler_params=_cparams("parallel", "arbitrary"),
        name="bias_cmp",
    )(table)


def _bias_tiles_kernel(tab_ref, o_ref, *, tq):
    head = pl.program_id(0)
    kind = pl.program_id(1)
    ij = (lax.broadcasted_iota(jnp.int32, (tq, tq), 0) - lax.broadcasted_iota(jnp.int32, (tq, tq), 1))
    dist = jnp.where(kind == KIND_DIAG, ij, jnp.where(kind == KIND_SUB, tq + ij, WINDOW + ij))
    lo = jnp.where(kind == KIND_DIAG, 0, -tq)
    hi = jnp.where(kind == KIND_CORNER, 0, tq)
    far_dist = jnp.full((SUBLANE, LANE), tq + 1, jnp.int32)
    far = _rel_bias(far_dist, far_dist > 0, tab_ref, head)[0:1, 0:1]
    o_ref[0, 0] = (_rel_bias(dist, (ij >= lo) & (ij < hi), tab_ref, head) - far) * LOG2E


def _bias_tiles(table, tq):
    assert WINDOW % tq == 0 and tq + 1 >= REL_MAX_DIST
    return pl.pallas_call(
        functools.partial(_bias_tiles_kernel, tq=tq),
        out_shape=jax.ShapeDtypeStruct((NSA_KV_GROUPS, N_BIAS_KINDS, NSA_HPG * tq, tq), F32),
        grid=(NSA_HEADS, N_BIAS_KINDS),
        in_specs=[pl.BlockSpec(memory_space=pltpu.SMEM)],
        out_specs=pl.BlockSpec((1, 1, tq, tq), lambda h, k: (h // NSA_HPG, k, h % NSA_HPG, 0)),
        compiler_params=_cparams("parallel", "arbitrary"),
        name="bias_tiles",
    )(table)


def _cmp_attn_kernel(q_ref, kc_ref, vc_ref, bias_ref, ov_ref, oc_ref, sel_ref, *, tq, n_sel, n_top):
    kc = kc_ref[0, 0]
    vc = vc_ref[0, 0]
    psum = None
    for j in range(NSA_HPG):
        hs = slice(j * HEAD_DIM, (j + 1) * HEAD_DIM)
        bias = bias_ref[j]
        s = _dot_nt(q_ref[:, hs], kc) + bias
        m = jnp.max(s, axis=-1, keepdims=True)
        e = jnp.exp2(s - m)
        p = e / jnp.sum(e, axis=-1, keepdims=True)
        p = jnp.where(bias > 0.5 * NEG_INF, p, 0.0)
        oc_ref[:, hs] = _dot(p.astype(BF16), vc)
        psum = p if psum is None else psum + p
    ov = ov_ref[...]
    imp = None
    rem = psum
    for _ in range(3):
        piece = rem.astype(BF16)
        part = _dot_nt(ov, piece)
        imp = part if imp is None else imp + part
        rem = rem - piece.astype(F32)
    t = pl.program_id(1) * tq + lax.broadcasted_iota(jnp.int32, (n_sel, tq), 1)
    cur = jnp.right_shift(t, int(math.log2(SEL_BLOCK)))
    blk = lax.broadcasted_iota(jnp.int32, (n_sel, tq), 0)
    forced = (blk == 0) | (blk == cur) | (blk == cur - 1)
    imp = jnp.where(blk > cur, -1.0, jnp.where(forced, FORCED_SCORE, imp))
    groups = [imp[g * SUBLANE:(g + 1) * SUBLANE, :] for g in range(n_sel // SUBLANE)]
    ranks = [jnp.zeros((SUBLANE, tq), jnp.int32) for _ in groups]
    sub = lax.broadcasted_iota(jnp.int32, (SUBLANE, tq), 0)
    for i in range(n_sel):
        row = imp[i:i + 1, :]
        for g, x in enumerate(groups):
            if g * SUBLANE > i:
                ahead = row >= x
            elif (g + 1) * SUBLANE - 1 < i:
                ahead = row > x
            else:
                ahead = (row > x) | ((row == x) & (sub > i - g * SUBLANE))
            ranks[g] = ranks[g] + ahead.astype(jnp.int32)
    sel_t = jnp.where(jnp.concatenate(ranks, axis=0) < n_top, 0.0, NEG_INF)
    if n_sel < LANE:
        sel_t = jnp.concatenate([sel_t, jnp.zeros((LANE - n_sel, tq), F32)], axis=0)
    sel_ref[0, 0] = sel_t.T.astype(BF16)


def _cmp_attn(hb, kcv, bias_c, ov, batch, seq, tq):
    rows = seq // CMP_STRIDE
    n_sel = seq // SEL_BLOCK
    n_top = min(N_SELECT, n_sel)
    assert n_sel <= LANE and n_sel % SUBLANE == 0
    nq = seq // tq
    g_n = NSA_KV_GROUPS
    return pl.pallas_call(
        functools.partial(_cmp_attn_kernel, tq=tq, n_sel=n_sel, n_top=n_top),
        out_shape=(jax.ShapeDtypeStruct((batch * seq, D_NSA), F32),
                   jax.ShapeDtypeStruct((batch, g_n, seq, LANE), BF16)),
        grid=(g_n, nq, batch),
        in_specs=[pl.BlockSpec((tq, GROUP_W), lambda g, i, b: (b * nq + i, HB_Q // GROUP_W + g)),
                  pl.BlockSpec((1, 1, rows, HEAD_DIM), lambda g, i, b: (b, g, 0, 0)),
                  pl.BlockSpec((1, 1, rows, HEAD_DIM), lambda g, i, b: (b, g_n + g, 0, 0)),
                  pl.BlockSpec((NSA_HPG, tq, rows), lambda g, i, b: (g, i, 0)),
                  pl.BlockSpec((n_sel, rows), lambda g, i, b: (0, 0))],
        out_specs=(pl.BlockSpec((tq, GROUP_W), lambda g, i, b: (b * nq + i, g)),
                   pl.BlockSpec((1, 1, tq, LANE), lambda g, i, b: (b, g, i, 0))),
        compiler_params=_cparams("parallel", "parallel", "arbitrary"),
        name="cmp_attn",
    )(hb, kcv, kcv, bias_c, ov)


FLASH_ROWS = 128


def _flash_init(state):
    _, _, m_ref, _, acc_ref = state
    m_ref[...] = jnp.full(m_ref.shape, -3e38, F32)
    acc_ref[...] = jnp.zeros(acc_ref.shape, F32)


def _flash_scores(qa_ref, kdim, k, s_ref, h, tq):
    hr = slice(h * tq, (h + 1) * tq)
    s_ref[hr, :] = _dot_nt(qa_ref[hr, 0:kdim], k)


def _flash_step(qa_ref, bias_ref, kind, v, nxt, state, tq):
    s_ref, p_ref, m_ref, a_ref, acc_ref = state
    tk = s_ref.shape[1]
    for h in range(NSA_HPG):
        hr = slice(h * tq, (h + 1) * tq)
        for r in range(tq // FLASH_ROWS):
            rs = slice(h * tq + r * FLASH_ROWS, h * tq + (r + 1) * FLASH_ROWS)
            s = s_ref[rs, :]
            if kind is not None:
                s = s + bias_ref[0, kind, rs, :]
            m_old = m_ref[rs, :]
            m_new = jnp.maximum(m_old, jnp.max(s, axis=-1, keepdims=True))
            p_ref[rs, :] = jnp.exp2(s - jnp.tile(m_new, (1, tk // LANE))).astype(BF16)
            a_ref[rs, :] = jnp.exp2(m_old - m_new)
            m_ref[rs, :] = m_new
        if nxt is not None:
            _flash_scores(qa_ref, nxt[0], nxt[1], s_ref, h, tq)
        acc_ref[hr, :] = jnp.tile(a_ref[hr, :], (1, 2)) * acc_ref[hr, :] + _dot(p_ref[hr, :], v)


def _selwin_kernel(q_ref, ks_ref, vs_ref, kw_ref, vw_ref, oh_ref, sel_ref, bias_ref, oc_ref, gt_ref, bz_ref,
                   o_ref, qa_ref, s_ref, p_ref, m_ref, a_ref, acc_ref, os_ref, *, tq):
    qi = pl.program_id(2)
    state = (s_ref, p_ref, m_ref, a_ref, acc_ref)
    aug = 2 * HEAD_DIM
    nwin = WINDOW // tq
    ones = jnp.ones((tq, HEAD_DIM), BF16)
    for j in range(NSA_HPG):
        qa_ref[j * tq:(j + 1) * tq, 0:HEAD_DIM] = q_ref[:, j * HEAD_DIM:(j + 1) * HEAD_DIM]
        qa_ref[j * tq:(j + 1) * tq, HEAD_DIM:aug] = sel_ref[0, 0]

    def rows_of(kt):
        return pl.ds(pl.multiple_of(kt * tq, tq), tq)

    def sel_keys(kt):
        return aug, jnp.concatenate([ks_ref[rows_of(kt), :], oh_ref[rows_of(kt), :]], axis=1)

    def win_keys(kt):
        return HEAD_DIM, kw_ref[rows_of(kt), :]

    def values(v_ref, kt):
        return jnp.concatenate([v_ref[rows_of(kt), :], ones], axis=1)

    _flash_init(state)
    for h in range(NSA_HPG):
        _flash_scores(qa_ref, *sel_keys(0), s_ref, h, tq)

    def far_body(kt, carry):
        _flash_step(qa_ref, bias_ref, None, values(vs_ref, kt), sel_keys(kt + 1), state, tq)
        return carry

    lax.fori_loop(0, jnp.maximum(qi - 1, 0), far_body, 0)

    @pl.when(qi >= 1)
    def _():
        _flash_step(qa_ref, bias_ref, KIND_SUB, values(vs_ref, qi - 1), sel_keys(qi), state, tq)

    first_win = jnp.maximum(qi - nwin, 0)
    _flash_step(qa_ref, bias_ref, KIND_DIAG, values(vs_ref, qi), win_keys(first_win), state, tq)
    os_ref[...] = acc_ref[:, 0:HEAD_DIM] / acc_ref[:, HEAD_DIM:aug]

    _flash_init(state)
    for d in range(nwin, -1, -1):
        kind = KIND_CORNER if d == nwin else (KIND_DIAG if d == 0 else (KIND_SUB if d == 1 else None))

        def win_step(d=d, kind=kind):
            nxt = win_keys(qi - d + 1) if d > 0 else None
            _flash_step(qa_ref, bias_ref, kind, values(vw_ref, qi - d), nxt, state, tq)

        if d == 0:
            win_step()
        else:
            pl.when(qi >= d)(win_step)

    gate = _sigmoid(gt_ref[...])
    for j in range(NSA_HPG):
        hs = slice(j * HEAD_DIM, (j + 1) * HEAD_DIM)
        rs = slice(j * tq, (j + 1) * tq)
        o_w = acc_ref[rs, 0:HEAD_DIM] / acc_ref[rs, HEAD_DIM:aug]
        o = (gate[:, j:j + 1] * oc_ref[:, hs]
             + gate[:, NSA_HPG + j:NSA_HPG + j + 1] * os_ref[rs, :]
             + gate[:, 2 * NSA_HPG + j:2 * NSA_HPG + j + 1] * o_w)
        o_ref[:, hs] = (o * _silu(bz_ref[:, hs])).astype(BF16)


def _selwin_attn(hb, ha, onehot, sel, bias_t, o_c, batch, seq, tq):
    nq = seq // tq
    rows = NSA_HPG * tq
    assert 0 < WINDOW // tq and WINDOW % tq == 0
    once = pl.Buffered(1)
    kv_spec = lambda base: pl.BlockSpec((seq, HEAD_DIM), lambda b, g, i: (b, base // HEAD_DIM + g),
                                        pipeline_mode=once)
    row_g = lambda b, g, i: (b * nq + i, g)
    return pl.pallas_call(
        functools.partial(_selwin_kernel, tq=tq),
        out_shape=jax.ShapeDtypeStruct((batch * seq, D_NSA), BF16),
        grid=(batch, NSA_KV_GROUPS, nq),
        in_specs=[pl.BlockSpec((tq, GROUP_W), lambda b, g, i: (b * nq + i, HB_Q // GROUP_W + g)),
                  kv_spec(HB_KS), kv_spec(HB_VS), kv_spec(HB_KW), kv_spec(HB_VW),
                  pl.BlockSpec((seq, LANE), lambda b, g, i: (0, 0), pipeline_mode=once),
                  pl.BlockSpec((1, 1, tq, LANE), lambda b, g, i: (b, g, i, 0)),
                  pl.BlockSpec((1, N_BIAS_KINDS, rows, tq), lambda b, g, i: (g, 0, 0, 0), pipeline_mode=once),
                  pl.BlockSpec((tq, GROUP_W), row_g),
                  pl.BlockSpec((tq, LANE), lambda b, g, i: (b * nq + i, HA_GT // LANE + g)),
                  pl.BlockSpec((tq, GROUP_W), lambda b, g, i: (b * nq + i, HA_BZ // GROUP_W + g))],
        out_specs=pl.BlockSpec((tq, GROUP_W), row_g),
        scratch_shapes=[pltpu.VMEM((rows, 2 * HEAD_DIM), BF16),
                        pltpu.VMEM((rows, tq), F32),
                        pltpu.VMEM((rows, tq), BF16),
                        pltpu.VMEM((rows, LANE), F32),
                        pltpu.VMEM((rows, LANE), F32),
                        pltpu.VMEM((rows, 2 * HEAD_DIM), F32),
                        pltpu.VMEM((rows, HEAD_DIM), F32)],
        compiler_params=_cparams("parallel", "parallel", "arbitrary"),
        name="selwin_attn",
    )(hb, hb, hb, hb, hb, onehot, sel, bias_t, o_c, ha, ha)


def _nsa_proj_weights(w):
    w_g = w[:, EV_GT:EV_BZ].reshape(D_MODEL, N_BRANCH, NSA_KV_GROUPS, NSA_HPG)
    w_g = jnp.transpose(w_g, (0, 2, 1, 3)).reshape(D_MODEL, NSA_KV_GROUPS, N_BRANCH * NSA_HPG)
    w_g = jnp.pad(w_g, ((0, 0), (0, 0), (0, LANE - N_BRANCH * NSA_HPG))).reshape(D_MODEL, NSA_KV_GROUPS * LANE)
    w_g = jnp.pad(w_g, ((0, 0), (0, SLAB_W - HA_GT - NSA_KV_GROUPS * LANE)))
    return jnp.concatenate([w[:, EV_Q:EV_KC], w[:, EV_KS:EV_GT], w[:, EV_KC:EV_KS], w[:, EV_BZ:EV_END], w_g],
                           axis=1).astype(BF16)


def _overlap_matrix(seq):
    rows = seq // CMP_STRIDE
    n_cmp = (seq - CMP_BLOCK) // CMP_STRIDE + 1
    n_sel = seq // SEL_BLOCK
    cstart = np.arange(rows)[None, :] * CMP_STRIDE
    sstart = np.arange(n_sel)[:, None] * SEL_BLOCK
    ov = (cstart < sstart + SEL_BLOCK) & (cstart + CMP_BLOCK > sstart) & (np.arange(rows)[None, :] < n_cmp)
    return jnp.asarray(ov.astype(np.float32), dtype=BF16)


def _block_onehot(seq):
    blk = np.arange(seq)[:, None] // SEL_BLOCK
    return jnp.asarray((blk == np.arange(LANE)[None, :]).astype(np.float32), dtype=BF16)


def kernel(x, rel_bias_table, ln_g, ln_b, ev_w_in, ev_conv_w, ev_cmp_pos, ev_cmp_w1, ev_cmp_w2, ev_w_out,
           od_w_in, od_ln_g, od_ln_b, od_sgu_w, od_sgu_b, od_w_out):
    batch, seq, d = x.shape
    depth = ln_g.shape[0]
    alpha = (2 * depth) ** 0.25
    m = batch * seq
    tq = ATT_TILE
    assert d == D_MODEL and seq % 1024 == 0 and ev_w_in.shape[-1] == EV_END

    xf = x.reshape(m, d)
    xb = xf.astype(BF16)
    bias_c = _bias_cmp(rel_bias_table, seq, tb=256)
    bias_t = _bias_tiles(rel_bias_table, tq)
    ov = _overlap_matrix(seq)
    onehot = _block_onehot(seq)

    for layer in range(depth):
        i = layer // 2
        g = ln_g[layer].reshape(1, d)
        b = ln_b[layer].reshape(1, d)
        if layer % 2 == 0:
            y_a = _conv_proj(xb, ev_w_in[i, :, EV_A:EV_Q].astype(BF16), ev_conv_w[i], seq, tm=1024, tc=256)
            hb, ha = _nsa_proj(xb, _nsa_proj_weights(ev_w_in[i]), tm=512)
            kcv = _compress(ha, ev_cmp_w1[i].astype(BF16), ev_cmp_w2[i].astype(BF16),
                            ev_cmp_pos[i].reshape(2, 1, CMP_BLOCK * HEAD_DIM).astype(BF16), batch, seq)
            o_c, sel = _cmp_attn(hb, kcv, bias_c, ov, batch, seq, CMP_TILE)
            y_b = _selwin_attn(hb, ha, onehot, sel, bias_t, o_c, batch, seq, tq)
            xf, xb = _outproj_ln(y_a, 0, y_b, 0, ev_w_out[i].astype(BF16), xf, g, b, alpha, 512, "outproj_even")
        else:
            y = _odd_mixer(xb, od_w_in[i].astype(BF16), od_ln_g[i].reshape(1, d), od_ln_b[i].reshape(1, d),
                           od_sgu_w[i], od_sgu_b[i].reshape(SGU_GROUPS, SGU_CHUNK, 1), tm=1024)
            xf, xb = _outproj_ln(y, 0, y, 1, od_w_out[i].astype(BF16), xf, g, b, alpha, 512, "outproj_odd")
    return xf.reshape(batch, seq, d)
```

```python
import functools
import math

import numpy as np
import jax
import jax.numpy as jnp
from jax import lax
from jax.experimental import pallas as pl
from jax.experimental.pallas import tpu as pltpu

F32 = jnp.float32
BF16 = jnp.bfloat16

D_MODEL = 2048
CONV_WIDTH = 3
D_CONV = 1024
NSA_HEADS = 8
NSA_KV_GROUPS = 2
NSA_HPG = NSA_HEADS // NSA_KV_GROUPS
HEAD_DIM = 128
D_NSA = NSA_HEADS * HEAD_DIM
D_KV = NSA_KV_GROUPS * HEAD_DIM
CMP_BLOCK = 32
CMP_STRIDE = 16
SEL_BLOCK = 64
N_SELECT = 16
WINDOW = 512
N_BRANCH = 3
D_SGU = D_MODEL
SGU_GROUPS = 8
SGU_CHUNK = 128
SGU_GROUP_DIM = D_SGU // SGU_GROUPS
REL_BUCKETS = 32
REL_MAX_DIST = 128
LN_EPS = 1e-5
NEG_INF = -1e30
FORCED_SCORE = 1e9
GROUP_W = NSA_HPG * HEAD_DIM
LOG2E = math.log2(math.e)
Q_SCALE = HEAD_DIM ** -0.5 * LOG2E

LANE = 128
SUBLANE = 8
VMEM_LIMIT = 56 * 1024 * 1024

EV_A = 0
EV_Q = 4 * D_CONV
EV_KC = EV_Q + D_NSA
EV_KS = EV_KC + 2 * D_KV
EV_GT = EV_KS + 4 * D_KV
EV_BZ = EV_GT + N_BRANCH * NSA_HEADS
EV_END = EV_BZ + D_NSA

HB_Q, HB_KS, HB_VS, HB_KW, HB_VW = 0, 1024, 1280, 1536, 1792
HA_KC, HA_BZ, HA_GT = 0, 512, 1536
SLAB_W = 2048
PROJ_CHUNK = 512

CMP_TILE = 512
ROW_SUB = 256
OUT_SUB = 128
ATT_TILE = 512
KIND_DIAG, KIND_SUB, KIND_CORNER = 0, 1, 2
N_BIAS_KINDS = 3


def _cparams(*sem):
    return pltpu.CompilerParams(dimension_semantics=sem, vmem_limit_bytes=VMEM_LIMIT)


def _sigmoid(x):
    return 1.0 / (1.0 + jnp.exp(-x))


def _silu(x):
    return x * _sigmoid(x)


def _gelu_tanh(x):
    c = math.sqrt(2.0 / math.pi)
    return x * (0.5 * (1.0 + jnp.tanh(c * (x + 0.044715 * (x * x * x)))))


def _dot_nt(a, b):
    return lax.dot_general(a, b, (((1,), (1,)), ((), ())), preferred_element_type=F32)


def _dot(a, b):
    return jnp.dot(a, b, preferred_element_type=F32)


def _layer_norm(z, g, b):
    mu = jnp.mean(z, axis=-1, keepdims=True)
    zc = z - mu
    var = jnp.mean(zc * zc, axis=-1, keepdims=True)
    return zc * lax.rsqrt(var + LN_EPS) * g + b


def _outproj_kernel(y1_ref, y2_ref, w1_ref, w2_ref, x_ref, g_ref, b_ref, o_ref, ob_ref, *, alpha):
    for r in range(x_ref.shape[0] // OUT_SUB):
        rs = slice(r * OUT_SUB, (r + 1) * OUT_SUB)
        y = _dot(y1_ref[rs, :], w1_ref[...]) + _dot(y2_ref[rs, :], w2_ref[...])
        out = _layer_norm(alpha * x_ref[rs, :] + y, g_ref[...], b_ref[...])
        o_ref[rs, :] = out
        ob_ref[rs, :] = out.astype(BF16)


def _outproj_ln(y1, y1_col, y2, y2_col, w_out, x, g, b, alpha, tm, name):
    m, d = x.shape
    kh = w_out.shape[0] // 2
    return pl.pallas_call(
        functools.partial(_outproj_kernel, alpha=alpha),
        out_shape=(jax.ShapeDtypeStruct((m, d), F32), jax.ShapeDtypeStruct((m, d), BF16)),
        grid=(m // tm,),
        in_specs=[pl.BlockSpec((tm, kh), lambda i: (i, y1_col)),
                  pl.BlockSpec((tm, kh), lambda i: (i, y2_col)),
                  pl.BlockSpec((kh, d), lambda i: (0, 0)),
                  pl.BlockSpec((kh, d), lambda i: (1, 0)),
                  pl.BlockSpec((tm, d), lambda i: (i, 0)),
                  pl.BlockSpec((1, d), lambda i: (0, 0)),
                  pl.BlockSpec((1, d), lambda i: (0, 0))],
        out_specs=(pl.BlockSpec((tm, d), lambda i: (i, 0)),
                   pl.BlockSpec((tm, d), lambda i: (i, 0))),
        compiler_params=_cparams("parallel"),
        name=name,
    )(y1, y2, w_out, w_out, x, g, b)


def _odd_kernel(x_ref, wv_ref, wu_ref, wz_ref, g_ref, b_ref, sw_ref, sb_ref, o_ref, vs_ref, mu_ref, rstd_ref,
                *, tm):
    step = pl.program_id(1)
    nchunk = D_SGU // PROJ_CHUNK

    @pl.when(step == 0)
    def _():
        for r in range(tm // ROW_SUB):
            rs = slice(r * ROW_SUB, (r + 1) * ROW_SUB)
            x = x_ref[rs, :]
            tot = None
            for c in range(nchunk):
                v = _gelu_tanh(_dot(x, wv_ref[:, c * PROJ_CHUNK:(c + 1) * PROJ_CHUNK]))
                vs_ref[c, rs, :] = v
                part = jnp.sum(v, axis=-1, keepdims=True)
                tot = part if tot is None else tot + part
            mu = tot * (1.0 / D_SGU)
            sq = None
            for c in range(nchunk):
                vc = vs_ref[c, rs, :] - mu
                part = jnp.sum(vc * vc, axis=-1, keepdims=True)
                sq = part if sq is None else sq + part
            mu_ref[rs, :] = mu
            rstd_ref[rs, :] = lax.rsqrt(sq * (1.0 / D_SGU) + LN_EPS)

    row = lax.broadcasted_iota(jnp.int32, (SGU_CHUNK, SGU_CHUNK), 0)
    col = lax.broadcasted_iota(jnp.int32, (SGU_CHUNK, SGU_CHUNK), 1)
    ngrp = PROJ_CHUNK // SGU_GROUP_DIM
    wgs = [jnp.where(col <= row, sw_ref[g], 0.0).astype(BF16) for g in range(ngrp)]
    for r in range(tm // ROW_SUB):
        rs = slice(r * ROW_SUB, (r + 1) * ROW_SUB)
        x = x_ref[rs, :]
        u = _gelu_tanh(_dot(x, wu_ref[...]))
        z = _dot(x, wz_ref[...])
        vn = ((vs_ref[step, rs, :] - mu_ref[rs, :]) * rstd_ref[rs, :] * g_ref[...] + b_ref[...]).astype(BF16)
        for g in range(ngrp):
            cs = slice(g * SGU_GROUP_DIM, (g + 1) * SGU_GROUP_DIM)
            for c in range(ROW_SUB // SGU_CHUNK):
                ls = slice(c * SGU_CHUNK, (c + 1) * SGU_CHUNK)
                os_ = slice(r * ROW_SUB + c * SGU_CHUNK, r * ROW_SUB + (c + 1) * SGU_CHUNK)
                mixed = _dot(wgs[g], vn[ls, cs]) + sb_ref[g]
                o_ref[os_, cs] = (u[ls, cs] * mixed * _silu(z[ls, cs])).astype(BF16)


def _odd_mixer(xb, w_in, ln_g, ln_b, sgu_w, sgu_b, tm):
    m = xb.shape[0]
    cw = PROJ_CHUNK
    nstep = D_SGU // cw
    gps = cw // SGU_GROUP_DIM
    return pl.pallas_call(
        functools.partial(_odd_kernel, tm=tm),
        out_shape=jax.ShapeDtypeStruct((m, D_SGU), BF16),
        grid=(m // tm, nstep),
        in_specs=[pl.BlockSpec((tm, D_MODEL), lambda i, s: (i, 0)),
                  pl.BlockSpec((D_MODEL, D_SGU), lambda i, s: (0, 1)),
                  pl.BlockSpec((D_MODEL, cw), lambda i, s: (0, s)),
                  pl.BlockSpec((D_MODEL, cw), lambda i, s: (0, 2 * nstep + s)),
                  pl.BlockSpec((1, cw), lambda i, s: (0, s)),
                  pl.BlockSpec((1, cw), lambda i, s: (0, s)),
                  pl.BlockSpec((gps, SGU_CHUNK, SGU_CHUNK), lambda i, s: (s, 0, 0)),
                  pl.BlockSpec((gps, SGU_CHUNK, 1), lambda i, s: (s, 0, 0))],
        out_specs=pl.BlockSpec((tm, cw), lambda i, s: (i, s)),
        scratch_shapes=[pltpu.VMEM((nstep, tm, cw), F32),
                        pltpu.VMEM((tm, 1), F32),
                        pltpu.VMEM((tm, 1), F32)],
        compiler_params=_cparams("parallel", "arbitrary"),
        name="odd_mixer",
    )(xb, w_in, w_in, w_in, ln_g, ln_b, sgu_w, sgu_b)


def _conv_proj_kernel(x_ref, wh_ref, wb_ref, wc_ref, wz_ref, cw_ref, o_ref, u_ref, *, tm, tiles_per_seq):
    i = pl.program_id(1)
    nsub = tm // ROW_SUB

    @pl.when(i % tiles_per_seq == 0)
    def _():
        u_ref[0:SUBLANE, :] = jnp.zeros((SUBLANE, u_ref.shape[1]), F32)

    for r in range(nsub):
        rs = slice(r * ROW_SUB, (r + 1) * ROW_SUB)
        x = x_ref[rs, :]
        u_ref[SUBLANE + r * ROW_SUB:SUBLANE + (r + 1) * ROW_SUB, :] = _dot(x, wc_ref[...]) * _dot(x, wh_ref[...])
    for r in range(nsub):
        rs = slice(r * ROW_SUB, (r + 1) * ROW_SUB)
        x = x_ref[rs, :]
        conv = cw_ref[CONV_WIDTH - 1:CONV_WIDTH, :] * u_ref[SUBLANE + r * ROW_SUB:SUBLANE + (r + 1) * ROW_SUB, :]
        for k in range(CONV_WIDTH - 1):
            lo = SUBLANE + r * ROW_SUB - (CONV_WIDTH - 1 - k)
            conv = conv + cw_ref[k:k + 1, :] * u_ref[lo:lo + ROW_SUB, :]
        o_ref[rs, :] = (_dot(x, wb_ref[...]) * conv * _silu(_dot(x, wz_ref[...]))).astype(BF16)
    u_ref[0:SUBLANE, :] = u_ref[tm:tm + SUBLANE, :]


def _conv_proj(xb, w_a, conv_w, seq, tm, tc):
    m = xb.shape[0]
    nct = D_CONV // tc

    def wspec(part):
        return pl.BlockSpec((D_MODEL, tc), lambda j, i: (0, part * nct + j))

    return pl.pallas_call(
        functools.partial(_conv_proj_kernel, tm=tm, tiles_per_seq=seq // tm),
        out_shape=jax.ShapeDtypeStruct((m, D_CONV), BF16),
        grid=(nct, m // tm),
        in_specs=[pl.BlockSpec((tm, D_MODEL), lambda j, i: (i, 0)),
                  wspec(0), wspec(1), wspec(2), wspec(3),
                  pl.BlockSpec((CONV_WIDTH, tc), lambda j, i: (0, j))],
        out_specs=pl.BlockSpec((tm, tc), lambda j, i: (i, j)),
        scratch_shapes=[pltpu.VMEM((tm + SUBLANE, tc), F32)],
        compiler_params=_cparams("parallel", "arbitrary"),
        name="conv_proj",
    )(xb, w_a, w_a, w_a, w_a, conv_w)


def _nsa_proj_kernel(x_ref, w_ref, hb_ref, ha_ref, *xb_ref):
    x = x_ref[...].astype(BF16)
    if xb_ref:
        xb_ref[0][...] = x
    nb = SLAB_W // PROJ_CHUNK
    for c in range(nb):
        cs = slice(c * PROJ_CHUNK, (c + 1) * PROJ_CHUNK)
        acc = _dot(x, w_ref[:, cs])
        if (c + 1) * PROJ_CHUNK <= HB_KS:
            acc = acc * Q_SCALE
        hb_ref[:, cs] = acc.astype(BF16)
    for c in range(nb):
        cs = slice(c * PROJ_CHUNK, (c + 1) * PROJ_CHUNK)
        ha_ref[:, cs] = _dot(x, w_ref[:, SLAB_W + c * PROJ_CHUNK:SLAB_W + (c + 1) * PROJ_CHUNK])


def _nsa_proj(x, w, wblock, tm):
    m = x.shape[0]
    row = pl.BlockSpec((tm, SLAB_W), lambda i: (i, 0))
    out_shape = [jax.ShapeDtypeStruct((m, SLAB_W), BF16), jax.ShapeDtypeStruct((m, SLAB_W), F32)]
    out_specs = [row, row]
    if x.dtype != BF16:
        out_shape.append(jax.ShapeDtypeStruct((m, D_MODEL), BF16))
        out_specs.append(pl.BlockSpec((tm, D_MODEL), lambda i: (i, 0)))
    return pl.pallas_call(
        _nsa_proj_kernel,
        out_shape=tuple(out_shape),
        grid=(m // tm,),
        in_specs=[pl.BlockSpec((tm, D_MODEL), lambda i: (i, 0)),
                  pl.BlockSpec((D_MODEL, 2 * SLAB_W), lambda i: (0, wblock), pipeline_mode=pl.Buffered(1))],
        out_specs=tuple(out_specs),
        compiler_params=_cparams("parallel"),
        name="nsa_proj",
    )(x, w)


def _compress_kernel(tok_ref, w1_ref, w2_ref, pos_ref, o_ref, b_ref, *, rows):
    half = CMP_STRIDE * HEAD_DIM
    x2 = jnp.concatenate(
        [tok_ref[pl.ds(l, rows, stride=CMP_STRIDE), :] for l in range(CMP_STRIDE)], axis=1).astype(BF16)
    lo = _dot(x2, w1_ref[0, 0:half, :])
    hi = _dot(x2, w1_ref[0, half:2 * half, :])
    b_ref[0:rows, :] = hi
    b_ref[rows:rows + SUBLANE, :] = jnp.zeros((SUBLANE, HEAD_DIM), F32)
    posb = _dot(jnp.broadcast_to(pos_ref[0], (SUBLANE, 2 * half)), w1_ref[0])[0:1, :]
    pre = lo + b_ref[1:rows + 1, :] + posb
    o_ref[0, 0] = _dot(_silu(pre).astype(BF16), w2_ref[0]).astype(BF16)


def _compress(ha, w1, w2, pos, batch, seq):
    rows = seq // CMP_STRIDE
    nkv = 2 * NSA_KV_GROUPS
    kc_block = HA_KC // HEAD_DIM
    return pl.pallas_call(
        functools.partial(_compress_kernel, rows=rows),
        out_shape=jax.ShapeDtypeStruct((batch, nkv, rows, HEAD_DIM), BF16),
        grid=(batch, nkv),
        in_specs=[pl.BlockSpec((seq, HEAD_DIM), lambda b, c: (b, kc_block + c)),
                  pl.BlockSpec((1, CMP_BLOCK * HEAD_DIM, HEAD_DIM), lambda b, c: (c // NSA_KV_GROUPS, 0, 0)),
                  pl.BlockSpec((1, HEAD_DIM, HEAD_DIM), lambda b, c: (c // NSA_KV_GROUPS, 0, 0)),
                  pl.BlockSpec((1, 1, CMP_BLOCK * HEAD_DIM), lambda b, c: (c // NSA_KV_GROUPS, 0, 0))],
        out_specs=pl.BlockSpec((1, 1, rows, HEAD_DIM), lambda b, c: (b, c, 0, 0)),
        scratch_shapes=[pltpu.VMEM((rows + SUBLANE, HEAD_DIM), F32)],
        compiler_params=_cparams("parallel", "arbitrary"),
        name="cmp_blocks",
    )(ha, w1, w2, pos)


def _t5_bucket(dist):
    n = jnp.maximum(dist, 0)
    max_exact = REL_BUCKETS // 2
    large = max_exact + (jnp.log(jnp.maximum(n, 1).astype(F32) / max_exact)
                         / math.log(REL_MAX_DIST / max_exact) * (REL_BUCKETS - max_exact)).astype(jnp.int32)
    large = jnp.minimum(large, REL_BUCKETS - 1)
    return jnp.where(n < max_exact, n, large)


def _table_lookup(dist, tab_ref, head):
    bkt = _t5_bucket(dist)
    acc = jnp.zeros(dist.shape, F32)
    for b in range(REL_BUCKETS):
        acc = jnp.where(bkt == b, tab_ref[b, head], acc)
    return acc


def _rel_bias(dist, valid, tab_ref, head):
    rows, cols = dist.shape
    if cols % LANE or REL_MAX_DIST > LANE:
        return jnp.where(valid, _table_lookup(dist, tab_ref, head), NEG_INF)
    lane_dist = lax.broadcasted_iota(jnp.int32, (SUBLANE, LANE), 1)
    near = jnp.broadcast_to(_table_lookup(lane_dist, tab_ref, head)[0:1, :], (rows, LANE))
    far = _table_lookup(jnp.full((SUBLANE, LANE), REL_MAX_DIST, jnp.int32), tab_ref, head)[0:1, 0:1]
    parts = []
    for c in range(cols // LANE):
        d = dist[:, c * LANE:(c + 1) * LANE]
        g = jnp.take_along_axis(near, jnp.clip(d, 0, LANE - 1), axis=1)
        parts.append(jnp.where(d >= REL_MAX_DIST, far, g))
    return jnp.where(valid, jnp.concatenate(parts, axis=1), NEG_INF)


def _bias_cmp_kernel(tab_ref, o_ref, *, tb, rows, n_cmp):
    head = pl.program_id(0)
    t = pl.program_id(1) * tb + lax.broadcasted_iota(jnp.int32, (tb, rows), 0)
    n = lax.broadcasted_iota(jnp.int32, (tb, rows), 1)
    dist = t - (n * CMP_STRIDE + CMP_BLOCK - 1)
    o_ref[0] = _rel_bias(dist, (dist >= 0) & (n < n_cmp), tab_ref, head) * LOG2E


def _bias_cmp(table, seq, tb):
    rows = seq // CMP_STRIDE
    n_cmp = (seq - CMP_BLOCK) // CMP_STRIDE + 1
    return pl.pallas_call(
        functools.partial(_bias_cmp_kernel, tb=tb, rows=rows, n_cmp=n_cmp),
        out_shape=jax.ShapeDtypeStruct((NSA_HEADS, seq, rows), F32),
        grid=(NSA_HEADS, seq // tb),
        in_specs=[pl.BlockSpec(memory_space=pltpu.SMEM)],
        out_specs=pl.BlockSpec((1, tb, rows), lambda h, i: (h, i, 0)),
        compiler_params=_cparams("parallel", "arbitrary"),
        name="bias_cmp",
    )(table)


def _bias_tiles_kernel(tab_ref, o_ref, *, tq):
    head = pl.program_id(0)
    kind = pl.program_id(1)
    ij = (lax.broadcasted_iota(jnp.int32, (tq, tq), 0) - lax.broadcasted_iota(jnp.int32, (tq, tq), 1))
    dist = jnp.where(kind == KIND_DIAG, ij, jnp.where(kind == KIND_SUB, tq + ij, WINDOW + ij))
    lo = jnp.where(kind == KIND_DIAG, 0, -tq)
    hi = jnp.where(kind == KIND_CORNER, 0, tq)
    far_dist = jnp.full((SUBLANE, LANE), tq + 1, jnp.int32)
    far = _rel_bias(far_dist, far_dist > 0, tab_ref, head)[0:1, 0:1]
    o_ref[0, 0] = (_rel_bias(dist, (ij >= lo) & (ij < hi), tab_ref, head) - far) * LOG2E


def _bias_tiles(table, tq):
    assert WINDOW % tq == 0 and tq + 1 >= REL_MAX_DIST
    return pl.pallas_call(
        functools.partial(_bias_tiles_kernel, tq=tq),
        out_shape=jax.ShapeDtypeStruct((NSA_KV_GROUPS, N_BIAS_KINDS, NSA_HPG * tq, tq), F32),
        grid=(NSA_HEADS, N_BIAS_KINDS),
        in_specs=[pl.BlockSpec(memory_space=pltpu.SMEM)],
        out_specs=pl.BlockSpec((1, 1, tq, tq), lambda h, k: (h // NSA_HPG, k, h % NSA_HPG, 0)),
        compiler_params=_cparams("parallel", "arbitrary"),
        name="bias_tiles",
    )(table)


def _cmp_attn_kernel(q_ref, kc_ref, vc_ref, bias_ref, ov_ref, oc_ref, sel_ref, *, tq, n_sel, n_top):
    kc = kc_ref[0, 0]
    vc = vc_ref[0, 0]
    psum = None
    for j in range(NSA_HPG):
        hs = slice(j * HEAD_DIM, (j + 1) * HEAD_DIM)
        bias = bias_ref[j]
        s = _dot_nt(q_ref[:, hs], kc) + bias
        m = jnp.max(s, axis=-1, keepdims=True)
        e = jnp.exp2(s - m)
        p = e / jnp.sum(e, axis=-1, keepdims=True)
        p = jnp.where(bias > 0.5 * NEG_INF, p, 0.0)
        oc_ref[:, hs] = _dot(p.astype(BF16), vc)
        psum = p if psum is None else psum + p
    ov = ov_ref[...]
    imp = None
    rem = psum
    for _ in range(3):
        piece = rem.astype(BF16)
        part = _dot_nt(ov, piece)
        imp = part if imp is None else imp + part
        rem = rem - piece.astype(F32)
    t = pl.program_id(1) * tq + lax.broadcasted_iota(jnp.int32, (n_sel, tq), 1)
    cur = jnp.right_shift(t, int(math.log2(SEL_BLOCK)))
    blk = lax.broadcasted_iota(jnp.int32, (n_sel, tq), 0)
    forced = (blk == 0) | (blk == cur) | (blk == cur - 1)
    imp = jnp.where(blk > cur, -1.0, jnp.where(forced, FORCED_SCORE, imp))
    groups = [imp[g * SUBLANE:(g + 1) * SUBLANE, :] for g in range(n_sel // SUBLANE)]
    ranks = [jnp.zeros((SUBLANE, tq), jnp.int32) for _ in groups]
    sub = lax.broadcasted_iota(jnp.int32, (SUBLANE, tq), 0)
    for i in range(n_sel):
        row = imp[i:i + 1, :]
        for g, x in enumerate(groups):
            if g * SUBLANE > i:
                ahead = row >= x
            elif (g + 1) * SUBLANE - 1 < i:
                ahead = row > x
            else:
                ahead = (row > x) | ((row == x) & (sub > i - g * SUBLANE))
            ranks[g] = ranks[g] + ahead.astype(jnp.int32)
    sel_t = jnp.where(jnp.concatenate(ranks, axis=0) < n_top, 0.0, NEG_INF)
    if n_sel < LANE:
        sel_t = jnp.concatenate([sel_t, jnp.zeros((LANE - n_sel, tq), F32)], axis=0)
    sel_ref[0, 0] = sel_t.T.astype(BF16)


def _cmp_attn(hb, kcv, bias_c, ov, batch, seq, tq):
    rows = seq // CMP_STRIDE
    n_sel = seq // SEL_BLOCK
    n_top = min(N_SELECT, n_sel)
    assert n_sel <= LANE and n_sel % SUBLANE == 0
    nq = seq // tq
    g_n = NSA_KV_GROUPS
    return pl.pallas_call(
        functools.partial(_cmp_attn_kernel, tq=tq, n_sel=n_sel, n_top=n_top),
        out_shape=(jax.ShapeDtypeStruct((batch * seq, D_NSA), F32),
                   jax.ShapeDtypeStruct((batch, g_n, seq, LANE), BF16)),
        grid=(g_n, nq, batch),
        in_specs=[pl.BlockSpec((tq, GROUP_W), lambda g, i, b: (b * nq + i, HB_Q // GROUP_W + g)),
                  pl.BlockSpec((1, 1, rows, HEAD_DIM), lambda g, i, b: (b, g, 0, 0)),
                  pl.BlockSpec((1, 1, rows, HEAD_DIM), lambda g, i, b: (b, g_n + g, 0, 0)),
                  pl.BlockSpec((NSA_HPG, tq, rows), lambda g, i, b: (g, i, 0)),
                  pl.BlockSpec((n_sel, rows), lambda g, i, b: (0, 0))],
        out_specs=(pl.BlockSpec((tq, GROUP_W), lambda g, i, b: (b * nq + i, g)),
                   pl.BlockSpec((1, 1, tq, LANE), lambda g, i, b: (b, g, i, 0))),
        compiler_params=_cparams("parallel", "parallel", "arbitrary"),
        name="cmp_attn",
    )(hb, kcv, kcv, bias_c, ov)


FLASH_ROWS = 128


def _flash_init(state):
    _, _, m_ref, _, acc_ref = state
    m_ref[...] = jnp.full(m_ref.shape, -3e38, F32)
    acc_ref[...] = jnp.zeros(acc_ref.shape, F32)


def _flash_scores(qa_ref, kdim, k, s_ref, h, tq):
    hr = slice(h * tq, (h + 1) * tq)
    s_ref[hr, :] = _dot_nt(qa_ref[hr, 0:kdim], k)


def _flash_step(qa_ref, bias_ref, kind, v, nxt, state, tq):
    s_ref, p_ref, m_ref, a_ref, acc_ref = state
    tk = s_ref.shape[1]
    for h in range(NSA_HPG):
        hr = slice(h * tq, (h + 1) * tq)
        for r in range(tq // FLASH_ROWS):
            rs = slice(h * tq + r * FLASH_ROWS, h * tq + (r + 1) * FLASH_ROWS)
            s = s_ref[rs, :]
            if kind is not None:
                s = s + bias_ref[0, kind, rs, :]
            m_old = m_ref[rs, :]
            m_new = jnp.maximum(m_old, jnp.max(s, axis=-1, keepdims=True))
            p_ref[rs, :] = jnp.exp2(s - jnp.tile(m_new, (1, tk // LANE))).astype(BF16)
            a_ref[rs, :] = jnp.exp2(m_old - m_new)
            m_ref[rs, :] = m_new
        if nxt is not None:
            _flash_scores(qa_ref, nxt[0], nxt[1], s_ref, h, tq)
        acc_ref[hr, :] = jnp.tile(a_ref[hr, :], (1, 2)) * acc_ref[hr, :] + _dot(p_ref[hr, :], v)


def _selwin_kernel(q_ref, ks_ref, vs_ref, kw_ref, vw_ref, oh_ref, sel_ref, bias_ref, oc_ref, gt_ref, bz_ref,
                   o_ref, qa_ref, s_ref, p_ref, m_ref, a_ref, acc_ref, os_ref, *, tq):
    qi = pl.program_id(2)
    state = (s_ref, p_ref, m_ref, a_ref, acc_ref)
    aug = 2 * HEAD_DIM
    nwin = WINDOW // tq
    ones = jnp.ones((tq, HEAD_DIM), BF16)
    for j in range(NSA_HPG):
        qa_ref[j * tq:(j + 1) * tq, 0:HEAD_DIM] = q_ref[:, j * HEAD_DIM:(j + 1) * HEAD_DIM]
        qa_ref[j * tq:(j + 1) * tq, HEAD_DIM:aug] = sel_ref[0, 0]

    def rows_of(kt):
        return pl.ds(pl.multiple_of(kt * tq, tq), tq)

    def sel_keys(kt):
        return aug, jnp.concatenate([ks_ref[rows_of(kt), :], oh_ref[rows_of(kt), :]], axis=1)

    def win_keys(kt):
        return HEAD_DIM, kw_ref[rows_of(kt), :]

    def values(v_ref, kt):
        return jnp.concatenate([v_ref[rows_of(kt), :], ones], axis=1)

    _flash_init(state)
    for h in range(NSA_HPG):
        _flash_scores(qa_ref, *sel_keys(0), s_ref, h, tq)

    def far_body(kt, carry):
        _flash_step(qa_ref, bias_ref, None, values(vs_ref, kt), sel_keys(kt + 1), state, tq)
        return carry

    lax.fori_loop(0, jnp.maximum(qi - 1, 0), far_body, 0)

    @pl.when(qi >= 1)
    def _():
        _flash_step(qa_ref, bias_ref, KIND_SUB, values(vs_ref, qi - 1), sel_keys(qi), state, tq)

    first_win = jnp.maximum(qi - nwin, 0)
    _flash_step(qa_ref, bias_ref, KIND_DIAG, values(vs_ref, qi), win_keys(first_win), state, tq)
    os_ref[...] = acc_ref[:, 0:HEAD_DIM] / acc_ref[:, HEAD_DIM:aug]

    _flash_init(state)
    for d in range(nwin, -1, -1):
        kind = KIND_CORNER if d == nwin else (KIND_DIAG if d == 0 else (KIND_SUB if d == 1 else None))

        def win_step(d=d, kind=kind):
            nxt = win_keys(qi - d + 1) if d > 0 else None
            _flash_step(qa_ref, bias_ref, kind, values(vw_ref, qi - d), nxt, state, tq)

        if d == 0:
            win_step()
        else:
            pl.when(qi >= d)(win_step)

    gate = _sigmoid(gt_ref[...])
    for j in range(NSA_HPG):
        hs = slice(j * HEAD_DIM, (j + 1) * HEAD_DIM)
        rs = slice(j * tq, (j + 1) * tq)
        o_w = acc_ref[rs, 0:HEAD_DIM] / acc_ref[rs, HEAD_DIM:aug]
        o = (gate[:, j:j + 1] * oc_ref[:, hs]
             + gate[:, NSA_HPG + j:NSA_HPG + j + 1] * os_ref[rs, :]
             + gate[:, 2 * NSA_HPG + j:2 * NSA_HPG + j + 1] * o_w)
        o_ref[:, hs] = (o * _silu(bz_ref[:, hs])).astype(BF16)


def _selwin_attn(hb, ha, onehot, sel, bias_t, o_c, batch, seq, tq):
    nq = seq // tq
    rows = NSA_HPG * tq
    assert 0 < WINDOW // tq and WINDOW % tq == 0
    once = pl.Buffered(1)
    kv_spec = lambda base: pl.BlockSpec((seq, HEAD_DIM), lambda b, g, i: (b, base // HEAD_DIM + g),
                                        pipeline_mode=once)
    row_g = lambda b, g, i: (b * nq + i, g)
    return pl.pallas_call(
        functools.partial(_selwin_kernel, tq=tq),
        out_shape=jax.ShapeDtypeStruct((batch * seq, D_NSA), BF16),
        grid=(batch, NSA_KV_GROUPS, nq),
        in_specs=[pl.BlockSpec((tq, GROUP_W), lambda b, g, i: (b * nq + i, HB_Q // GROUP_W + g)),
                  kv_spec(HB_KS), kv_spec(HB_VS), kv_spec(HB_KW), kv_spec(HB_VW),
                  pl.BlockSpec((seq, LANE), lambda b, g, i: (0, 0), pipeline_mode=once),
                  pl.BlockSpec((1, 1, tq, LANE), lambda b, g, i: (b, g, i, 0)),
                  pl.BlockSpec((1, N_BIAS_KINDS, rows, tq), lambda b, g, i: (g, 0, 0, 0), pipeline_mode=once),
                  pl.BlockSpec((tq, GROUP_W), row_g),
                  pl.BlockSpec((tq, LANE), lambda b, g, i: (b * nq + i, HA_GT // LANE + g)),
                  pl.BlockSpec((tq, GROUP_W), lambda b, g, i: (b * nq + i, HA_BZ // GROUP_W + g))],
        out_specs=pl.BlockSpec((tq, GROUP_W), row_g),
        scratch_shapes=[pltpu.VMEM((rows, 2 * HEAD_DIM), BF16),
                        pltpu.VMEM((rows, tq), F32),
                        pltpu.VMEM((rows, tq), BF16),
                        pltpu.VMEM((rows, LANE), F32),
                        pltpu.VMEM((rows, LANE), F32),
                        pltpu.VMEM((rows, 2 * HEAD_DIM), F32),
                        pltpu.VMEM((rows, HEAD_DIM), F32)],
        compiler_params=_cparams("parallel", "parallel", "arbitrary"),
        name="selwin_attn",
    )(hb, hb, hb, hb, hb, onehot, sel, bias_t, o_c, ha, ha)


def _even_weights(w):
    w_g = w[:, EV_GT:EV_BZ].reshape(D_MODEL, N_BRANCH, NSA_KV_GROUPS, NSA_HPG)
    w_g = jnp.transpose(w_g, (0, 2, 1, 3)).reshape(D_MODEL, NSA_KV_GROUPS, N_BRANCH * NSA_HPG)
    w_g = jnp.pad(w_g, ((0, 0), (0, 0), (0, LANE - N_BRANCH * NSA_HPG))).reshape(D_MODEL, NSA_KV_GROUPS * LANE)
    w_g = jnp.pad(w_g, ((0, 0), (0, SLAB_W - HA_GT - NSA_KV_GROUPS * LANE)))
    return jnp.concatenate([w[:, EV_A:EV_Q], w[:, EV_Q:EV_KC], w[:, EV_KS:EV_GT], w[:, EV_KC:EV_KS],
                            w[:, EV_BZ:EV_END], w_g], axis=1).astype(BF16)


def _overlap_matrix(seq):
    rows = seq // CMP_STRIDE
    n_cmp = (seq - CMP_BLOCK) // CMP_STRIDE + 1
    n_sel = seq // SEL_BLOCK
    cstart = np.arange(rows)[None, :] * CMP_STRIDE
    sstart = np.arange(n_sel)[:, None] * SEL_BLOCK
    ov = (cstart < sstart + SEL_BLOCK) & (cstart + CMP_BLOCK > sstart) & (np.arange(rows)[None, :] < n_cmp)
    return jnp.asarray(ov.astype(np.float32), dtype=BF16)


def _block_onehot(seq):
    blk = np.arange(seq)[:, None] // SEL_BLOCK
    return jnp.asarray((blk == np.arange(LANE)[None, :]).astype(np.float32), dtype=BF16)


def kernel(x, rel_bias_table, ln_g, ln_b, ev_w_in, ev_conv_w, ev_cmp_pos, ev_cmp_w1, ev_cmp_w2, ev_w_out,
           od_w_in, od_ln_g, od_ln_b, od_sgu_w, od_sgu_b, od_w_out):
    batch, seq, d = x.shape
    depth = ln_g.shape[0]
    alpha = (2 * depth) ** 0.25
    m = batch * seq
    tq = ATT_TILE
    assert d == D_MODEL and seq % 1024 == 0 and ev_w_in.shape[-1] == EV_END

    xf = x.reshape(m, d)
    xb = None
    bias_c = _bias_cmp(rel_bias_table, seq, tb=256)
    bias_t = _bias_tiles(rel_bias_table, tq)
    ov = _overlap_matrix(seq)
    onehot = _block_onehot(seq)

    for layer in range(depth):
        i = layer // 2
        g = ln_g[layer].reshape(1, d)
        b = ln_b[layer].reshape(1, d)
        if layer % 2 == 0:
            w_ev = _even_weights(ev_w_in[i])
            hb, ha, *cast = _nsa_proj(xf if xb is None else xb, w_ev, (EV_Q - EV_A) // (2 * SLAB_W), tm=512)
            xb = cast[0] if cast else xb
            y_a = _conv_proj(xb, w_ev, ev_conv_w[i], seq, tm=1024, tc=256)
            kcv = _compress(ha, ev_cmp_w1[i].astype(BF16), ev_cmp_w2[i].astype(BF16),
                            ev_cmp_pos[i].reshape(2, 1, CMP_BLOCK * HEAD_DIM).astype(BF16), batch, seq)
            o_c, sel = _cmp_attn(hb, kcv, bias_c, ov, batch, seq, CMP_TILE)
            y_b = _selwin_attn(hb, ha, onehot, sel, bias_t, o_c, batch, seq, tq)
            xf, xb = _outproj_ln(y_a, 0, y_b, 0, ev_w_out[i].astype(BF16), xf, g, b, alpha, 512, "outproj_even")
        else:
            y = _odd_mixer(xb, od_w_in[i].astype(BF16), od_ln_g[i].reshape(1, d), od_ln_b[i].reshape(1, d),
                           od_sgu_w[i], od_sgu_b[i].reshape(SGU_GROUPS, SGU_CHUNK, 1), tm=1024)
            xf, xb = _outproj_ln(y, 0, y, 1, od_w_out[i].astype(BF16), xf, g, b, alpha, 512, "outproj_odd")
    return xf.reshape(batch, seq, d)
```

```python
import functools
import math

import numpy as np
import jax
import jax.numpy as jnp
from jax import lax
from jax.experimental import pallas as pl
from jax.experimental.pallas import tpu as pltpu

F32 = jnp.float32
BF16 = jnp.bfloat16

D_MODEL = 2048
CONV_WIDTH = 3
D_CONV = 1024
NSA_HEADS = 8
NSA_KV_GROUPS = 2
NSA_HPG = NSA_HEADS // NSA_KV_GROUPS
HEAD_DIM = 128
D_NSA = NSA_HEADS * HEAD_DIM
D_KV = NSA_KV_GROUPS * HEAD_DIM
CMP_BLOCK = 32
CMP_STRIDE = 16
SEL_BLOCK = 64
N_SELECT = 16
WINDOW = 512
N_BRANCH = 3
D_SGU = D_MODEL
SGU_GROUPS = 8
SGU_CHUNK = 128
SGU_GROUP_DIM = D_SGU // SGU_GROUPS
REL_BUCKETS = 32
REL_MAX_DIST = 128
LN_EPS = 1e-5
NEG_INF = -1e30
FORCED_SCORE = 1e9
GROUP_W = NSA_HPG * HEAD_DIM
LOG2E = math.log2(math.e)
Q_SCALE = HEAD_DIM ** -0.5 * LOG2E

LANE = 128
SUBLANE = 8
VMEM_LIMIT = 56 * 1024 * 1024

EV_A = 0
EV_Q = 4 * D_CONV
EV_KC = EV_Q + D_NSA
EV_KS = EV_KC + 2 * D_KV
EV_GT = EV_KS + 4 * D_KV
EV_BZ = EV_GT + N_BRANCH * NSA_HEADS
EV_END = EV_BZ + D_NSA

HB_Q, HB_KS, HB_VS, HB_KW, HB_VW = 0, 1024, 1280, 1536, 1792
HA_KC, HA_BZ, HA_GT = 0, 512, 1536
SLAB_W = 2048
PROJ_CHUNK = 512

CMP_TILE = 512
ROW_SUB = 256
OUT_SUB = 128
ATT_TILE = 512
KIND_DIAG, KIND_SUB, KIND_CORNER = 0, 1, 2
N_BIAS_KINDS = 3


def _cparams(*sem):
    return pltpu.CompilerParams(dimension_semantics=sem, vmem_limit_bytes=VMEM_LIMIT)


def _sigmoid(x):
    return 1.0 / (1.0 + jnp.exp(-x))


def _silu(x):
    return x * _sigmoid(x)


def _gelu_tanh(x):
    c = math.sqrt(2.0 / math.pi)
    return x * (0.5 * (1.0 + jnp.tanh(c * (x + 0.044715 * (x * x * x)))))


def _dot_nt(a, b):
    return lax.dot_general(a, b, (((1,), (1,)), ((), ())), preferred_element_type=F32)


def _dot(a, b):
    return jnp.dot(a, b, preferred_element_type=F32)


def _layer_norm(z, g, b):
    mu = jnp.mean(z, axis=-1, keepdims=True)
    zc = z - mu
    var = jnp.mean(zc * zc, axis=-1, keepdims=True)
    return zc * lax.rsqrt(var + LN_EPS) * g + b


def _outproj_kernel(y1_ref, y2_ref, w1_ref, w2_ref, x_ref, g_ref, b_ref, o_ref, ob_ref, *, alpha):
    for r in range(x_ref.shape[0] // OUT_SUB):
        rs = slice(r * OUT_SUB, (r + 1) * OUT_SUB)
        y = _dot(y1_ref[rs, :], w1_ref[...]) + _dot(y2_ref[rs, :], w2_ref[...])
        out = _layer_norm(alpha * x_ref[rs, :] + y, g_ref[...], b_ref[...])
        o_ref[rs, :] = out
        ob_ref[rs, :] = out.astype(BF16)


def _outproj_ln(y1, y1_col, y2, y2_col, w_out, x, g, b, alpha, tm, name):
    m, d = x.shape
    kh = w_out.shape[0] // 2
    return pl.pallas_call(
        functools.partial(_outproj_kernel, alpha=alpha),
        out_shape=(jax.ShapeDtypeStruct((m, d), F32), jax.ShapeDtypeStruct((m, d), BF16)),
        grid=(m // tm,),
        in_specs=[pl.BlockSpec((tm, kh), lambda i: (i, y1_col)),
                  pl.BlockSpec((tm, kh), lambda i: (i, y2_col)),
                  pl.BlockSpec((kh, d), lambda i: (0, 0)),
                  pl.BlockSpec((kh, d), lambda i: (1, 0)),
                  pl.BlockSpec((tm, d), lambda i: (i, 0)),
                  pl.BlockSpec((1, d), lambda i: (0, 0)),
                  pl.BlockSpec((1, d), lambda i: (0, 0))],
        out_specs=(pl.BlockSpec((tm, d), lambda i: (i, 0)),
                   pl.BlockSpec((tm, d), lambda i: (i, 0))),
        compiler_params=_cparams("parallel"),
        name=name,
    )(y1, y2, w_out, w_out, x, g, b)


def _odd_kernel(x_ref, wv_ref, wu_ref, wz_ref, g_ref, b_ref, sw_ref, sb_ref, o_ref, vs_ref, mu_ref, rstd_ref,
                *, tm):
    step = pl.program_id(1)
    nchunk = D_SGU // PROJ_CHUNK

    @pl.when(step == 0)
    def _():
        for r in range(tm // ROW_SUB):
            rs = slice(r * ROW_SUB, (r + 1) * ROW_SUB)
            x = x_ref[rs, :]
            tot = None
            for c in range(nchunk):
                v = _gelu_tanh(_dot(x, wv_ref[:, c * PROJ_CHUNK:(c + 1) * PROJ_CHUNK]))
                vs_ref[c, rs, :] = v
                part = jnp.sum(v, axis=-1, keepdims=True)
                tot = part if tot is None else tot + part
            mu = tot * (1.0 / D_SGU)
            sq = None
            for c in range(nchunk):
                vc = vs_ref[c, rs, :] - mu
                part = jnp.sum(vc * vc, axis=-1, keepdims=True)
                sq = part if sq is None else sq + part
            mu_ref[rs, :] = mu
            rstd_ref[rs, :] = lax.rsqrt(sq * (1.0 / D_SGU) + LN_EPS)

    row = lax.broadcasted_iota(jnp.int32, (SGU_CHUNK, SGU_CHUNK), 0)
    col = lax.broadcasted_iota(jnp.int32, (SGU_CHUNK, SGU_CHUNK), 1)
    ngrp = PROJ_CHUNK // SGU_GROUP_DIM
    wgs = [jnp.where(col <= row, sw_ref[g], 0.0).astype(BF16) for g in range(ngrp)]
    for r in range(tm // ROW_SUB):
        rs = slice(r * ROW_SUB, (r + 1) * ROW_SUB)
        x = x_ref[rs, :]
        u = _gelu_tanh(_dot(x, wu_ref[...]))
        z = _dot(x, wz_ref[...])
        vn = ((vs_ref[step, rs, :] - mu_ref[rs, :]) * rstd_ref[rs, :] * g_ref[...] + b_ref[...]).astype(BF16)
        for g in range(ngrp):
            cs = slice(g * SGU_GROUP_DIM, (g + 1) * SGU_GROUP_DIM)
            for c in range(ROW_SUB // SGU_CHUNK):
                ls = slice(c * SGU_CHUNK, (c + 1) * SGU_CHUNK)
                os_ = slice(r * ROW_SUB + c * SGU_CHUNK, r * ROW_SUB + (c + 1) * SGU_CHUNK)
                mixed = _dot(wgs[g], vn[ls, cs]) + sb_ref[g]
                o_ref[os_, cs] = (u[ls, cs] * mixed * _silu(z[ls, cs])).astype(BF16)


def _odd_mixer(xb, w_in, ln_g, ln_b, sgu_w, sgu_b, tm):
    m = xb.shape[0]
    cw = PROJ_CHUNK
    nstep = D_SGU // cw
    gps = cw // SGU_GROUP_DIM
    return pl.pallas_call(
        functools.partial(_odd_kernel, tm=tm),
        out_shape=jax.ShapeDtypeStruct((m, D_SGU), BF16),
        grid=(m // tm, nstep),
        in_specs=[pl.BlockSpec((tm, D_MODEL), lambda i, s: (i, 0)),
                  pl.BlockSpec((D_MODEL, D_SGU), lambda i, s: (0, 1)),
                  pl.BlockSpec((D_MODEL, cw), lambda i, s: (0, s)),
                  pl.BlockSpec((D_MODEL, cw), lambda i, s: (0, 2 * nstep + s)),
                  pl.BlockSpec((1, cw), lambda i, s: (0, s)),
                  pl.BlockSpec((1, cw), lambda i, s: (0, s)),
                  pl.BlockSpec((gps, SGU_CHUNK, SGU_CHUNK), lambda i, s: (s, 0, 0)),
                  pl.BlockSpec((gps, SGU_CHUNK, 1), lambda i, s: (s, 0, 0))],
        out_specs=pl.BlockSpec((tm, cw), lambda i, s: (i, s)),
        scratch_shapes=[pltpu.VMEM((nstep, tm, cw), F32),
                        pltpu.VMEM((tm, 1), F32),
                        pltpu.VMEM((tm, 1), F32)],
        compiler_params=_cparams("parallel", "arbitrary"),
        name="odd_mixer",
    )(xb, w_in, w_in, w_in, ln_g, ln_b, sgu_w, sgu_b)


def _conv_proj_kernel(x_ref, wh_ref, wb_ref, wc_ref, wz_ref, cw_ref, o_ref, u_ref, *, tm, tiles_per_seq):
    i = pl.program_id(1)
    nsub = tm // ROW_SUB

    @pl.when(i % tiles_per_seq == 0)
    def _():
        u_ref[0:SUBLANE, :] = jnp.zeros((SUBLANE, u_ref.shape[1]), F32)

    for r in range(nsub):
        rs = slice(r * ROW_SUB, (r + 1) * ROW_SUB)
        x = x_ref[rs, :]
        u_ref[SUBLANE + r * ROW_SUB:SUBLANE + (r + 1) * ROW_SUB, :] = _dot(x, wc_ref[...]) * _dot(x, wh_ref[...])
    for r in range(nsub):
        rs = slice(r * ROW_SUB, (r + 1) * ROW_SUB)
        x = x_ref[rs, :]
        conv = cw_ref[CONV_WIDTH - 1:CONV_WIDTH, :] * u_ref[SUBLANE + r * ROW_SUB:SUBLANE + (r + 1) * ROW_SUB, :]
        for k in range(CONV_WIDTH - 1):
            lo = SUBLANE + r * ROW_SUB - (CONV_WIDTH - 1 - k)
            conv = conv + cw_ref[k:k + 1, :] * u_ref[lo:lo + ROW_SUB, :]
        o_ref[rs, :] = (_dot(x, wb_ref[...]) * conv * _silu(_dot(x, wz_ref[...]))).astype(BF16)
    u_ref[0:SUBLANE, :] = u_ref[tm:tm + SUBLANE, :]


def _conv_proj(xb, w_a, conv_w, seq, tm, tc):
    m = xb.shape[0]
    nct = D_CONV // tc

    def wspec(part):
        return pl.BlockSpec((D_MODEL, tc), lambda j, i: (0, part * nct + j))

    return pl.pallas_call(
        functools.partial(_conv_proj_kernel, tm=tm, tiles_per_seq=seq // tm),
        out_shape=jax.ShapeDtypeStruct((m, D_CONV), BF16),
        grid=(nct, m // tm),
        in_specs=[pl.BlockSpec((tm, D_MODEL), lambda j, i: (i, 0)),
                  wspec(0), wspec(1), wspec(2), wspec(3),
                  pl.BlockSpec((CONV_WIDTH, tc), lambda j, i: (0, j))],
        out_specs=pl.BlockSpec((tm, tc), lambda j, i: (i, j)),
        scratch_shapes=[pltpu.VMEM((tm + SUBLANE, tc), F32)],
        compiler_params=_cparams("parallel", "arbitrary"),
        name="conv_proj",
    )(xb, w_a, w_a, w_a, w_a, conv_w)


def _nsa_proj_kernel(x_ref, w_ref, hb_ref, ha_ref, *xb_ref):
    x = x_ref[...].astype(BF16)
    if xb_ref:
        xb_ref[0][...] = x
    nb = SLAB_W // PROJ_CHUNK
    for c in range(nb):
        cs = slice(c * PROJ_CHUNK, (c + 1) * PROJ_CHUNK)
        acc = _dot(x, w_ref[:, cs])
        if (c + 1) * PROJ_CHUNK <= HB_KS:
            acc = acc * Q_SCALE
        hb_ref[:, cs] = acc.astype(BF16)
    for c in range(nb):
        cs = slice(c * PROJ_CHUNK, (c + 1) * PROJ_CHUNK)
        ha_ref[:, cs] = _dot(x, w_ref[:, SLAB_W + c * PROJ_CHUNK:SLAB_W + (c + 1) * PROJ_CHUNK])


def _nsa_proj(x, w, wblock, tm):
    m = x.shape[0]
    row = pl.BlockSpec((tm, SLAB_W), lambda i: (i, 0))
    out_shape = [jax.ShapeDtypeStruct((m, SLAB_W), BF16), jax.ShapeDtypeStruct((m, SLAB_W), F32)]
    out_specs = [row, row]
    if x.dtype != BF16:
        out_shape.append(jax.ShapeDtypeStruct((m, D_MODEL), BF16))
        out_specs.append(pl.BlockSpec((tm, D_MODEL), lambda i: (i, 0)))
    return pl.pallas_call(
        _nsa_proj_kernel,
        out_shape=tuple(out_shape),
        grid=(m // tm,),
        in_specs=[pl.BlockSpec((tm, D_MODEL), lambda i: (i, 0)),
                  pl.BlockSpec((D_MODEL, 2 * SLAB_W), lambda i: (0, wblock), pipeline_mode=pl.Buffered(1))],
        out_specs=tuple(out_specs),
        compiler_params=_cparams("parallel"),
        name="nsa_proj",
    )(x, w)


def _compress_kernel(tok_ref, w1_ref, w2_ref, pos_ref, o_ref, b_ref, *, rows):
    half = CMP_STRIDE * HEAD_DIM
    x2 = jnp.concatenate(
        [tok_ref[pl.ds(l, rows, stride=CMP_STRIDE), :] for l in range(CMP_STRIDE)], axis=1).astype(BF16)
    lo = _dot(x2, w1_ref[0, 0:half, :])
    hi = _dot(x2, w1_ref[0, half:2 * half, :])
    b_ref[0:rows, :] = hi
    b_ref[rows:rows + SUBLANE, :] = jnp.zeros((SUBLANE, HEAD_DIM), F32)
    posb = _dot(jnp.broadcast_to(pos_ref[0], (SUBLANE, 2 * half)), w1_ref[0])[0:1, :]
    pre = lo + b_ref[1:rows + 1, :] + posb
    o_ref[0, 0] = _dot(_silu(pre).astype(BF16), w2_ref[0]).astype(BF16)


def _compress(ha, w1, w2, pos, batch, seq):
    rows = seq // CMP_STRIDE
    nkv = 2 * NSA_KV_GROUPS
    kc_block = HA_KC // HEAD_DIM
    return pl.pallas_call(
        functools.partial(_compress_kernel, rows=rows),
        out_shape=jax.ShapeDtypeStruct((batch, nkv, rows, HEAD_DIM), BF16),
        grid=(batch, nkv),
        in_specs=[pl.BlockSpec((seq, HEAD_DIM), lambda b, c: (b, kc_block + c)),
                  pl.BlockSpec((1, CMP_BLOCK * HEAD_DIM, HEAD_DIM), lambda b, c: (c // NSA_KV_GROUPS, 0, 0)),
                  pl.BlockSpec((1, HEAD_DIM, HEAD_DIM), lambda b, c: (c // NSA_KV_GROUPS, 0, 0)),
                  pl.BlockSpec((1, 1, CMP_BLOCK * HEAD_DIM), lambda b, c: (c // NSA_KV_GROUPS, 0, 0))],
        out_specs=pl.BlockSpec((1, 1, rows, HEAD_DIM), lambda b, c: (b, c, 0, 0)),
        scratch_shapes=[pltpu.VMEM((rows + SUBLANE, HEAD_DIM), F32)],
        compiler_params=_cparams("parallel", "arbitrary"),
        name="cmp_blocks",
    )(ha, w1, w2, pos)


def _t5_bucket(dist):
    n = jnp.maximum(dist, 0)
    max_exact = REL_BUCKETS // 2
    large = max_exact + (jnp.log(jnp.maximum(n, 1).astype(F32) / max_exact)
                         / math.log(REL_MAX_DIST / max_exact) * (REL_BUCKETS - max_exact)).astype(jnp.int32)
    large = jnp.minimum(large, REL_BUCKETS - 1)
    return jnp.where(n < max_exact, n, large)


def _table_lookup(dist, tab_ref, head):
    bkt = _t5_bucket(dist)
    acc = jnp.zeros(dist.shape, F32)
    for b in range(REL_BUCKETS):
        acc = jnp.where(bkt == b, tab_ref[b, head], acc)
    return acc


def _rel_bias(dist, valid, tab_ref, head):
    rows, cols = dist.shape
    if cols % LANE or REL_MAX_DIST > LANE:
        return jnp.where(valid, _table_lookup(dist, tab_ref, head), NEG_INF)
    lane_dist = lax.broadcasted_iota(jnp.int32, (SUBLANE, LANE), 1)
    near = jnp.broadcast_to(_table_lookup(lane_dist, tab_ref, head)[0:1, :], (rows, LANE))
    far = _table_lookup(jnp.full((SUBLANE, LANE), REL_MAX_DIST, jnp.int32), tab_ref, head)[0:1, 0:1]
    parts = []
    for c in range(cols // LANE):
        d = dist[:, c * LANE:(c + 1) * LANE]
        g = jnp.take_along_axis(near, jnp.clip(d, 0, LANE - 1), axis=1)
        parts.append(jnp.where(d >= REL_MAX_DIST, far, g))
    return jnp.where(valid, jnp.concatenate(parts, axis=1), NEG_INF)


def _bias_cmp_kernel(tab_ref, o_ref, *, tb, rows, n_cmp):
    head = pl.program_id(0)
    t = pl.program_id(1) * tb + lax.broadcasted_iota(jnp.int32, (tb, rows), 0)
    n = lax.broadcasted_iota(jnp.int32, (tb, rows), 1)
    dist = t - (n * CMP_STRIDE + CMP_BLOCK - 1)
    o_ref[0] = _rel_bias(dist, (dist >= 0) & (n < n_cmp), tab_ref, head) * LOG2E


def _bias_cmp(table, seq, tb):
    rows = seq // CMP_STRIDE
    n_cmp = (seq - CMP_BLOCK) // CMP_STRIDE + 1
    return pl.pallas_call(
        functools.partial(_bias_cmp_kernel, tb=tb, rows=rows, n_cmp=n_cmp),
        out_shape=jax.ShapeDtypeStruct((NSA_HEADS, seq, rows), F32),
        grid=(NSA_HEADS, seq // tb),
        in_specs=[pl.BlockSpec(memory_space=pltpu.SMEM)],
        out_specs=pl.BlockSpec((1, tb, rows), lambda h, i: (h, i, 0)),
        compiler_params=_cparams("parallel", "arbitrary"),
        name="bias_cmp",
    )(table)


def _bias_tiles_kernel(tab_ref, o_ref, *, tq):
    head = pl.program_id(0)
    kind = pl.program_id(1)
    ij = (lax.broadcasted_iota(jnp.int32, (tq, tq), 0) - lax.broadcasted_iota(jnp.int32, (tq, tq), 1))
    dist = jnp.where(kind == KIND_DIAG, ij, jnp.where(kind == KIND_SUB, tq + ij, WINDOW + ij))
    lo = jnp.where(kind == KIND_DIAG, 0, -tq)
    hi = jnp.where(kind == KIND_CORNER, 0, tq)
    far_dist = jnp.full((SUBLANE, LANE), tq + 1, jnp.int32)
    far = _rel_bias(far_dist, far_dist > 0, tab_ref, head)[0:1, 0:1]
    o_ref[0, 0] = (_rel_bias(dist, (ij >= lo) & (ij < hi), tab_ref, head) - far) * LOG2E


def _bias_tiles(table, tq):
    assert WINDOW % tq == 0 and tq + 1 >= REL_MAX_DIST
    return pl.pallas_call(
        functools.partial(_bias_tiles_kernel, tq=tq),
        out_shape=jax.ShapeDtypeStruct((NSA_KV_GROUPS, N_BIAS_KINDS, NSA_HPG * tq, tq), F32),
        grid=(NSA_HEADS, N_BIAS_KINDS),
        in_specs=[pl.BlockSpec(memory_space=pltpu.SMEM)],
        out_specs=pl.BlockSpec((1, 1, tq, tq), lambda h, k: (h // NSA_HPG, k, h % NSA_HPG, 0)),
        compiler_params=_cparams("parallel", "arbitrary"),
        name="bias_tiles",
    )(table)


def _cmp_attn_kernel(q_ref, kc_ref, vc_ref, bias_ref, ov_ref, oc_ref, sel_ref, *, tq, n_sel, n_top):
    kc = kc_ref[0, 0]
    vc = vc_ref[0, 0]
    psum = None
    for j in range(NSA_HPG):
        hs = slice(j * HEAD_DIM, (j + 1) * HEAD_DIM)
        bias = bias_ref[j]
        s = _dot_nt(q_ref[:, hs], kc) + bias
        m = jnp.max(s, axis=-1, keepdims=True)
        e = jnp.exp2(s - m)
        p = e / jnp.sum(e, axis=-1, keepdims=True)
        p = jnp.where(bias > 0.5 * NEG_INF, p, 0.0)
        oc_ref[:, hs] = _dot(p.astype(BF16), vc)
        psum = p if psum is None else psum + p
    ov = ov_ref[...]
    imp = None
    rem = psum
    for _ in range(3):
        piece = rem.astype(BF16)
        part = _dot_nt(ov, piece)
        imp = part if imp is None else imp + part
        rem = rem - piece.astype(F32)
    t = pl.program_id(1) * tq + lax.broadcasted_iota(jnp.int32, (n_sel, tq), 1)
    cur = jnp.right_shift(t, int(math.log2(SEL_BLOCK)))
    blk = lax.broadcasted_iota(jnp.int32, (n_sel, tq), 0)
    forced = (blk == 0) | (blk == cur) | (blk == cur - 1)
    imp = jnp.where(blk > cur, -1.0, jnp.where(forced, FORCED_SCORE, imp))
    groups = [imp[g * SUBLANE:(g + 1) * SUBLANE, :] for g in range(n_sel // SUBLANE)]
    ranks = [jnp.zeros((SUBLANE, tq), jnp.int32) for _ in groups]
    sub = lax.broadcasted_iota(jnp.int32, (SUBLANE, tq), 0)
    for i in range(n_sel):
        row = imp[i:i + 1, :]
        for g, x in enumerate(groups):
            if g * SUBLANE > i:
                ahead = row >= x
            elif (g + 1) * SUBLANE - 1 < i:
                ahead = row > x
            else:
                ahead = (row > x) | ((row == x) & (sub > i - g * SUBLANE))
            ranks[g] = ranks[g] + ahead.astype(jnp.int32)
    sel_t = jnp.where(jnp.concatenate(ranks, axis=0) < n_top, 0.0, NEG_INF)
    if n_sel < LANE:
        sel_t = jnp.concatenate([sel_t, jnp.zeros((LANE - n_sel, tq), F32)], axis=0)
    sel_ref[0, 0] = sel_t.T.astype(BF16)


def _cmp_attn(hb, kcv, bias_c, ov, batch, seq, tq):
    rows = seq // CMP_STRIDE
    n_sel = seq // SEL_BLOCK
    n_top = min(N_SELECT, n_sel)
    assert n_sel <= LANE and n_sel % SUBLANE == 0
    nq = seq // tq
    g_n = NSA_KV_GROUPS
    return pl.pallas_call(
        functools.partial(_cmp_attn_kernel, tq=tq, n_sel=n_sel, n_top=n_top),
        out_shape=(jax.ShapeDtypeStruct((batch * seq, D_NSA), F32),
                   jax.ShapeDtypeStruct((batch, g_n, seq, LANE), BF16)),
        grid=(g_n, nq, batch),
        in_specs=[pl.BlockSpec((tq, GROUP_W), lambda g, i, b: (b * nq + i, HB_Q // GROUP_W + g)),
                  pl.BlockSpec((1, 1, rows, HEAD_DIM), lambda g, i, b: (b, g, 0, 0)),
                  pl.BlockSpec((1, 1, rows, HEAD_DIM), lambda g, i, b: (b, g_n + g, 0, 0)),
                  pl.BlockSpec((NSA_HPG, tq, rows), lambda g, i, b: (g, i, 0)),
                  pl.BlockSpec((n_sel, rows), lambda g, i, b: (0, 0))],
        out_specs=(pl.BlockSpec((tq, GROUP_W), lambda g, i, b: (b * nq + i, g)),
                   pl.BlockSpec((1, 1, tq, LANE), lambda g, i, b: (b, g, i, 0))),
        compiler_params=_cparams("parallel", "parallel", "arbitrary"),
        name="cmp_attn",
    )(hb, kcv, kcv, bias_c, ov)


FLASH_ROWS = 128


def _flash_init(state):
    _, _, m_ref, _, acc_ref = state
    m_ref[...] = jnp.full(m_ref.shape, -3e38, F32)
    acc_ref[...] = jnp.zeros(acc_ref.shape, F32)


def _flash_scores(qa_ref, kdim, k, s_ref, h, tq):
    hr = slice(h * tq, (h + 1) * tq)
    s_ref[hr, :] = _dot_nt(qa_ref[hr, 0:kdim], k)


def _flash_step(qa_ref, bias_ref, kind, v, nxt, state, tq):
    s_ref, p_ref, m_ref, a_ref, acc_ref = state
    tk = s_ref.shape[1]
    for h in range(NSA_HPG):
        hr = slice(h * tq, (h + 1) * tq)
        for r in range(tq // FLASH_ROWS):
            rs = slice(h * tq + r * FLASH_ROWS, h * tq + (r + 1) * FLASH_ROWS)
            s = s_ref[rs, :]
            if kind is not None:
                s = s + bias_ref[0, kind, rs, :]
            m_old = m_ref[rs, :]
            m_new = jnp.maximum(m_old, jnp.max(s, axis=-1, keepdims=True))
            p_ref[rs, :] = jnp.exp2(s - jnp.tile(m_new, (1, tk // LANE))).astype(BF16)
            a_ref[rs, :] = jnp.exp2(m_old - m_new)
            m_ref[rs, :] = m_new
        if nxt is not None:
            _flash_scores(qa_ref, nxt[0], nxt[1], s_ref, h, tq)
        acc_ref[hr, :] = jnp.tile(a_ref[hr, :], (1, 2)) * acc_ref[hr, :] + _dot(p_ref[hr, :], v)


def _selwin_kernel(q_ref, ks_ref, vs_ref, kw_ref, vw_ref, oh_ref, sel_ref, bias_ref, oc_ref, gt_ref, bz_ref,
                   o_ref, qa_ref, s_ref, p_ref, m_ref, a_ref, acc_ref, os_ref, *, tq):
    qi = pl.program_id(2)
    state = (s_ref, p_ref, m_ref, a_ref, acc_ref)
    aug = 2 * HEAD_DIM
    ones = jnp.ones((tq, HEAD_DIM), BF16)
    for j in range(NSA_HPG):
        qa_ref[j * tq:(j + 1) * tq, 0:HEAD_DIM] = q_ref[:, j * HEAD_DIM:(j + 1) * HEAD_DIM]
        qa_ref[j * tq:(j + 1) * tq, HEAD_DIM:aug] = sel_ref[0, 0]

    def rows_of(kt):
        return pl.ds(pl.multiple_of(kt * tq, tq), tq)

    def sel_keys(kt):
        return aug, jnp.concatenate([ks_ref[rows_of(kt), :], oh_ref[rows_of(kt), :]], axis=1)

    def win_keys(kt):
        return HEAD_DIM, kw_ref[rows_of(kt), :]

    def values(v_ref, kt):
        return jnp.concatenate([v_ref[rows_of(kt), :], ones], axis=1)

    _flash_init(state)
    for h in range(NSA_HPG):
        _flash_scores(qa_ref, *sel_keys(0), s_ref, h, tq)

    def far_body(kt, carry):
        _flash_step(qa_ref, bias_ref, None, values(vs_ref, kt), sel_keys(kt + 1), state, tq)
        return carry

    lax.fori_loop(0, jnp.maximum(qi - 1, 0), far_body, 0)

    def near_tiles(first):
        if not first:
            _flash_step(qa_ref, bias_ref, KIND_SUB, values(vs_ref, qi - 1), sel_keys(qi), state, tq)
        _flash_step(qa_ref, bias_ref, KIND_DIAG, values(vs_ref, qi), win_keys(jnp.maximum(qi - 1, 0)), state, tq)
        os_ref[...] = acc_ref[:, 0:HEAD_DIM] / acc_ref[:, HEAD_DIM:aug]
        _flash_init(state)
        if not first:
            _flash_step(qa_ref, bias_ref, KIND_CORNER, values(vw_ref, qi - 1), win_keys(qi), state, tq)
        _flash_step(qa_ref, bias_ref, KIND_DIAG, values(vw_ref, qi), None, state, tq)
        gate = _sigmoid(gt_ref[...])
        for j in range(NSA_HPG):
            hs = slice(j * HEAD_DIM, (j + 1) * HEAD_DIM)
            rs = slice(j * tq, (j + 1) * tq)
            o_w = acc_ref[rs, 0:HEAD_DIM] / acc_ref[rs, HEAD_DIM:aug]
            o = (gate[:, j:j + 1] * oc_ref[:, hs]
                 + gate[:, NSA_HPG + j:NSA_HPG + j + 1] * os_ref[rs, :]
                 + gate[:, 2 * NSA_HPG + j:2 * NSA_HPG + j + 1] * o_w)
            o_ref[:, hs] = (o * _silu(bz_ref[:, hs])).astype(BF16)

    pl.when(qi == 0)(functools.partial(near_tiles, True))
    pl.when(qi > 0)(functools.partial(near_tiles, False))


def _selwin_attn(hb, ha, onehot, sel, bias_t, o_c, batch, seq, tq):
    nq = seq // tq
    rows = NSA_HPG * tq
    assert tq == WINDOW
    once = pl.Buffered(1)
    kv_spec = lambda base: pl.BlockSpec((seq, HEAD_DIM), lambda g, b, i: (b, base // HEAD_DIM + g))
    row_g = lambda g, b, i: (b * nq + i, g)
    return pl.pallas_call(
        functools.partial(_selwin_kernel, tq=tq),
        out_shape=jax.ShapeDtypeStruct((batch * seq, D_NSA), BF16),
        grid=(NSA_KV_GROUPS, batch, nq),
        in_specs=[pl.BlockSpec((tq, GROUP_W), lambda g, b, i: (b * nq + i, HB_Q // GROUP_W + g)),
                  kv_spec(HB_KS), kv_spec(HB_VS), kv_spec(HB_KW), kv_spec(HB_VW),
                  pl.BlockSpec((seq, LANE), lambda g, b, i: (0, 0), pipeline_mode=once),
                  pl.BlockSpec((1, 1, tq, LANE), lambda g, b, i: (b, g, i, 0)),
                  pl.BlockSpec((1, N_BIAS_KINDS, rows, tq), lambda g, b, i: (g, 0, 0, 0), pipeline_mode=once),
                  pl.BlockSpec((tq, GROUP_W), row_g),
                  pl.BlockSpec((tq, LANE), lambda g, b, i: (b * nq + i, HA_GT // LANE + g)),
                  pl.BlockSpec((tq, GROUP_W), lambda g, b, i: (b * nq + i, HA_BZ // GROUP_W + g))],
        out_specs=pl.BlockSpec((tq, GROUP_W), row_g),
        scratch_shapes=[pltpu.VMEM((rows, 2 * HEAD_DIM), BF16),
                        pltpu.VMEM((rows, tq), F32),
                        pltpu.VMEM((rows, tq), BF16),
                        pltpu.VMEM((rows, LANE), F32),
                        pltpu.VMEM((rows, LANE), F32),
                        pltpu.VMEM((rows, 2 * HEAD_DIM), F32),
                        pltpu.VMEM((rows, HEAD_DIM), F32)],
        compiler_params=_cparams("parallel", "parallel", "arbitrary"),
        name="selwin_attn",
    )(hb, hb, hb, hb, hb, onehot, sel, bias_t, o_c, ha, ha)


def _even_weights(w):
    w_g = w[:, EV_GT:EV_BZ].reshape(D_MODEL, N_BRANCH, NSA_KV_GROUPS, NSA_HPG)
    w_g = jnp.transpose(w_g, (0, 2, 1, 3)).reshape(D_MODEL, NSA_KV_GROUPS, N_BRANCH * NSA_HPG)
    w_g = jnp.pad(w_g, ((0, 0), (0, 0), (0, LANE - N_BRANCH * NSA_HPG))).reshape(D_MODEL, NSA_KV_GROUPS * LANE)
    w_g = jnp.pad(w_g, ((0, 0), (0, SLAB_W - HA_GT - NSA_KV_GROUPS * LANE)))
    return jnp.concatenate([w[:, EV_A:EV_Q], w[:, EV_Q:EV_KC], w[:, EV_KS:EV_GT], w[:, EV_KC:EV_KS],
                            w[:, EV_BZ:EV_END], w_g], axis=1).astype(BF16)


def _overlap_matrix(seq):
    rows = seq // CMP_STRIDE
    n_cmp = (seq - CMP_BLOCK) // CMP_STRIDE + 1
    n_sel = seq // SEL_BLOCK
    cstart = np.arange(rows)[None, :] * CMP_STRIDE
    sstart = np.arange(n_sel)[:, None] * SEL_BLOCK
    ov = (cstart < sstart + SEL_BLOCK) & (cstart + CMP_BLOCK > sstart) & (np.arange(rows)[None, :] < n_cmp)
    return jnp.asarray(ov.astype(np.float32), dtype=BF16)


def _block_onehot(seq):
    blk = np.arange(seq)[:, None] // SEL_BLOCK
    return jnp.asarray((blk == np.arange(LANE)[None, :]).astype(np.float32), dtype=BF16)


def kernel(x, rel_bias_table, ln_g, ln_b, ev_w_in, ev_conv_w, ev_cmp_pos, ev_cmp_w1, ev_cmp_w2, ev_w_out,
           od_w_in, od_ln_g, od_ln_b, od_sgu_w, od_sgu_b, od_w_out):
    batch, seq, d = x.shape
    depth = ln_g.shape[0]
    alpha = (2 * depth) ** 0.25
    m = batch * seq
    tq = ATT_TILE
    assert d == D_MODEL and seq % 1024 == 0 and ev_w_in.shape[-1] == EV_END

    xf = x.reshape(m, d)
    xb = None
    bias_c = _bias_cmp(rel_bias_table, seq, tb=1024)
    bias_t = _bias_tiles(rel_bias_table, tq)
    ov = _overlap_matrix(seq)
    onehot = _block_onehot(seq)

    for layer in range(depth):
        i = layer // 2
        g = ln_g[layer].reshape(1, d)
        b = ln_b[layer].reshape(1, d)
        if layer % 2 == 0:
            w_ev = _even_weights(ev_w_in[i])
            hb, ha, *cast = _nsa_proj(xf if xb is None else xb, w_ev, (EV_Q - EV_A) // (2 * SLAB_W), tm=512)
            xb = cast[0] if cast else xb
            y_a = _conv_proj(xb, w_ev, ev_conv_w[i], seq, tm=1024, tc=256)
            kcv = _compress(ha, ev_cmp_w1[i].astype(BF16), ev_cmp_w2[i].astype(BF16),
                            ev_cmp_pos[i].reshape(2, 1, CMP_BLOCK * HEAD_DIM).astype(BF16), batch, seq)
            o_c, sel = _cmp_attn(hb, kcv, bias_c, ov, batch, seq, CMP_TILE)
            y_b = _selwin_attn(hb, ha, onehot, sel, bias_t, o_c, batch, seq, tq)
            xf, xb = _outproj_ln(y_a, 0, y_b, 0, ev_w_out[i].astype(BF16), xf, g, b, alpha, 512, "outproj_even")
        else:
            y = _odd_mixer(xb, od_w_in[i].astype(BF16), od_ln_g[i].reshape(1, d), od_ln_b[i].reshape(1, d),
                           od_sgu_w[i], od_sgu_b[i].reshape(SGU_GROUPS, SGU_CHUNK, 1), tm=1024)
            xf, xb = _outproj_ln(y, 0, y, 1, od_w_out[i].astype(BF16), xf, g, b, alpha, 512, "outproj_odd")
    return xf.reshape(batch, seq, d)
```

```python
import functools
import math

import numpy as np
import jax
import jax.numpy as jnp
from jax import lax
from jax.experimental import pallas as pl
from jax.experimental.pallas import tpu as pltpu

F32 = jnp.float32
BF16 = jnp.bfloat16

D_MODEL = 2048
CONV_WIDTH = 3
D_CONV = 1024
NSA_HEADS = 8
NSA_KV_GROUPS = 2
NSA_HPG = NSA_HEADS // NSA_KV_GROUPS
HEAD_DIM = 128
D_NSA = NSA_HEADS * HEAD_DIM
D_KV = NSA_KV_GROUPS * HEAD_DIM
CMP_BLOCK = 32
CMP_STRIDE = 16
SEL_BLOCK = 64
N_SELECT = 16
WINDOW = 512
N_BRANCH = 3
D_SGU = D_MODEL
SGU_GROUPS = 8
SGU_CHUNK = 128
SGU_GROUP_DIM = D_SGU // SGU_GROUPS
REL_BUCKETS = 32
REL_MAX_DIST = 128
LN_EPS = 1e-5
NEG_INF = -1e30
FORCED_SCORE = 1e9
GROUP_W = NSA_HPG * HEAD_DIM
LOG2E = math.log2(math.e)
Q_SCALE = HEAD_DIM ** -0.5 * LOG2E

LANE = 128
SUBLANE = 8
VMEM_LIMIT = 56 * 1024 * 1024

EV_A = 0
EV_Q = 4 * D_CONV
EV_KC = EV_Q + D_NSA
EV_KS = EV_KC + 2 * D_KV
EV_GT = EV_KS + 4 * D_KV
EV_BZ = EV_GT + N_BRANCH * NSA_HEADS
EV_END = EV_BZ + D_NSA

HB_Q, HB_KS, HB_VS, HB_KW, HB_VW = 0, 1024, 1280, 1536, 1792
HA_KC, HA_BZ, HA_GT = 0, 512, 1536
HB_W = 2048
HA_W = HA_GT + NSA_KV_GROUPS * LANE
PROJ_CHUNK = 512

CMP_TILE = 512
ROW_SUB = 256
OUT_SUB = 128
ATT_TILE = 512
KIND_DIAG, KIND_SUB, KIND_CORNER = 0, 1, 2
N_BIAS_KINDS = 3


def _cparams(*sem):
    return pltpu.CompilerParams(dimension_semantics=sem, vmem_limit_bytes=VMEM_LIMIT)


def _sigmoid(x):
    return 1.0 / (1.0 + jnp.exp(-x))


def _silu(x):
    return x * _sigmoid(x)


def _gelu_tanh(x):
    c = math.sqrt(2.0 / math.pi)
    return x * (0.5 * (1.0 + jnp.tanh(c * (x + 0.044715 * (x * x * x)))))


def _dot_nt(a, b):
    return lax.dot_general(a, b, (((1,), (1,)), ((), ())), preferred_element_type=F32)


def _dot(a, b):
    return jnp.dot(a, b, preferred_element_type=F32)


def _layer_norm(z, g, b):
    mu = jnp.mean(z, axis=-1, keepdims=True)
    zc = z - mu
    var = jnp.mean(zc * zc, axis=-1, keepdims=True)
    return zc * lax.rsqrt(var + LN_EPS) * g + b


def _outproj_kernel(y1_ref, y2_ref, w1_ref, w2_ref, x_ref, g_ref, b_ref, o_ref, ob_ref, *, alpha):
    for r in range(x_ref.shape[0] // OUT_SUB):
        rs = slice(r * OUT_SUB, (r + 1) * OUT_SUB)
        y = _dot(y1_ref[rs, :], w1_ref[...]) + _dot(y2_ref[rs, :], w2_ref[...])
        out = _layer_norm(alpha * x_ref[rs, :] + y, g_ref[...], b_ref[...])
        o_ref[rs, :] = out
        ob_ref[rs, :] = out.astype(BF16)


def _outproj_ln(y1, y1_col, y2, y2_col, w_out, x, g, b, alpha, tm, name):
    m, d = x.shape
    kh = w_out.shape[0] // 2
    return pl.pallas_call(
        functools.partial(_outproj_kernel, alpha=alpha),
        out_shape=(jax.ShapeDtypeStruct((m, d), F32), jax.ShapeDtypeStruct((m, d), BF16)),
        grid=(m // tm,),
        in_specs=[pl.BlockSpec((tm, kh), lambda i: (i, y1_col)),
                  pl.BlockSpec((tm, kh), lambda i: (i, y2_col)),
                  pl.BlockSpec((kh, d), lambda i: (0, 0)),
                  pl.BlockSpec((kh, d), lambda i: (1, 0)),
                  pl.BlockSpec((tm, d), lambda i: (i, 0)),
                  pl.BlockSpec((1, d), lambda i: (0, 0)),
                  pl.BlockSpec((1, d), lambda i: (0, 0))],
        out_specs=(pl.BlockSpec((tm, d), lambda i: (i, 0)),
                   pl.BlockSpec((tm, d), lambda i: (i, 0))),
        compiler_params=_cparams("parallel"),
        name=name,
    )(y1, y2, w_out, w_out, x, g, b)


def _odd_kernel(x_ref, wv_ref, wu_ref, wz_ref, g_ref, b_ref, sw_ref, sb_ref, o_ref, vs_ref, mu_ref, rstd_ref,
                *, tm):
    step = pl.program_id(1)
    nchunk = D_SGU // PROJ_CHUNK
    ngrp = PROJ_CHUNK // SGU_GROUP_DIM

    def project_v():
        for r in range(tm // ROW_SUB):
            rs = slice(r * ROW_SUB, (r + 1) * ROW_SUB)
            x = x_ref[rs, :]
            tot = None
            for c in range(nchunk):
                v = _gelu_tanh(_dot(x, wv_ref[:, c * PROJ_CHUNK:(c + 1) * PROJ_CHUNK]))
                vs_ref[c, rs, :] = v
                part = jnp.sum(v, axis=-1, keepdims=True)
                tot = part if tot is None else tot + part
            mu = tot * (1.0 / D_SGU)
            sq = None
            for c in range(nchunk):
                vc = vs_ref[c, rs, :] - mu
                part = jnp.sum(vc * vc, axis=-1, keepdims=True)
                sq = part if sq is None else sq + part
            mu_ref[rs, :] = mu
            rstd_ref[rs, :] = lax.rsqrt(sq * (1.0 / D_SGU) + LN_EPS)

    def mix_columns(chunk):
        row = lax.broadcasted_iota(jnp.int32, (SGU_CHUNK, SGU_CHUNK), 0)
        col = lax.broadcasted_iota(jnp.int32, (SGU_CHUNK, SGU_CHUNK), 1)
        wgs = [jnp.where(col <= row, sw_ref[g], 0.0).astype(BF16) for g in range(ngrp)]
        for r in range(tm // ROW_SUB):
            rs = slice(r * ROW_SUB, (r + 1) * ROW_SUB)
            x = x_ref[rs, :]
            u = _gelu_tanh(_dot(x, wu_ref[...]))
            z = _dot(x, wz_ref[...])
            vn = ((vs_ref[chunk, rs, :] - mu_ref[rs, :]) * rstd_ref[rs, :] * g_ref[...] + b_ref[...]).astype(BF16)
            for g in range(ngrp):
                cs = slice(g * SGU_GROUP_DIM, (g + 1) * SGU_GROUP_DIM)
                for c in range(ROW_SUB // SGU_CHUNK):
                    ls = slice(c * SGU_CHUNK, (c + 1) * SGU_CHUNK)
                    os_ = slice(r * ROW_SUB + c * SGU_CHUNK, r * ROW_SUB + (c + 1) * SGU_CHUNK)
                    mixed = _dot(wgs[g], vn[ls, cs]) + sb_ref[g]
                    o_ref[os_, cs] = (u[ls, cs] * mixed * _silu(z[ls, cs])).astype(BF16)

    @pl.when(step == 0)
    def _():
        project_v()
        mix_columns(0)

    @pl.when(step > 0)
    def _():
        mix_columns(step)


def _odd_mixer(xb, w_in, ln_g, ln_b, sgu_w, sgu_b, tm):
    m = xb.shape[0]
    cw = PROJ_CHUNK
    nstep = D_SGU // cw
    gps = cw // SGU_GROUP_DIM
    return pl.pallas_call(
        functools.partial(_odd_kernel, tm=tm),
        out_shape=jax.ShapeDtypeStruct((m, D_SGU), BF16),
        grid=(m // tm, nstep),
        in_specs=[pl.BlockSpec((tm, D_MODEL), lambda i, s: (i, 0)),
                  pl.BlockSpec((D_MODEL, D_SGU), lambda i, s: (0, 1)),
                  pl.BlockSpec((D_MODEL, cw), lambda i, s: (0, s)),
                  pl.BlockSpec((D_MODEL, cw), lambda i, s: (0, 2 * nstep + s)),
                  pl.BlockSpec((1, cw), lambda i, s: (0, s)),
                  pl.BlockSpec((1, cw), lambda i, s: (0, s)),
                  pl.BlockSpec((gps, SGU_CHUNK, SGU_CHUNK), lambda i, s: (s, 0, 0)),
                  pl.BlockSpec((gps, SGU_CHUNK, 1), lambda i, s: (s, 0, 0))],
        out_specs=pl.BlockSpec((tm, cw), lambda i, s: (i, s)),
        scratch_shapes=[pltpu.VMEM((nstep, tm, cw), F32),
                        pltpu.VMEM((tm, 1), F32),
                        pltpu.VMEM((tm, 1), F32)],
        compiler_params=_cparams("parallel", "arbitrary"),
        name="odd_mixer",
    )(xb, w_in, w_in, w_in, ln_g, ln_b, sgu_w, sgu_b)


def _conv_proj_kernel(x_ref, wh_ref, wb_ref, wc_ref, wz_ref, cw_ref, o_ref, u_ref, *, tm, tiles_per_seq):
    i = pl.program_id(1)
    nsub = tm // ROW_SUB

    @pl.when(i % tiles_per_seq == 0)
    def _():
        u_ref[0:SUBLANE, :] = jnp.zeros((SUBLANE, u_ref.shape[1]), F32)

    for r in range(nsub):
        rs = slice(r * ROW_SUB, (r + 1) * ROW_SUB)
        x = x_ref[rs, :]
        u_ref[SUBLANE + r * ROW_SUB:SUBLANE + (r + 1) * ROW_SUB, :] = _dot(x, wc_ref[...]) * _dot(x, wh_ref[...])
    for r in range(nsub):
        rs = slice(r * ROW_SUB, (r + 1) * ROW_SUB)
        x = x_ref[rs, :]
        conv = cw_ref[CONV_WIDTH - 1:CONV_WIDTH, :] * u_ref[SUBLANE + r * ROW_SUB:SUBLANE + (r + 1) * ROW_SUB, :]
        for k in range(CONV_WIDTH - 1):
            lo = SUBLANE + r * ROW_SUB - (CONV_WIDTH - 1 - k)
            conv = conv + cw_ref[k:k + 1, :] * u_ref[lo:lo + ROW_SUB, :]
        o_ref[rs, :] = (_dot(x, wb_ref[...]) * conv * _silu(_dot(x, wz_ref[...]))).astype(BF16)
    u_ref[0:SUBLANE, :] = u_ref[tm:tm + SUBLANE, :]


def _conv_proj(xb, w_a, conv_w, seq, tm, tc):
    m = xb.shape[0]
    nct = D_CONV // tc

    def wspec(part):
        return pl.BlockSpec((D_MODEL, tc), lambda j, i: (0, part * nct + j))

    return pl.pallas_call(
        functools.partial(_conv_proj_kernel, tm=tm, tiles_per_seq=seq // tm),
        out_shape=jax.ShapeDtypeStruct((m, D_CONV), BF16),
        grid=(nct, m // tm),
        in_specs=[pl.BlockSpec((tm, D_MODEL), lambda j, i: (i, 0)),
                  wspec(0), wspec(1), wspec(2), wspec(3),
                  pl.BlockSpec((CONV_WIDTH, tc), lambda j, i: (0, j))],
        out_specs=pl.BlockSpec((tm, tc), lambda j, i: (i, j)),
        scratch_shapes=[pltpu.VMEM((tm + SUBLANE, tc), F32)],
        compiler_params=_cparams("parallel", "arbitrary"),
        name="conv_proj",
    )(xb, w_a, w_a, w_a, w_a, conv_w)


def _nsa_proj_kernel(x_ref, wq_ref, wkc_ref, wks_ref, wkw_ref, wt_ref, hb_ref, ha_ref, *xb_ref):
    x = x_ref[...].astype(BF16)
    if xb_ref:
        xb_ref[0][...] = x
    cw = PROJ_CHUNK
    for c in range(D_NSA // cw):
        hb_ref[:, c * cw:(c + 1) * cw] = (_dot(x, wq_ref[:, c * cw:(c + 1) * cw]) * Q_SCALE).astype(BF16)
    hb_ref[:, HB_KS:HB_KS + cw] = _dot(x, wks_ref[...]).astype(BF16)
    hb_ref[:, HB_KW:HB_KW + cw] = _dot(x, wkw_ref[...]).astype(BF16)
    ha_ref[:, HA_KC:HA_KC + cw] = _dot(x, wkc_ref[...])
    for lo in range(0, HA_W - HA_BZ, cw):
        hi = min(lo + cw, HA_W - HA_BZ)
        ha_ref[:, HA_BZ + lo:HA_BZ + hi] = _dot(x, wt_ref[:, lo:hi])


def _nsa_proj(x, w, w_tail, tm):
    m = x.shape[0]
    once = pl.Buffered(1)
    cw = PROJ_CHUNK
    assert EV_Q % D_NSA == 0 and EV_KC % cw == 0 and EV_KS % cw == 0 and 2 * D_KV == cw
    out_shape = [jax.ShapeDtypeStruct((m, HB_W), BF16), jax.ShapeDtypeStruct((m, HA_W), F32)]
    out_specs = [pl.BlockSpec((tm, HB_W), lambda i: (i, 0)), pl.BlockSpec((tm, HA_W), lambda i: (i, 0))]
    if x.dtype != BF16:
        out_shape.append(jax.ShapeDtypeStruct((m, D_MODEL), BF16))
        out_specs.append(pl.BlockSpec((tm, D_MODEL), lambda i: (i, 0)))
    return pl.pallas_call(
        _nsa_proj_kernel,
        out_shape=tuple(out_shape),
        grid=(m // tm,),
        in_specs=[pl.BlockSpec((tm, D_MODEL), lambda i: (i, 0)),
                  pl.BlockSpec((D_MODEL, D_NSA), lambda i: (0, EV_Q // D_NSA), pipeline_mode=once),
                  pl.BlockSpec((D_MODEL, cw), lambda i: (0, EV_KC // cw), pipeline_mode=once),
                  pl.BlockSpec((D_MODEL, cw), lambda i: (0, EV_KS // cw), pipeline_mode=once),
                  pl.BlockSpec((D_MODEL, cw), lambda i: (0, EV_KS // cw + 1), pipeline_mode=once),
                  pl.BlockSpec((D_MODEL, HA_W - HA_BZ), lambda i: (0, 0), pipeline_mode=once)],
        out_specs=tuple(out_specs),
        compiler_params=_cparams("parallel"),
        name="nsa_proj",
    )(x, w, w, w, w, w_tail)


def _compress_kernel(tok_ref, w1_ref, w2_ref, pos_ref, o_ref, b_ref, *, rows):
    half = CMP_STRIDE * HEAD_DIM
    x2 = jnp.concatenate(
        [tok_ref[pl.ds(l, rows, stride=CMP_STRIDE), :] for l in range(CMP_STRIDE)], axis=1).astype(BF16)
    lo = _dot(x2, w1_ref[0, 0:half, :])
    hi = _dot(x2, w1_ref[0, half:2 * half, :])
    b_ref[0:rows, :] = hi
    b_ref[rows:rows + SUBLANE, :] = jnp.zeros((SUBLANE, HEAD_DIM), F32)
    posb = _dot(jnp.broadcast_to(pos_ref[0], (SUBLANE, 2 * half)), w1_ref[0])[0:1, :]
    pre = lo + b_ref[1:rows + 1, :] + posb
    o_ref[0, 0] = _dot(_silu(pre).astype(BF16), w2_ref[0]).astype(BF16)


def _compress(ha, w1, w2, pos, batch, seq):
    rows = seq // CMP_STRIDE
    nkv = 2 * NSA_KV_GROUPS
    kc_block = HA_KC // HEAD_DIM
    return pl.pallas_call(
        functools.partial(_compress_kernel, rows=rows),
        out_shape=jax.ShapeDtypeStruct((batch, nkv, rows, HEAD_DIM), BF16),
        grid=(batch, nkv),
        in_specs=[pl.BlockSpec((seq, HEAD_DIM), lambda b, c: (b, kc_block + c)),
                  pl.BlockSpec((1, CMP_BLOCK * HEAD_DIM, HEAD_DIM), lambda b, c: (c // NSA_KV_GROUPS, 0, 0)),
                  pl.BlockSpec((1, HEAD_DIM, HEAD_DIM), lambda b, c: (c // NSA_KV_GROUPS, 0, 0)),
                  pl.BlockSpec((1, 1, CMP_BLOCK * HEAD_DIM), lambda b, c: (c // NSA_KV_GROUPS, 0, 0))],
        out_specs=pl.BlockSpec((1, 1, rows, HEAD_DIM), lambda b, c: (b, c, 0, 0)),
        scratch_shapes=[pltpu.VMEM((rows + SUBLANE, HEAD_DIM), F32)],
        compiler_params=_cparams("parallel", "arbitrary"),
        name="cmp_blocks",
    )(ha, w1, w2, pos)


def _t5_bucket(dist):
    n = jnp.maximum(dist, 0)
    max_exact = REL_BUCKETS // 2
    large = max_exact + (jnp.log(jnp.maximum(n, 1).astype(F32) / max_exact)
                         / math.log(REL_MAX_DIST / max_exact) * (REL_BUCKETS - max_exact)).astype(jnp.int32)
    large = jnp.minimum(large, REL_BUCKETS - 1)
    return jnp.where(n < max_exact, n, large)


def _table_lookup(dist, tab_ref, head):
    bkt = _t5_bucket(dist)
    acc = jnp.zeros(dist.shape, F32)
    for b in range(REL_BUCKETS):
        acc = jnp.where(bkt == b, tab_ref[b, head], acc)
    return acc


def _rel_bias(dist, valid, tab_ref, head):
    rows, cols = dist.shape
    if cols % LANE or REL_MAX_DIST > LANE:
        return jnp.where(valid, _table_lookup(dist, tab_ref, head), NEG_INF)
    lane_dist = lax.broadcasted_iota(jnp.int32, (SUBLANE, LANE), 1)
    near = jnp.broadcast_to(_table_lookup(lane_dist, tab_ref, head)[0:1, :], (rows, LANE))
    far = _table_lookup(jnp.full((SUBLANE, LANE), REL_MAX_DIST, jnp.int32), tab_ref, head)[0:1, 0:1]
    parts = []
    for c in range(cols // LANE):
        d = dist[:, c * LANE:(c + 1) * LANE]
        g = jnp.take_along_axis(near, jnp.clip(d, 0, LANE - 1), axis=1)
        parts.append(jnp.where(d >= REL_MAX_DIST, far, g))
    return jnp.where(valid, jnp.concatenate(parts, axis=1), NEG_INF)


def _bias_cmp_kernel(tab_ref, o_ref, *, tb, rows, n_cmp):
    head = pl.program_id(0)
    t = pl.program_id(1) * tb + lax.broadcasted_iota(jnp.int32, (tb, rows), 0)
    n = lax.broadcasted_iota(jnp.int32, (tb, rows), 1)
    dist = t - (n * CMP_STRIDE + CMP_BLOCK - 1)
    o_ref[0] = _rel_bias(dist, (dist >= 0) & (n < n_cmp), tab_ref, head) * LOG2E


def _bias_cmp(table, seq, tb):
    rows = seq // CMP_STRIDE
    n_cmp = (seq - CMP_BLOCK) // CMP_STRIDE + 1
    return pl.pallas_call(
        functools.partial(_bias_cmp_kernel, tb=tb, rows=rows, n_cmp=n_cmp),
        out_shape=jax.ShapeDtypeStruct((NSA_HEADS, seq, rows), F32),
        grid=(NSA_HEADS, seq // tb),
        in_specs=[pl.BlockSpec(memory_space=pltpu.SMEM)],
        out_specs=pl.BlockSpec((1, tb, rows), lambda h, i: (h, i, 0)),
        compiler_params=_cparams("parallel", "arbitrary"),
        name="bias_cmp",
    )(table)


def _bias_tiles_kernel(tab_ref, o_ref, *, tq):
    head = pl.program_id(0)
    kind = pl.program_id(1)
    ij = (lax.broadcasted_iota(jnp.int32, (tq, tq), 0) - lax.broadcasted_iota(jnp.int32, (tq, tq), 1))
    dist = jnp.where(kind == KIND_DIAG, ij, jnp.where(kind == KIND_SUB, tq + ij, WINDOW + ij))
    lo = jnp.where(kind == KIND_DIAG, 0, -tq)
    hi = jnp.where(kind == KIND_CORNER, 0, tq)
    far_dist = jnp.full((SUBLANE, LANE), tq + 1, jnp.int32)
    far = _rel_bias(far_dist, far_dist > 0, tab_ref, head)[0:1, 0:1]
    o_ref[0, 0] = (_rel_bias(dist, (ij >= lo) & (ij < hi), tab_ref, head) - far) * LOG2E


def _bias_tiles(table, tq):
    assert WINDOW % tq == 0 and tq + 1 >= REL_MAX_DIST
    return pl.pallas_call(
        functools.partial(_bias_tiles_kernel, tq=tq),
        out_shape=jax.ShapeDtypeStruct((NSA_KV_GROUPS, N_BIAS_KINDS, NSA_HPG * tq, tq), F32),
        grid=(NSA_HEADS, N_BIAS_KINDS),
        in_specs=[pl.BlockSpec(memory_space=pltpu.SMEM)],
        out_specs=pl.BlockSpec((1, 1, tq, tq), lambda h, k: (h // NSA_HPG, k, h % NSA_HPG, 0)),
        compiler_params=_cparams("parallel", "arbitrary"),
        name="bias_tiles",
    )(table)


def _cmp_attn_kernel(q_ref, kc_ref, vc_ref, bias_ref, ov_ref, oc_ref, sel_ref, *, tq, n_sel, n_top):
    kc = kc_ref[0, 0]
    vc = vc_ref[0, 0]
    psum = None
    for j in range(NSA_HPG):
        hs = slice(j * HEAD_DIM, (j + 1) * HEAD_DIM)
        bias = bias_ref[j]
        s = _dot_nt(q_ref[:, hs], kc) + bias
        m = jnp.max(s, axis=-1, keepdims=True)
        e = jnp.exp2(s - m)
        p = e / jnp.sum(e, axis=-1, keepdims=True)
        p = jnp.where(bias > 0.5 * NEG_INF, p, 0.0)
        oc_ref[:, hs] = _dot(p.astype(BF16), vc)
        psum = p if psum is None else psum + p
    ov = ov_ref[...]
    imp = None
    rem = psum
    for _ in range(3):
        piece = rem.astype(BF16)
        part = _dot_nt(ov, piece)
        imp = part if imp is None else imp + part
        rem = rem - piece.astype(F32)
    t = pl.program_id(1) * tq + lax.broadcasted_iota(jnp.int32, (n_sel, tq), 1)
    cur = jnp.right_shift(t, int(math.log2(SEL_BLOCK)))
    blk = lax.broadcasted_iota(jnp.int32, (n_sel, tq), 0)
    forced = (blk == 0) | (blk == cur) | (blk == cur - 1)
    imp = jnp.where(blk > cur, -1.0, jnp.where(forced, FORCED_SCORE, imp))
    groups = [imp[g * SUBLANE:(g + 1) * SUBLANE, :] for g in range(n_sel // SUBLANE)]
    ranks = [jnp.zeros((SUBLANE, tq), jnp.int32) for _ in groups]
    sub = lax.broadcasted_iota(jnp.int32, (SUBLANE, tq), 0)
    for i in range(n_sel):
        row = imp[i:i + 1, :]
        for g, x in enumerate(groups):
            if g * SUBLANE > i:
                ahead = row >= x
            elif (g + 1) * SUBLANE - 1 < i:
                ahead = row > x
            else:
                ahead = (row > x) | ((row == x) & (sub > i - g * SUBLANE))
            ranks[g] = ranks[g] + ahead.astype(jnp.int32)
    sel_t = jnp.where(jnp.concatenate(ranks, axis=0) < n_top, 0.0, NEG_INF)
    if n_sel < LANE:
        sel_t = jnp.concatenate([sel_t, jnp.zeros((LANE - n_sel, tq), F32)], axis=0)
    sel_ref[0, 0] = sel_t.T.astype(BF16)


def _cmp_attn(hb, kcv, bias_c, ov, batch, seq, tq):
    rows = seq // CMP_STRIDE
    n_sel = seq // SEL_BLOCK
    n_top = min(N_SELECT, n_sel)
    assert n_sel <= LANE and n_sel % SUBLANE == 0
    nq = seq // tq
    g_n = NSA_KV_GROUPS
    return pl.pallas_call(
        functools.partial(_cmp_attn_kernel, tq=tq, n_sel=n_sel, n_top=n_top),
        out_shape=(jax.ShapeDtypeStruct((batch * seq, D_NSA), F32),
                   jax.ShapeDtypeStruct((batch, g_n, seq, LANE), BF16)),
        grid=(g_n, nq, batch),
        in_specs=[pl.BlockSpec((tq, GROUP_W), lambda g, i, b: (b * nq + i, HB_Q // GROUP_W + g)),
                  pl.BlockSpec((1, 1, rows, HEAD_DIM), lambda g, i, b: (b, g, 0, 0)),
                  pl.BlockSpec((1, 1, rows, HEAD_DIM), lambda g, i, b: (b, g_n + g, 0, 0)),
                  pl.BlockSpec((NSA_HPG, tq, rows), lambda g, i, b: (g, i, 0)),
                  pl.BlockSpec((n_sel, rows), lambda g, i, b: (0, 0))],
        out_specs=(pl.BlockSpec((tq, GROUP_W), lambda g, i, b: (b * nq + i, g)),
                   pl.BlockSpec((1, 1, tq, LANE), lambda g, i, b: (b, g, i, 0))),
        compiler_params=_cparams("parallel", "parallel", "arbitrary"),
        name="cmp_attn",
    )(hb, kcv, kcv, bias_c, ov)


FLASH_ROWS = 128


def _flash_init(state):
    _, _, m_ref, _, acc_ref = state
    m_ref[...] = jnp.full(m_ref.shape, -3e38, F32)
    acc_ref[...] = jnp.zeros(acc_ref.shape, F32)


def _tile_parts(kind, tq):
    half = tq // 2
    if kind == KIND_DIAG:
        return ((0, half, 0, half), (half, half, 0, tq))
    if kind == KIND_CORNER:
        return ((0, half, 0, tq), (half, half, half, tq))
    return ((0, tq, 0, tq),)


def _flash_scores(qa_ref, kdim, k, kind, s_ref, h, tq):
    for r0, nr, lo, hi in _tile_parts(kind, tq):
        rs = slice(h * tq + r0, h * tq + r0 + nr)
        s_ref[rs, lo:hi] = _dot_nt(qa_ref[rs, 0:kdim], k[lo:hi, :])


def _flash_step(qa_ref, bias_ref, kind, v, nxt, state, tq):
    s_ref, p_ref, m_ref, a_ref, acc_ref = state
    parts = _tile_parts(kind, tq)
    for h in range(NSA_HPG):
        for r0, nr, lo, hi in parts:
            for r in range(nr // FLASH_ROWS):
                rs = slice(h * tq + r0 + r * FLASH_ROWS, h * tq + r0 + (r + 1) * FLASH_ROWS)
                s = s_ref[rs, lo:hi]
                if kind is not None:
                    s = s + bias_ref[0, kind, rs, lo:hi]
                m_old = m_ref[rs, :]
                m_new = jnp.maximum(m_old, jnp.max(s, axis=-1, keepdims=True))
                p_ref[rs, lo:hi] = jnp.exp2(s - jnp.tile(m_new, (1, (hi - lo) // LANE))).astype(BF16)
                a_ref[rs, :] = jnp.exp2(m_old - m_new)
                m_ref[rs, :] = m_new
        if nxt is not None:
            _flash_scores(qa_ref, *nxt, s_ref, h, tq)
        for r0, nr, lo, hi in parts:
            rs = slice(h * tq + r0, h * tq + r0 + nr)
            acc_ref[rs, :] = jnp.tile(a_ref[rs, :], (1, 2)) * acc_ref[rs, :] + _dot(p_ref[rs, lo:hi], v[lo:hi, :])


def _selwin_kernel(q_ref, ks_ref, vs_ref, kw_ref, vw_ref, oh_ref, sel_ref, bias_ref, oc_ref, gt_ref, bz_ref,
                   o_ref, qa_ref, s_ref, p_ref, m_ref, a_ref, acc_ref, os_ref, *, tq):
    qi = pl.program_id(2)
    state = (s_ref, p_ref, m_ref, a_ref, acc_ref)
    aug = 2 * HEAD_DIM
    ones = jnp.ones((tq, HEAD_DIM), BF16)
    for j in range(NSA_HPG):
        qa_ref[j * tq:(j + 1) * tq, 0:HEAD_DIM] = q_ref[:, j * HEAD_DIM:(j + 1) * HEAD_DIM]
        qa_ref[j * tq:(j + 1) * tq, HEAD_DIM:aug] = sel_ref[0, 0]

    def rows_of(kt):
        return pl.ds(pl.multiple_of(kt * tq, tq), tq)

    def sel_keys(kt, kind):
        return aug, jnp.concatenate([ks_ref[rows_of(kt), :], oh_ref[rows_of(kt), :]], axis=1), kind

    def win_keys(kt, kind):
        return HEAD_DIM, kw_ref[rows_of(kt), :], kind

    def values(v_ref, kt):
        return jnp.concatenate([v_ref[rows_of(kt), :], ones], axis=1)

    _flash_init(state)
    for h in range(NSA_HPG):
        _flash_scores(qa_ref, *sel_keys(0, None), s_ref, h, tq)

    def far_body(kt, carry):
        _flash_step(qa_ref, bias_ref, None, values(vs_ref, kt), sel_keys(kt + 1, None), state, tq)
        return carry

    lax.fori_loop(0, jnp.maximum(qi - 1, 0), far_body, 0)

    def near_tiles(first):
        if not first:
            _flash_step(qa_ref, bias_ref, KIND_SUB, values(vs_ref, qi - 1), sel_keys(qi, KIND_DIAG), state, tq)
        first_win = win_keys(0, KIND_DIAG) if first else win_keys(qi - 1, KIND_CORNER)
        _flash_step(qa_ref, bias_ref, KIND_DIAG, values(vs_ref, qi), first_win, state, tq)
        os_ref[...] = acc_ref[:, 0:HEAD_DIM] / acc_ref[:, HEAD_DIM:aug]
        _flash_init(state)
        if not first:
            _flash_step(qa_ref, bias_ref, KIND_CORNER, values(vw_ref, qi - 1), win_keys(qi, KIND_DIAG), state, tq)
        _flash_step(qa_ref, bias_ref, KIND_DIAG, values(vw_ref, qi), None, state, tq)
        gate = _sigmoid(gt_ref[...])
        for j in range(NSA_HPG):
            hs = slice(j * HEAD_DIM, (j + 1) * HEAD_DIM)
            rs = slice(j * tq, (j + 1) * tq)
            o_w = acc_ref[rs, 0:HEAD_DIM] / acc_ref[rs, HEAD_DIM:aug]
            o = (gate[:, j:j + 1] * oc_ref[:, hs]
                 + gate[:, NSA_HPG + j:NSA_HPG + j + 1] * os_ref[rs, :]
                 + gate[:, 2 * NSA_HPG + j:2 * NSA_HPG + j + 1] * o_w)
            o_ref[:, hs] = (o * _silu(bz_ref[:, hs])).astype(BF16)

    pl.when(qi == 0)(functools.partial(near_tiles, True))
    pl.when(qi > 0)(functools.partial(near_tiles, False))


def _selwin_attn(hb, ha, onehot, sel, bias_t, o_c, batch, seq, tq):
    nq = seq // tq
    rows = NSA_HPG * tq
    assert tq == WINDOW
    once = pl.Buffered(1)
    kv_spec = lambda base: pl.BlockSpec((seq, HEAD_DIM), lambda g, b, i: (b, base // HEAD_DIM + g))
    row_g = lambda g, b, i: (b * nq + i, g)
    return pl.pallas_call(
        functools.partial(_selwin_kernel, tq=tq),
        out_shape=jax.ShapeDtypeStruct((batch * seq, D_NSA), BF16),
        grid=(NSA_KV_GROUPS, batch, nq),
        in_specs=[pl.BlockSpec((tq, GROUP_W), lambda g, b, i: (b * nq + i, HB_Q // GROUP_W + g)),
                  kv_spec(HB_KS), kv_spec(HB_VS), kv_spec(HB_KW), kv_spec(HB_VW),
                  pl.BlockSpec((seq, LANE), lambda g, b, i: (0, 0), pipeline_mode=once),
                  pl.BlockSpec((1, 1, tq, LANE), lambda g, b, i: (b, g, i, 0)),
                  pl.BlockSpec((1, N_BIAS_KINDS, rows, tq), lambda g, b, i: (g, 0, 0, 0), pipeline_mode=once),
                  pl.BlockSpec((tq, GROUP_W), row_g),
                  pl.BlockSpec((tq, LANE), lambda g, b, i: (b * nq + i, HA_GT // LANE + g)),
                  pl.BlockSpec((tq, GROUP_W), lambda g, b, i: (b * nq + i, HA_BZ // GROUP_W + g))],
        out_specs=pl.BlockSpec((tq, GROUP_W), row_g),
        scratch_shapes=[pltpu.VMEM((rows, 2 * HEAD_DIM), BF16),
                        pltpu.VMEM((rows, tq), F32),
                        pltpu.VMEM((rows, tq), BF16),
                        pltpu.VMEM((rows, LANE), F32),
                        pltpu.VMEM((rows, LANE), F32),
                        pltpu.VMEM((rows, 2 * HEAD_DIM), F32),
                        pltpu.VMEM((rows, HEAD_DIM), F32)],
        compiler_params=_cparams("parallel", "parallel", "arbitrary"),
        name="selwin_attn",
    )(hb, hb, hb, hb, hb, onehot, sel, bias_t, o_c, ha, ha)


def _even_weights(w):
    w_g = w[:, EV_GT:EV_BZ].reshape(D_MODEL, N_BRANCH, NSA_KV_GROUPS, NSA_HPG)
    w_g = jnp.transpose(w_g, (0, 2, 1, 3)).reshape(D_MODEL, NSA_KV_GROUPS, N_BRANCH * NSA_HPG)
    w_g = jnp.pad(w_g, ((0, 0), (0, 0), (0, LANE - N_BRANCH * NSA_HPG))).reshape(D_MODEL, NSA_KV_GROUPS * LANE)
    w_tail = jnp.concatenate([w[:, EV_BZ:EV_END], w_g], axis=1).astype(BF16)
    return w.astype(BF16), w_tail


def _overlap_matrix(seq):
    rows = seq // CMP_STRIDE
    n_cmp = (seq - CMP_BLOCK) // CMP_STRIDE + 1
    n_sel = seq // SEL_BLOCK
    cstart = np.arange(rows)[None, :] * CMP_STRIDE
    sstart = np.arange(n_sel)[:, None] * SEL_BLOCK
    ov = (cstart < sstart + SEL_BLOCK) & (cstart + CMP_BLOCK > sstart) & (np.arange(rows)[None, :] < n_cmp)
    return jnp.asarray(ov.astype(np.float32), dtype=BF16)


def _block_onehot(seq):
    blk = np.arange(seq)[:, None] // SEL_BLOCK
    return jnp.asarray((blk == np.arange(LANE)[None, :]).astype(np.float32), dtype=BF16)


def kernel(x, rel_bias_table, ln_g, ln_b, ev_w_in, ev_conv_w, ev_cmp_pos, ev_cmp_w1, ev_cmp_w2, ev_w_out,
           od_w_in, od_ln_g, od_ln_b, od_sgu_w, od_sgu_b, od_w_out):
    batch, seq, d = x.shape
    depth = ln_g.shape[0]
    alpha = (2 * depth) ** 0.25
    m = batch * seq
    tq = ATT_TILE
    assert d == D_MODEL and seq % 1024 == 0 and ev_w_in.shape[-1] == EV_END

    xf = x.reshape(m, d)
    xb = None
    bias_c = _bias_cmp(rel_bias_table, seq, tb=1024)
    bias_t = _bias_tiles(rel_bias_table, tq)
    ov = _overlap_matrix(seq)
    onehot = _block_onehot(seq)

    for layer in range(depth):
        i = layer // 2
        g = ln_g[layer].reshape(1, d)
        b = ln_b[layer].reshape(1, d)
        if layer % 2 == 0:
            w_ev, w_tail = _even_weights(ev_w_in[i])
            hb, ha, *cast = _nsa_proj(xf if xb is None else xb, w_ev, w_tail, tm=512)
            xb = cast[0] if cast else xb
            y_a = _conv_proj(xb, w_ev, ev_conv_w[i], seq, tm=1024, tc=256)
            kcv = _compress(ha, ev_cmp_w1[i].astype(BF16), ev_cmp_w2[i].astype(BF16),
                            ev_cmp_pos[i].reshape(2, 1, CMP_BLOCK * HEAD_DIM).astype(BF16), batch, seq)
            o_c, sel = _cmp_attn(hb, kcv, bias_c, ov, batch, seq, CMP_TILE)
            y_b = _selwin_attn(hb, ha, onehot, sel, bias_t, o_c, batch, seq, tq)
            xf, xb = _outproj_ln(y_a, 0, y_b, 0, ev_w_out[i].astype(BF16), xf, g, b, alpha, 512, "outproj_even")
        else:
            y = _odd_mixer(xb, od_w_in[i].astype(BF16), od_ln_g[i].reshape(1, d), od_ln_b[i].reshape(1, d),
                           od_sgu_w[i], od_sgu_b[i].reshape(SGU_GROUPS, SGU_CHUNK, 1), tm=1024)
            xf, xb = _outproj_ln(y, 0, y, 1, od_w_out[i].astype(BF16), xf, g, b, alpha, 512, "outproj_odd")
    return xf.reshape(batch, seq, d)
```

```python
import functools
import math

import numpy as np
import jax
import jax.numpy as jnp
from jax import lax
from jax.experimental import pallas as pl
from jax.experimental.pallas import tpu as pltpu

F32 = jnp.float32
BF16 = jnp.bfloat16

D_MODEL = 2048
CONV_WIDTH = 3
D_CONV = 1024
NSA_HEADS = 8
NSA_KV_GROUPS = 2
NSA_HPG = NSA_HEADS // NSA_KV_GROUPS
HEAD_DIM = 128
D_NSA = NSA_HEADS * HEAD_DIM
D_KV = NSA_KV_GROUPS * HEAD_DIM
CMP_BLOCK = 32
CMP_STRIDE = 16
SEL_BLOCK = 64
N_SELECT = 16
WINDOW = 512
N_BRANCH = 3
D_SGU = D_MODEL
SGU_GROUPS = 8
SGU_CHUNK = 128
SGU_GROUP_DIM = D_SGU // SGU_GROUPS
REL_BUCKETS = 32
REL_MAX_DIST = 128
LN_EPS = 1e-5
NEG_INF = -1e30
FORCED_SCORE = 1e9
GROUP_W = NSA_HPG * HEAD_DIM
LOG2E = math.log2(math.e)
Q_SCALE = HEAD_DIM ** -0.5 * LOG2E

LANE = 128
SUBLANE = 8
VMEM_LIMIT = 56 * 1024 * 1024

EV_A = 0
EV_Q = 4 * D_CONV
EV_KC = EV_Q + D_NSA
EV_KS = EV_KC + 2 * D_KV
EV_GT = EV_KS + 4 * D_KV
EV_BZ = EV_GT + N_BRANCH * NSA_HEADS
EV_END = EV_BZ + D_NSA

HB_Q, HB_KS, HB_VS, HB_KW, HB_VW = 0, 1024, 1280, 1536, 1792
HA_KC, HA_BZ, HA_GT = 0, 512, 1536
HB_W = 2048
HA_W = HA_GT + NSA_KV_GROUPS * LANE
PROJ_CHUNK = 512

CMP_TILE = 512
ROW_SUB = 256
OUT_SUB = 128
ATT_TILE = 512
KIND_DIAG, KIND_SUB, KIND_CORNER = 0, 1, 2
N_BIAS_KINDS = 3


def _cparams(*sem):
    return pltpu.CompilerParams(dimension_semantics=sem, vmem_limit_bytes=VMEM_LIMIT)


def _sigmoid(x):
    return 1.0 / (1.0 + jnp.exp(-x))


def _silu(x):
    return x * _sigmoid(x)


def _gelu_tanh(x):
    c = math.sqrt(2.0 / math.pi)
    return x * (0.5 * (1.0 + jnp.tanh(c * (x + 0.044715 * (x * x * x)))))


def _dot_nt(a, b):
    return lax.dot_general(a, b, (((1,), (1,)), ((), ())), preferred_element_type=F32)


def _dot(a, b):
    return jnp.dot(a, b, preferred_element_type=F32)


def _layer_norm(z, g, b):
    mu = jnp.mean(z, axis=-1, keepdims=True)
    zc = z - mu
    var = jnp.mean(zc * zc, axis=-1, keepdims=True)
    return zc * lax.rsqrt(var + LN_EPS) * g + b


def _outproj_kernel(y1_ref, y2_ref, w1_ref, w2_ref, x_ref, g_ref, b_ref, o_ref, ob_ref, *, alpha):
    for r in range(x_ref.shape[0] // OUT_SUB):
        rs = slice(r * OUT_SUB, (r + 1) * OUT_SUB)
        y = _dot(y1_ref[rs, :], w1_ref[...]) + _dot(y2_ref[rs, :], w2_ref[...])
        out = _layer_norm(alpha * x_ref[rs, :] + y, g_ref[...], b_ref[...])
        o_ref[rs, :] = out
        ob_ref[rs, :] = out.astype(BF16)


def _outproj_ln(y1, y1_col, y2, y2_col, w_out, layer, x, g, b, alpha, tm, name):
    m, d = x.shape
    kh = w_out.shape[1] // 2
    return pl.pallas_call(
        functools.partial(_outproj_kernel, alpha=alpha),
        out_shape=(jax.ShapeDtypeStruct((m, d), F32), jax.ShapeDtypeStruct((m, d), BF16)),
        grid=(m // tm,),
        in_specs=[pl.BlockSpec((tm, kh), lambda i: (i, y1_col)),
                  pl.BlockSpec((tm, kh), lambda i: (i, y2_col)),
                  pl.BlockSpec((None, kh, d), lambda i: (layer, 0, 0)),
                  pl.BlockSpec((None, kh, d), lambda i: (layer, 1, 0)),
                  pl.BlockSpec((tm, d), lambda i: (i, 0)),
                  pl.BlockSpec((1, d), lambda i: (0, 0)),
                  pl.BlockSpec((1, d), lambda i: (0, 0))],
        out_specs=(pl.BlockSpec((tm, d), lambda i: (i, 0)),
                   pl.BlockSpec((tm, d), lambda i: (i, 0))),
        compiler_params=_cparams("parallel"),
        name=name,
    )(y1, y2, w_out, w_out, x, g, b)


def _odd_kernel(x_ref, wv_ref, wu_ref, wz_ref, g_ref, b_ref, sw_ref, sb_ref, o_ref, vs_ref, mu_ref, rstd_ref,
                *, tm):
    step = pl.program_id(1)
    nchunk = D_SGU // PROJ_CHUNK
    ngrp = PROJ_CHUNK // SGU_GROUP_DIM

    def project_v():
        for r in range(tm // ROW_SUB):
            rs = slice(r * ROW_SUB, (r + 1) * ROW_SUB)
            x = x_ref[rs, :]
            tot = None
            for c in range(nchunk):
                v = _gelu_tanh(_dot(x, wv_ref[:, c * PROJ_CHUNK:(c + 1) * PROJ_CHUNK]))
                vs_ref[c, rs, :] = v
                part = jnp.sum(v, axis=-1, keepdims=True)
                tot = part if tot is None else tot + part
            mu = tot * (1.0 / D_SGU)
            sq = None
            for c in range(nchunk):
                vc = vs_ref[c, rs, :] - mu
                part = jnp.sum(vc * vc, axis=-1, keepdims=True)
                sq = part if sq is None else sq + part
            mu_ref[rs, :] = mu
            rstd_ref[rs, :] = lax.rsqrt(sq * (1.0 / D_SGU) + LN_EPS)

    def mix_columns(chunk):
        row = lax.broadcasted_iota(jnp.int32, (SGU_CHUNK, SGU_CHUNK), 0)
        col = lax.broadcasted_iota(jnp.int32, (SGU_CHUNK, SGU_CHUNK), 1)
        wgs = [jnp.where(col <= row, sw_ref[g], 0.0).astype(BF16) for g in range(ngrp)]
        for r in range(tm // ROW_SUB):
            rs = slice(r * ROW_SUB, (r + 1) * ROW_SUB)
            x = x_ref[rs, :]
            u = _gelu_tanh(_dot(x, wu_ref[...]))
            z = _dot(x, wz_ref[...])
            vn = ((vs_ref[chunk, rs, :] - mu_ref[rs, :]) * rstd_ref[rs, :] * g_ref[...] + b_ref[...]).astype(BF16)
            for g in range(ngrp):
                cs = slice(g * SGU_GROUP_DIM, (g + 1) * SGU_GROUP_DIM)
                for c in range(ROW_SUB // SGU_CHUNK):
                    ls = slice(c * SGU_CHUNK, (c + 1) * SGU_CHUNK)
                    os_ = slice(r * ROW_SUB + c * SGU_CHUNK, r * ROW_SUB + (c + 1) * SGU_CHUNK)
                    mixed = _dot(wgs[g], vn[ls, cs]) + sb_ref[g]
                    o_ref[os_, cs] = (u[ls, cs] * mixed * _silu(z[ls, cs])).astype(BF16)

    @pl.when(step == 0)
    def _():
        project_v()
        mix_columns(0)

    @pl.when(step > 0)
    def _():
        mix_columns(step)


def _odd_mixer(xb, w_in, layer, ln_g, ln_b, sgu_w, sgu_b, tm):
    m = xb.shape[0]
    cw = PROJ_CHUNK
    nstep = D_SGU // cw
    gps = cw // SGU_GROUP_DIM
    return pl.pallas_call(
        functools.partial(_odd_kernel, tm=tm),
        out_shape=jax.ShapeDtypeStruct((m, D_SGU), BF16),
        grid=(m // tm, nstep),
        in_specs=[pl.BlockSpec((tm, D_MODEL), lambda i, s: (i, 0)),
                  pl.BlockSpec((None, D_MODEL, D_SGU), lambda i, s: (layer, 0, 1)),
                  pl.BlockSpec((None, D_MODEL, cw), lambda i, s: (layer, 0, s)),
                  pl.BlockSpec((None, D_MODEL, cw), lambda i, s: (layer, 0, 2 * nstep + s)),
                  pl.BlockSpec((1, cw), lambda i, s: (0, s)),
                  pl.BlockSpec((1, cw), lambda i, s: (0, s)),
                  pl.BlockSpec((gps, SGU_CHUNK, SGU_CHUNK), lambda i, s: (s, 0, 0)),
                  pl.BlockSpec((gps, SGU_CHUNK, 1), lambda i, s: (s, 0, 0))],
        out_specs=pl.BlockSpec((tm, cw), lambda i, s: (i, s)),
        scratch_shapes=[pltpu.VMEM((nstep, tm, cw), F32),
                        pltpu.VMEM((tm, 1), F32),
                        pltpu.VMEM((tm, 1), F32)],
        compiler_params=_cparams("parallel", "arbitrary"),
        name="odd_mixer",
    )(xb, w_in, w_in, w_in, ln_g, ln_b, sgu_w, sgu_b)


def _conv_proj_kernel(x_ref, wh_ref, wb_ref, wc_ref, wz_ref, cw_ref, o_ref, u_ref, *, tm, tiles_per_seq):
    i = pl.program_id(1)
    nsub = tm // ROW_SUB

    @pl.when(i % tiles_per_seq == 0)
    def _():
        u_ref[0:SUBLANE, :] = jnp.zeros((SUBLANE, u_ref.shape[1]), F32)

    for r in range(nsub):
        rs = slice(r * ROW_SUB, (r + 1) * ROW_SUB)
        x = x_ref[rs, :]
        u_ref[SUBLANE + r * ROW_SUB:SUBLANE + (r + 1) * ROW_SUB, :] = _dot(x, wc_ref[...]) * _dot(x, wh_ref[...])
    for r in range(nsub):
        rs = slice(r * ROW_SUB, (r + 1) * ROW_SUB)
        x = x_ref[rs, :]
        conv = cw_ref[CONV_WIDTH - 1:CONV_WIDTH, :] * u_ref[SUBLANE + r * ROW_SUB:SUBLANE + (r + 1) * ROW_SUB, :]
        for k in range(CONV_WIDTH - 1):
            lo = SUBLANE + r * ROW_SUB - (CONV_WIDTH - 1 - k)
            conv = conv + cw_ref[k:k + 1, :] * u_ref[lo:lo + ROW_SUB, :]
        o_ref[rs, :] = (_dot(x, wb_ref[...]) * conv * _silu(_dot(x, wz_ref[...]))).astype(BF16)
    u_ref[0:SUBLANE, :] = u_ref[tm:tm + SUBLANE, :]


def _conv_proj(xb, w, layer, conv_w, seq, tm, tc):
    m = xb.shape[0]
    nct = D_CONV // tc

    def wspec(part):
        return pl.BlockSpec((None, D_MODEL, tc), lambda j, i: (layer, 0, (EV_A + part * D_CONV) // tc + j))

    return pl.pallas_call(
        functools.partial(_conv_proj_kernel, tm=tm, tiles_per_seq=seq // tm),
        out_shape=jax.ShapeDtypeStruct((m, D_CONV), BF16),
        grid=(nct, m // tm),
        in_specs=[pl.BlockSpec((tm, D_MODEL), lambda j, i: (i, 0)),
                  wspec(0), wspec(1), wspec(2), wspec(3),
                  pl.BlockSpec((CONV_WIDTH, tc), lambda j, i: (0, j))],
        out_specs=pl.BlockSpec((tm, tc), lambda j, i: (i, j)),
        scratch_shapes=[pltpu.VMEM((tm + SUBLANE, tc), F32)],
        compiler_params=_cparams("parallel", "arbitrary"),
        name="conv_proj",
    )(xb, w, w, w, w, conv_w)


def _nsa_proj_kernel(x_ref, wq_ref, wkc_ref, wks_ref, wkw_ref, wt_ref, hb_ref, ha_ref, *xb_ref):
    x = x_ref[...].astype(BF16)
    if xb_ref:
        xb_ref[0][...] = x
    cw = PROJ_CHUNK
    for c in range(D_NSA // cw):
        hb_ref[:, c * cw:(c + 1) * cw] = (_dot(x, wq_ref[:, c * cw:(c + 1) * cw]) * Q_SCALE).astype(BF16)
    hb_ref[:, HB_KS:HB_KS + cw] = _dot(x, wks_ref[...]).astype(BF16)
    hb_ref[:, HB_KW:HB_KW + cw] = _dot(x, wkw_ref[...]).astype(BF16)
    ha_ref[:, HA_KC:HA_KC + cw] = _dot(x, wkc_ref[...])
    for lo in range(0, HA_W - HA_BZ, cw):
        hi = min(lo + cw, HA_W - HA_BZ)
        ha_ref[:, HA_BZ + lo:HA_BZ + hi] = _dot(x, wt_ref[:, lo:hi])


def _nsa_proj(x, w, layer, w_tail, tm):
    m = x.shape[0]
    once = pl.Buffered(1)
    cw = PROJ_CHUNK
    assert EV_Q % D_NSA == 0 and EV_KC % cw == 0 and EV_KS % cw == 0 and 2 * D_KV == cw
    out_shape = [jax.ShapeDtypeStruct((m, HB_W), BF16), jax.ShapeDtypeStruct((m, HA_W), F32)]
    out_specs = [pl.BlockSpec((tm, HB_W), lambda i: (i, 0)), pl.BlockSpec((tm, HA_W), lambda i: (i, 0))]
    if x.dtype != BF16:
        out_shape.append(jax.ShapeDtypeStruct((m, D_MODEL), BF16))
        out_specs.append(pl.BlockSpec((tm, D_MODEL), lambda i: (i, 0)))
    return pl.pallas_call(
        _nsa_proj_kernel,
        out_shape=tuple(out_shape),
        grid=(m // tm,),
        in_specs=[pl.BlockSpec((tm, D_MODEL), lambda i: (i, 0)),
                  pl.BlockSpec((None, D_MODEL, D_NSA), lambda i: (layer, 0, EV_Q // D_NSA), pipeline_mode=once),
                  pl.BlockSpec((None, D_MODEL, cw), lambda i: (layer, 0, EV_KC // cw), pipeline_mode=once),
                  pl.BlockSpec((None, D_MODEL, cw), lambda i: (layer, 0, EV_KS // cw), pipeline_mode=once),
                  pl.BlockSpec((None, D_MODEL, cw), lambda i: (layer, 0, EV_KS // cw + 1), pipeline_mode=once),
                  pl.BlockSpec((D_MODEL, HA_W - HA_BZ), lambda i: (0, 0), pipeline_mode=once)],
        out_specs=tuple(out_specs),
        compiler_params=_cparams("parallel"),
        name="nsa_proj",
    )(x, w, w, w, w, w_tail)


def _compress_kernel(tok_ref, w1_ref, w2_ref, pos_ref, o_ref, b_ref, *, rows):
    half = CMP_STRIDE * HEAD_DIM
    x2 = jnp.concatenate(
        [tok_ref[pl.ds(l, rows, stride=CMP_STRIDE), :] for l in range(CMP_STRIDE)], axis=1).astype(BF16)
    lo = _dot(x2, w1_ref[0, 0:half, :])
    hi = _dot(x2, w1_ref[0, half:2 * half, :])
    b_ref[0:rows, :] = hi
    b_ref[rows:rows + SUBLANE, :] = jnp.zeros((SUBLANE, HEAD_DIM), F32)
    posb = _dot(jnp.broadcast_to(pos_ref[0], (SUBLANE, 2 * half)), w1_ref[0])[0:1, :]
    pre = lo + b_ref[1:rows + 1, :] + posb
    o_ref[0, 0] = _dot(_silu(pre).astype(BF16), w2_ref[0]).astype(BF16)


def _compress(ha, w1, w2, pos, batch, seq):
    rows = seq // CMP_STRIDE
    nkv = 2 * NSA_KV_GROUPS
    kc_block = HA_KC // HEAD_DIM
    return pl.pallas_call(
        functools.partial(_compress_kernel, rows=rows),
        out_shape=jax.ShapeDtypeStruct((batch, nkv, rows, HEAD_DIM), BF16),
        grid=(batch, nkv),
        in_specs=[pl.BlockSpec((seq, HEAD_DIM), lambda b, c: (b, kc_block + c)),
                  pl.BlockSpec((1, CMP_BLOCK * HEAD_DIM, HEAD_DIM), lambda b, c: (c // NSA_KV_GROUPS, 0, 0)),
                  pl.BlockSpec((1, HEAD_DIM, HEAD_DIM), lambda b, c: (c // NSA_KV_GROUPS, 0, 0)),
                  pl.BlockSpec((1, 1, CMP_BLOCK * HEAD_DIM), lambda b, c: (c // NSA_KV_GROUPS, 0, 0))],
        out_specs=pl.BlockSpec((1, 1, rows, HEAD_DIM), lambda b, c: (b, c, 0, 0)),
        scratch_shapes=[pltpu.VMEM((rows + SUBLANE, HEAD_DIM), F32)],
        compiler_params=_cparams("parallel", "arbitrary"),
        name="cmp_blocks",
    )(ha, w1, w2, pos)


def _t5_bucket(dist):
    n = jnp.maximum(dist, 0)
    max_exact = REL_BUCKETS // 2
    large = max_exact + (jnp.log(jnp.maximum(n, 1).astype(F32) / max_exact)
                         / math.log(REL_MAX_DIST / max_exact) * (REL_BUCKETS - max_exact)).astype(jnp.int32)
    large = jnp.minimum(large, REL_BUCKETS - 1)
    return jnp.where(n < max_exact, n, large)


def _table_lookup(dist, tab_ref, head):
    bkt = _t5_bucket(dist)
    acc = jnp.zeros(dist.shape, F32)
    for b in range(REL_BUCKETS):
        acc = jnp.where(bkt == b, tab_ref[b, head], acc)
    return acc


def _rel_bias(dist, valid, tab_ref, head):
    rows, cols = dist.shape
    if cols % LANE or REL_MAX_DIST > LANE:
        return jnp.where(valid, _table_lookup(dist, tab_ref, head), NEG_INF)
    lane_dist = lax.broadcasted_iota(jnp.int32, (SUBLANE, LANE), 1)
    near = jnp.broadcast_to(_table_lookup(lane_dist, tab_ref, head)[0:1, :], (rows, LANE))
    far = _table_lookup(jnp.full((SUBLANE, LANE), REL_MAX_DIST, jnp.int32), tab_ref, head)[0:1, 0:1]
    parts = []
    for c in range(cols // LANE):
        d = dist[:, c * LANE:(c + 1) * LANE]
        g = jnp.take_along_axis(near, jnp.clip(d, 0, LANE - 1), axis=1)
        parts.append(jnp.where(d >= REL_MAX_DIST, far, g))
    return jnp.where(valid, jnp.concatenate(parts, axis=1), NEG_INF)


def _bias_cmp_kernel(tab_ref, o_ref, *, tb, rows, n_cmp):
    head = pl.program_id(0)
    t = pl.program_id(1) * tb + lax.broadcasted_iota(jnp.int32, (tb, rows), 0)
    n = lax.broadcasted_iota(jnp.int32, (tb, rows), 1)
    dist = t - (n * CMP_STRIDE + CMP_BLOCK - 1)
    o_ref[0] = _rel_bias(dist, (dist >= 0) & (n < n_cmp), tab_ref, head) * LOG2E


def _bias_cmp(table, seq, tb):
    rows = seq // CMP_STRIDE
    n_cmp = (seq - CMP_BLOCK) // CMP_STRIDE + 1
    return pl.pallas_call(
        functools.partial(_bias_cmp_kernel, tb=tb, rows=rows, n_cmp=n_cmp),
        out_shape=jax.ShapeDtypeStruct((NSA_HEADS, seq, rows), F32),
        grid=(NSA_HEADS, seq // tb),
        in_specs=[pl.BlockSpec(memory_space=pltpu.SMEM)],
        out_specs=pl.BlockSpec((1, tb, rows), lambda h, i: (h, i, 0)),
        compiler_params=_cparams("parallel", "arbitrary"),
        name="bias_cmp",
    )(table)


def _bias_tiles_kernel(tab_ref, o_ref, *, tq):
    head = pl.program_id(0)
    kind = pl.program_id(1)
    ij = (lax.broadcasted_iota(jnp.int32, (tq, tq), 0) - lax.broadcasted_iota(jnp.int32, (tq, tq), 1))
    dist = jnp.where(kind == KIND_DIAG, ij, jnp.where(kind == KIND_SUB, tq + ij, WINDOW + ij))
    lo = jnp.where(kind == KIND_DIAG, 0, -tq)
    hi = jnp.where(kind == KIND_CORNER, 0, tq)
    far_dist = jnp.full((SUBLANE, LANE), tq + 1, jnp.int32)
    far = _rel_bias(far_dist, far_dist > 0, tab_ref, head)[0:1, 0:1]
    o_ref[0, 0] = (_rel_bias(dist, (ij >= lo) & (ij < hi), tab_ref, head) - far) * LOG2E


def _bias_tiles(table, tq):
    assert WINDOW % tq == 0 and tq + 1 >= REL_MAX_DIST
    return pl.pallas_call(
        functools.partial(_bias_tiles_kernel, tq=tq),
        out_shape=jax.ShapeDtypeStruct((NSA_KV_GROUPS, N_BIAS_KINDS, NSA_HPG * tq, tq), F32),
        grid=(NSA_HEADS, N_BIAS_KINDS),
        in_specs=[pl.BlockSpec(memory_space=pltpu.SMEM)],
        out_specs=pl.BlockSpec((1, 1, tq, tq), lambda h, k: (h // NSA_HPG, k, h % NSA_HPG, 0)),
        compiler_params=_cparams("parallel", "arbitrary"),
        name="bias_tiles",
    )(table)


def _cmp_attn_kernel(q_ref, kc_ref, vc_ref, bias_ref, ov_ref, oc_ref, sel_ref, *, tq, n_sel, n_top):
    kc = kc_ref[0, 0]
    vc = vc_ref[0, 0]
    psum = None
    for j in range(NSA_HPG):
        hs = slice(j * HEAD_DIM, (j + 1) * HEAD_DIM)
        bias = bias_ref[j]
        s = _dot_nt(q_ref[:, hs], kc) + bias
        m = jnp.max(s, axis=-1, keepdims=True)
        e = jnp.exp2(s - m)
        p = e / jnp.sum(e, axis=-1, keepdims=True)
        p = jnp.where(bias > 0.5 * NEG_INF, p, 0.0)
        oc_ref[:, hs] = _dot(p.astype(BF16), vc)
        psum = p if psum is None else psum + p
    ov = ov_ref[...]
    imp = None
    rem = psum
    for _ in range(3):
        piece = rem.astype(BF16)
        part = _dot_nt(ov, piece)
        imp = part if imp is None else imp + part
        rem = rem - piece.astype(F32)
    t = pl.program_id(1) * tq + lax.broadcasted_iota(jnp.int32, (n_sel, tq), 1)
    cur = jnp.right_shift(t, int(math.log2(SEL_BLOCK)))
    blk = lax.broadcasted_iota(jnp.int32, (n_sel, tq), 0)
    forced = (blk == 0) | (blk == cur) | (blk == cur - 1)
    imp = jnp.where(blk > cur, -1.0, jnp.where(forced, FORCED_SCORE, imp))
    groups = [imp[g * SUBLANE:(g + 1) * SUBLANE, :] for g in range(n_sel // SUBLANE)]
    ranks = [jnp.zeros((SUBLANE, tq), jnp.int32) for _ in groups]
    sub = lax.broadcasted_iota(jnp.int32, (SUBLANE, tq), 0)
    for i in range(n_sel):
        row = imp[i:i + 1, :]
        for g, x in enumerate(groups):
            if g * SUBLANE > i:
                ahead = row >= x
            elif (g + 1) * SUBLANE - 1 < i:
                ahead = row > x
            else:
                ahead = (row > x) | ((row == x) & (sub > i - g * SUBLANE))
            ranks[g] = ranks[g] + ahead.astype(jnp.int32)
    sel_t = jnp.where(jnp.concatenate(ranks, axis=0) < n_top, 0.0, NEG_INF)
    if n_sel < LANE:
        sel_t = jnp.concatenate([sel_t, jnp.zeros((LANE - n_sel, tq), F32)], axis=0)
    sel_ref[0, 0] = sel_t.T.astype(BF16)


def _cmp_attn(hb, kcv, bias_c, ov, batch, seq, tq):
    rows = seq // CMP_STRIDE
    n_sel = seq // SEL_BLOCK
    n_top = min(N_SELECT, n_sel)
    assert n_sel <= LANE and n_sel % SUBLANE == 0
    nq = seq // tq
    g_n = NSA_KV_GROUPS
    return pl.pallas_call(
        functools.partial(_cmp_attn_kernel, tq=tq, n_sel=n_sel, n_top=n_top),
        out_shape=(jax.ShapeDtypeStruct((batch * seq, D_NSA), F32),
                   jax.ShapeDtypeStruct((batch, g_n, seq, LANE), BF16)),
        grid=(g_n, nq, batch),
        in_specs=[pl.BlockSpec((tq, GROUP_W), lambda g, i, b: (b * nq + i, HB_Q // GROUP_W + g)),
                  pl.BlockSpec((1, 1, rows, HEAD_DIM), lambda g, i, b: (b, g, 0, 0)),
                  pl.BlockSpec((1, 1, rows, HEAD_DIM), lambda g, i, b: (b, g_n + g, 0, 0)),
                  pl.BlockSpec((NSA_HPG, tq, rows), lambda g, i, b: (g, i, 0)),
                  pl.BlockSpec((n_sel, rows), lambda g, i, b: (0, 0))],
        out_specs=(pl.BlockSpec((tq, GROUP_W), lambda g, i, b: (b * nq + i, g)),
                   pl.BlockSpec((1, 1, tq, LANE), lambda g, i, b: (b, g, i, 0))),
        compiler_params=_cparams("parallel", "parallel", "arbitrary"),
        name="cmp_attn",
    )(hb, kcv, kcv, bias_c, ov)


FLASH_ROWS = 128


def _flash_init(state):
    _, _, m_ref, _, acc_ref = state
    m_ref[...] = jnp.full(m_ref.shape, -3e38, F32)
    acc_ref[...] = jnp.zeros(acc_ref.shape, F32)


def _flash_scores(qa_ref, kdim, k, s_ref, h, tq):
    hr = slice(h * tq, (h + 1) * tq)
    s_ref[hr, :] = _dot_nt(qa_ref[hr, 0:kdim], k)


def _flash_step(qa_ref, bias_ref, kind, v, nxt, state, tq):
    s_ref, p_ref, m_ref, a_ref, acc_ref = state
    tk = s_ref.shape[1]
    for h in range(NSA_HPG):
        hr = slice(h * tq, (h + 1) * tq)
        for r in range(tq // FLASH_ROWS):
            rs = slice(h * tq + r * FLASH_ROWS, h * tq + (r + 1) * FLASH_ROWS)
            s = s_ref[rs, :]
            if kind is not None:
                s = s + bias_ref[0, kind, rs, :]
            m_old = m_ref[rs, :]
            m_new = jnp.maximum(m_old, jnp.max(s, axis=-1, keepdims=True))
            p_ref[rs, :] = jnp.exp2(s - jnp.tile(m_new, (1, tk // LANE))).astype(BF16)
            a_ref[rs, :] = jnp.exp2(m_old - m_new)
            m_ref[rs, :] = m_new
        if nxt is not None:
            _flash_scores(qa_ref, nxt[0], nxt[1], s_ref, h, tq)
        acc_ref[hr, :] = jnp.tile(a_ref[hr, :], (1, 2)) * acc_ref[hr, :] + _dot(p_ref[hr, :], v)


def _selwin_kernel(q_ref, ks_ref, vs_ref, kw_ref, vw_ref, oh_ref, sel_ref, bias_ref, oc_ref, gt_ref, bz_ref,
                   o_ref, qa_ref, s_ref, p_ref, m_ref, a_ref, acc_ref, os_ref, *, tq):
    qi = pl.program_id(2)
    state = (s_ref, p_ref, m_ref, a_ref, acc_ref)
    aug = 2 * HEAD_DIM
    ones = jnp.ones((tq, HEAD_DIM), BF16)
    for j in range(NSA_HPG):
        qa_ref[j * tq:(j + 1) * tq, 0:HEAD_DIM] = q_ref[:, j * HEAD_DIM:(j + 1) * HEAD_DIM]
        qa_ref[j * tq:(j + 1) * tq, HEAD_DIM:aug] = sel_ref[0, 0]

    def rows_of(kt):
        return pl.ds(pl.multiple_of(kt * tq, tq), tq)

    def sel_keys(kt):
        return aug, jnp.concatenate([ks_ref[rows_of(kt), :], oh_ref[rows_of(kt), :]], axis=1)

    def win_keys(kt):
        return HEAD_DIM, kw_ref[rows_of(kt), :]

    def values(v_ref, kt):
        return jnp.concatenate([v_ref[rows_of(kt), :], ones], axis=1)

    _flash_init(state)
    for h in range(NSA_HPG):
        _flash_scores(qa_ref, *sel_keys(0), s_ref, h, tq)

    def far_body(kt, carry):
        _flash_step(qa_ref, bias_ref, None, values(vs_ref, kt), sel_keys(kt + 1), state, tq)
        return carry

    lax.fori_loop(0, jnp.maximum(qi - 1, 0), far_body, 0)

    def near_tiles(first):
        if not first:
            _flash_step(qa_ref, bias_ref, KIND_SUB, values(vs_ref, qi - 1), sel_keys(qi), state, tq)
        _flash_step(qa_ref, bias_ref, KIND_DIAG, values(vs_ref, qi), win_keys(jnp.maximum(qi - 1, 0)), state, tq)
        os_ref[...] = acc_ref[:, 0:HEAD_DIM] / acc_ref[:, HEAD_DIM:aug]
        _flash_init(state)
        if not first:
            _flash_step(qa_ref, bias_ref, KIND_CORNER, values(vw_ref, qi - 1), win_keys(qi), state, tq)
        _flash_step(qa_ref, bias_ref, KIND_DIAG, values(vw_ref, qi), None, state, tq)
        gate = _sigmoid(gt_ref[...])
        for j in range(NSA_HPG):
            hs = slice(j * HEAD_DIM, (j + 1) * HEAD_DIM)
            rs = slice(j * tq, (j + 1) * tq)
            o_w = acc_ref[rs, 0:HEAD_DIM] / acc_ref[rs, HEAD_DIM:aug]
            o = (gate[:, j:j + 1] * oc_ref[:, hs]
                 + gate[:, NSA_HPG + j:NSA_HPG + j + 1] * os_ref[rs, :]
                 + gate[:, 2 * NSA_HPG + j:2 * NSA_HPG + j + 1] * o_w)
            o_ref[:, hs] = (o * _silu(bz_ref[:, hs])).astype(BF16)

    pl.when(qi == 0)(functools.partial(near_tiles, True))
    pl.when(qi > 0)(functools.partial(near_tiles, False))


def _selwin_attn(hb, ha, onehot, sel, bias_t, o_c, batch, seq, tq):
    nq = seq // tq
    rows = NSA_HPG * tq
    assert tq == WINDOW
    once = pl.Buffered(1)
    kv_spec = lambda base: pl.BlockSpec((seq, HEAD_DIM), lambda g, b, i: (b, base // HEAD_DIM + g))
    row_g = lambda g, b, i: (b * nq + i, g)
    return pl.pallas_call(
        functools.partial(_selwin_kernel, tq=tq),
        out_shape=jax.ShapeDtypeStruct((batch * seq, D_NSA), BF16),
        grid=(NSA_KV_GROUPS, batch, nq),
        in_specs=[pl.BlockSpec((tq, GROUP_W), lambda g, b, i: (b * nq + i, HB_Q // GROUP_W + g)),
                  kv_spec(HB_KS), kv_spec(HB_VS), kv_spec(HB_KW), kv_spec(HB_VW),
                  pl.BlockSpec((seq, LANE), lambda g, b, i: (0, 0), pipeline_mode=once),
                  pl.BlockSpec((1, 1, tq, LANE), lambda g, b, i: (b, g, i, 0)),
                  pl.BlockSpec((1, N_BIAS_KINDS, rows, tq), lambda g, b, i: (g, 0, 0, 0), pipeline_mode=once),
                  pl.BlockSpec((tq, GROUP_W), row_g),
                  pl.BlockSpec((tq, LANE), lambda g, b, i: (b * nq + i, HA_GT // LANE + g)),
                  pl.BlockSpec((tq, GROUP_W), lambda g, b, i: (b * nq + i, HA_BZ // GROUP_W + g))],
        out_specs=pl.BlockSpec((tq, GROUP_W), row_g),
        scratch_shapes=[pltpu.VMEM((rows, 2 * HEAD_DIM), BF16),
                        pltpu.VMEM((rows, tq), F32),
                        pltpu.VMEM((rows, tq), BF16),
                        pltpu.VMEM((rows, LANE), F32),
                        pltpu.VMEM((rows, LANE), F32),
                        pltpu.VMEM((rows, 2 * HEAD_DIM), F32),
                        pltpu.VMEM((rows, HEAD_DIM), F32)],
        compiler_params=_cparams("parallel", "parallel", "arbitrary"),
        name="selwin_attn",
    )(hb, hb, hb, hb, hb, onehot, sel, bias_t, o_c, ha, ha)


def _even_tail_weights(w):
    w_g = w[:, EV_GT:EV_BZ].reshape(D_MODEL, N_BRANCH, NSA_KV_GROUPS, NSA_HPG)
    w_g = jnp.transpose(w_g, (0, 2, 1, 3)).reshape(D_MODEL, NSA_KV_GROUPS, N_BRANCH * NSA_HPG)
    w_g = jnp.pad(w_g, ((0, 0), (0, 0), (0, LANE - N_BRANCH * NSA_HPG))).reshape(D_MODEL, NSA_KV_GROUPS * LANE)
    return jnp.concatenate([w[:, EV_BZ:EV_END], w_g], axis=1).astype(BF16)


def _overlap_matrix(seq):
    rows = seq // CMP_STRIDE
    n_cmp = (seq - CMP_BLOCK) // CMP_STRIDE + 1
    n_sel = seq // SEL_BLOCK
    cstart = np.arange(rows)[None, :] * CMP_STRIDE
    sstart = np.arange(n_sel)[:, None] * SEL_BLOCK
    ov = (cstart < sstart + SEL_BLOCK) & (cstart + CMP_BLOCK > sstart) & (np.arange(rows)[None, :] < n_cmp)
    return jnp.asarray(ov.astype(np.float32), dtype=BF16)


def _block_onehot(seq):
    blk = np.arange(seq)[:, None] // SEL_BLOCK
    return jnp.asarray((blk == np.arange(LANE)[None, :]).astype(np.float32), dtype=BF16)


def kernel(x, rel_bias_table, ln_g, ln_b, ev_w_in, ev_conv_w, ev_cmp_pos, ev_cmp_w1, ev_cmp_w2, ev_w_out,
           od_w_in, od_ln_g, od_ln_b, od_sgu_w, od_sgu_b, od_w_out):
    batch, seq, d = x.shape
    depth = ln_g.shape[0]
    alpha = (2 * depth) ** 0.25
    m = batch * seq
    tq = ATT_TILE
    assert d == D_MODEL and seq % 1024 == 0 and ev_w_in.shape[-1] == EV_END

    xf = x.reshape(m, d)
    xb = None
    bias_c = _bias_cmp(rel_bias_table, seq, tb=1024)
    bias_t = _bias_tiles(rel_bias_table, tq)
    ov = _overlap_matrix(seq)
    onehot = _block_onehot(seq)
    ev_w, ev_wo = ev_w_in.astype(BF16), ev_w_out.astype(BF16)
    od_w, od_wo = od_w_in.astype(BF16), od_w_out.astype(BF16)

    for layer in range(depth):
        i = layer // 2
        g = ln_g[layer].reshape(1, d)
        b = ln_b[layer].reshape(1, d)
        if layer % 2 == 0:
            w_tail = _even_tail_weights(ev_w_in[i])
            hb, ha, *cast = _nsa_proj(xf if xb is None else xb, ev_w, i, w_tail, tm=512)
            xb = cast[0] if cast else xb
            y_a = _conv_proj(xb, ev_w, i, ev_conv_w[i], seq, tm=1024, tc=256)
            kcv = _compress(ha, ev_cmp_w1[i].astype(BF16), ev_cmp_w2[i].astype(BF16),
                            ev_cmp_pos[i].reshape(2, 1, CMP_BLOCK * HEAD_DIM).astype(BF16), batch, seq)
            o_c, sel = _cmp_attn(hb, kcv, bias_c, ov, batch, seq, CMP_TILE)
            y_b = _selwin_attn(hb, ha, onehot, sel, bias_t, o_c, batch, seq, tq)
            xf, xb = _outproj_ln(y_a, 0, y_b, 0, ev_wo, i, xf, g, b, alpha, 512, "outproj_even")
        else:
            y = _odd_mixer(xb, od_w, i, od_ln_g[i].reshape(1, d), od_ln_b[i].reshape(1, d),
                           od_sgu_w[i], od_sgu_b[i].reshape(SGU_GROUPS, SGU_CHUNK, 1), tm=1024)
            xf, xb = _outproj_ln(y, 0, y, 1, od_wo, i, xf, g, b, alpha, 512, "outproj_odd")
    return xf.reshape(batch, seq, d)
```

```python
import functools
import math

import numpy as np
import jax
import jax.numpy as jnp
from jax import lax
from jax.experimental import pallas as pl
from jax.experimental.pallas import tpu as pltpu

F32 = jnp.float32
BF16 = jnp.bfloat16

D_MODEL = 2048
CONV_WIDTH = 3
D_CONV = 1024
NSA_HEADS = 8
NSA_KV_GROUPS = 2
NSA_HPG = NSA_HEADS // NSA_KV_GROUPS
HEAD_DIM = 128
D_NSA = NSA_HEADS * HEAD_DIM
D_KV = NSA_KV_GROUPS * HEAD_DIM
CMP_BLOCK = 32
CMP_STRIDE = 16
SEL_BLOCK = 64
N_SELECT = 16
WINDOW = 512
N_BRANCH = 3
D_SGU = D_MODEL
SGU_GROUPS = 8
SGU_CHUNK = 128
SGU_GROUP_DIM = D_SGU // SGU_GROUPS
REL_BUCKETS = 32
REL_MAX_DIST = 128
LN_EPS = 1e-5
NEG_INF = -1e30
FORCED_SCORE = 1e9
GROUP_W = NSA_HPG * HEAD_DIM
LOG2E = math.log2(math.e)
Q_SCALE = HEAD_DIM ** -0.5 * LOG2E

LANE = 128
SUBLANE = 8
VMEM_LIMIT = 56 * 1024 * 1024

EV_A = 0
EV_Q = 4 * D_CONV
EV_KC = EV_Q + D_NSA
EV_KS = EV_KC + 2 * D_KV
EV_GT = EV_KS + 4 * D_KV
EV_BZ = EV_GT + N_BRANCH * NSA_HEADS
EV_END = EV_BZ + D_NSA

HB_Q, HB_KS, HB_VS, HB_KW, HB_VW = 0, 1024, 1280, 1536, 1792
HA_KC, HA_BZ, HA_GT = 0, 512, 1536
HB_W = 2048
HA_W = HA_GT + NSA_KV_GROUPS * LANE
PROJ_CHUNK = 512

CMP_TILE = 512
ROW_SUB = 256
OUT_SUB = 128
ATT_TILE = 512
KIND_DIAG, KIND_SUB, KIND_CORNER = 0, 1, 2
N_BIAS_KINDS = 3


def _cparams(*sem):
    return pltpu.CompilerParams(dimension_semantics=sem, vmem_limit_bytes=VMEM_LIMIT)


def _sigmoid(x):
    return 1.0 / (1.0 + jnp.exp(-x))


def _silu(x):
    return x * _sigmoid(x)


def _gelu_tanh(x):
    c = math.sqrt(2.0 / math.pi)
    return x * (0.5 * (1.0 + jnp.tanh(c * (x + 0.044715 * (x * x * x)))))


def _dot_nt(a, b):
    return lax.dot_general(a, b, (((1,), (1,)), ((), ())), preferred_element_type=F32)


def _dot(a, b):
    return jnp.dot(a, b, preferred_element_type=F32)


def _layer_norm(z, g, b):
    mu = jnp.mean(z, axis=-1, keepdims=True)
    zc = z - mu
    var = jnp.mean(zc * zc, axis=-1, keepdims=True)
    return zc * lax.rsqrt(var + LN_EPS) * g + b


def _outproj_kernel(y1_ref, y2_ref, w1_ref, w2_ref, x_ref, g_ref, b_ref, o_ref, ob_ref, *, alpha):
    for r in range(x_ref.shape[0] // OUT_SUB):
        rs = slice(r * OUT_SUB, (r + 1) * OUT_SUB)
        y = _dot(y1_ref[rs, :], w1_ref[...]) + _dot(y2_ref[rs, :], w2_ref[...])
        out = _layer_norm(alpha * x_ref[rs, :] + y, g_ref[...], b_ref[...])
        o_ref[rs, :] = out
        ob_ref[rs, :] = out.astype(BF16)


def _outproj_ln(y1, y1_col, y2, y2_col, w_out, layer, x, g, b, alpha, tm, name):
    m, d = x.shape
    kh = w_out.shape[1] // 2
    return pl.pallas_call(
        functools.partial(_outproj_kernel, alpha=alpha),
        out_shape=(jax.ShapeDtypeStruct((m, d), F32), jax.ShapeDtypeStruct((m, d), BF16)),
        grid=(m // tm,),
        in_specs=[pl.BlockSpec((tm, kh), lambda i: (i, y1_col)),
                  pl.BlockSpec((tm, kh), lambda i: (i, y2_col)),
                  pl.BlockSpec((None, kh, d), lambda i: (layer, 0, 0)),
                  pl.BlockSpec((None, kh, d), lambda i: (layer, 1, 0)),
                  pl.BlockSpec((tm, d), lambda i: (i, 0)),
                  pl.BlockSpec((1, d), lambda i: (0, 0)),
                  pl.BlockSpec((1, d), lambda i: (0, 0))],
        out_specs=(pl.BlockSpec((tm, d), lambda i: (i, 0)),
                   pl.BlockSpec((tm, d), lambda i: (i, 0))),
        compiler_params=_cparams("parallel"),
        name=name,
    )(y1, y2, w_out, w_out, x, g, b)


def _odd_kernel(x_ref, wv_ref, wu_ref, wz_ref, g_ref, b_ref, sw_ref, sb_ref, o_ref, vs_ref, mu_ref, rstd_ref,
                *, tm):
    step = pl.program_id(1)
    nchunk = D_SGU // PROJ_CHUNK
    ngrp = PROJ_CHUNK // SGU_GROUP_DIM

    def project_v():
        for r in range(tm // ROW_SUB):
            rs = slice(r * ROW_SUB, (r + 1) * ROW_SUB)
            x = x_ref[rs, :]
            tot = None
            for c in range(nchunk):
                v = _gelu_tanh(_dot(x, wv_ref[c]))
                vs_ref[c, rs, :] = v
                part = jnp.sum(v, axis=-1, keepdims=True)
                tot = part if tot is None else tot + part
            mu = tot * (1.0 / D_SGU)
            sq = None
            for c in range(nchunk):
                vc = vs_ref[c, rs, :] - mu
                part = jnp.sum(vc * vc, axis=-1, keepdims=True)
                sq = part if sq is None else sq + part
            mu_ref[rs, :] = mu
            rstd_ref[rs, :] = lax.rsqrt(sq * (1.0 / D_SGU) + LN_EPS)

    def mix_columns(chunk):
        row = lax.broadcasted_iota(jnp.int32, (SGU_CHUNK, SGU_CHUNK), 0)
        col = lax.broadcasted_iota(jnp.int32, (SGU_CHUNK, SGU_CHUNK), 1)
        wgs = [jnp.where(col <= row, sw_ref[g], 0.0).astype(BF16) for g in range(ngrp)]
        for r in range(tm // ROW_SUB):
            rs = slice(r * ROW_SUB, (r + 1) * ROW_SUB)
            x = x_ref[rs, :]
            u = _gelu_tanh(_dot(x, wu_ref[...]))
            z = _dot(x, wz_ref[...])
            vn = ((vs_ref[chunk, rs, :] - mu_ref[rs, :]) * rstd_ref[rs, :] * g_ref[...] + b_ref[...]).astype(BF16)
            for g in range(ngrp):
                cs = slice(g * SGU_GROUP_DIM, (g + 1) * SGU_GROUP_DIM)
                for c in range(ROW_SUB // SGU_CHUNK):
                    ls = slice(c * SGU_CHUNK, (c + 1) * SGU_CHUNK)
                    os_ = slice(r * ROW_SUB + c * SGU_CHUNK, r * ROW_SUB + (c + 1) * SGU_CHUNK)
                    mixed = _dot(wgs[g], vn[ls, cs]) + sb_ref[g]
                    o_ref[os_, cs] = (u[ls, cs] * mixed * _silu(z[ls, cs])).astype(BF16)

    @pl.when(step == 0)
    def _():
        project_v()
        mix_columns(0)

    @pl.when(step > 0)
    def _():
        mix_columns(step)


def _odd_mixer(xb, w_in, layer, ln_g, ln_b, sgu_w, sgu_b, tm):
    m = xb.shape[0]
    cw = PROJ_CHUNK
    nstep = D_SGU // cw
    gps = cw // SGU_GROUP_DIM
    return pl.pallas_call(
        functools.partial(_odd_kernel, tm=tm),
        out_shape=jax.ShapeDtypeStruct((m, D_SGU), BF16),
        grid=(m // tm, nstep),
        in_specs=[pl.BlockSpec((tm, D_MODEL), lambda i, s: (i, 0)),
                  pl.BlockSpec((None, nstep, D_MODEL, cw), lambda i, s: (layer, 1, 0, 0)),
                  pl.BlockSpec((None, None, D_MODEL, cw), lambda i, s: (layer, s, 0, 0)),
                  pl.BlockSpec((None, None, D_MODEL, cw), lambda i, s: (layer, 2 * nstep + s, 0, 0)),
                  pl.BlockSpec((1, cw), lambda i, s: (0, s)),
                  pl.BlockSpec((1, cw), lambda i, s: (0, s)),
                  pl.BlockSpec((gps, SGU_CHUNK, SGU_CHUNK), lambda i, s: (s, 0, 0)),
                  pl.BlockSpec((gps, SGU_CHUNK, 1), lambda i, s: (s, 0, 0))],
        out_specs=pl.BlockSpec((tm, cw), lambda i, s: (i, s)),
        scratch_shapes=[pltpu.VMEM((nstep, tm, cw), F32),
                        pltpu.VMEM((tm, 1), F32),
                        pltpu.VMEM((tm, 1), F32)],
        compiler_params=_cparams("parallel", "arbitrary"),
        name="odd_mixer",
    )(xb, w_in, w_in, w_in, ln_g, ln_b, sgu_w, sgu_b)


def _conv_proj_kernel(x_ref, wh_ref, wb_ref, wc_ref, wz_ref, cw_ref, o_ref, u_ref, *, tm, tiles_per_seq):
    i = pl.program_id(1)
    nsub = tm // ROW_SUB

    @pl.when(i % tiles_per_seq == 0)
    def _():
        u_ref[0:SUBLANE, :] = jnp.zeros((SUBLANE, u_ref.shape[1]), F32)

    for r in range(nsub):
        rs = slice(r * ROW_SUB, (r + 1) * ROW_SUB)
        x = x_ref[rs, :]
        u_ref[SUBLANE + r * ROW_SUB:SUBLANE + (r + 1) * ROW_SUB, :] = _dot(x, wc_ref[...]) * _dot(x, wh_ref[...])
    for r in range(nsub):
        rs = slice(r * ROW_SUB, (r + 1) * ROW_SUB)
        x = x_ref[rs, :]
        conv = cw_ref[CONV_WIDTH - 1:CONV_WIDTH, :] * u_ref[SUBLANE + r * ROW_SUB:SUBLANE + (r + 1) * ROW_SUB, :]
        for k in range(CONV_WIDTH - 1):
            lo = SUBLANE + r * ROW_SUB - (CONV_WIDTH - 1 - k)
            conv = conv + cw_ref[k:k + 1, :] * u_ref[lo:lo + ROW_SUB, :]
        o_ref[rs, :] = (_dot(x, wb_ref[...]) * conv * _silu(_dot(x, wz_ref[...]))).astype(BF16)
    u_ref[0:SUBLANE, :] = u_ref[tm:tm + SUBLANE, :]


def _conv_proj(xb, w, layer, conv_w, seq, tm, tc):
    m = xb.shape[0]
    nct = D_CONV // tc

    def wspec(part):
        return pl.BlockSpec((None, D_MODEL, tc), lambda j, i: (layer, 0, (EV_A + part * D_CONV) // tc + j))

    return pl.pallas_call(
        functools.partial(_conv_proj_kernel, tm=tm, tiles_per_seq=seq // tm),
        out_shape=jax.ShapeDtypeStruct((m, D_CONV), BF16),
        grid=(nct, m // tm),
        in_specs=[pl.BlockSpec((tm, D_MODEL), lambda j, i: (i, 0)),
                  wspec(0), wspec(1), wspec(2), wspec(3),
                  pl.BlockSpec((CONV_WIDTH, tc), lambda j, i: (0, j))],
        out_specs=pl.BlockSpec((tm, tc), lambda j, i: (i, j)),
        scratch_shapes=[pltpu.VMEM((tm + SUBLANE, tc), F32)],
        compiler_params=_cparams("parallel", "arbitrary"),
        name="conv_proj",
    )(xb, w, w, w, w, conv_w)


def _nsa_proj_kernel(x_ref, wq_ref, wkc_ref, wks_ref, wkw_ref, wt_ref, hb_ref, ha_ref, *xb_ref):
    x = x_ref[...].astype(BF16)
    if xb_ref:
        xb_ref[0][...] = x
    cw = PROJ_CHUNK
    for c in range(D_NSA // cw):
        hb_ref[:, c * cw:(c + 1) * cw] = (_dot(x, wq_ref[:, c * cw:(c + 1) * cw]) * Q_SCALE).astype(BF16)
    hb_ref[:, HB_KS:HB_KS + cw] = _dot(x, wks_ref[...]).astype(BF16)
    hb_ref[:, HB_KW:HB_KW + cw] = _dot(x, wkw_ref[...]).astype(BF16)
    ha_ref[:, HA_KC:HA_KC + cw] = _dot(x, wkc_ref[...])
    for lo in range(0, HA_W - HA_BZ, cw):
        hi = min(lo + cw, HA_W - HA_BZ)
        ha_ref[:, HA_BZ + lo:HA_BZ + hi] = _dot(x, wt_ref[:, lo:hi])


def _nsa_proj(x, w, layer, w_tail, tm):
    m = x.shape[0]
    once = pl.Buffered(1)
    cw = PROJ_CHUNK
    assert EV_Q % D_NSA == 0 and EV_KC % cw == 0 and EV_KS % cw == 0 and 2 * D_KV == cw
    out_shape = [jax.ShapeDtypeStruct((m, HB_W), BF16), jax.ShapeDtypeStruct((m, HA_W), F32)]
    out_specs = [pl.BlockSpec((tm, HB_W), lambda i: (i, 0)), pl.BlockSpec((tm, HA_W), lambda i: (i, 0))]
    if x.dtype != BF16:
        out_shape.append(jax.ShapeDtypeStruct((m, D_MODEL), BF16))
        out_specs.append(pl.BlockSpec((tm, D_MODEL), lambda i: (i, 0)))
    return pl.pallas_call(
        _nsa_proj_kernel,
        out_shape=tuple(out_shape),
        grid=(m // tm,),
        in_specs=[pl.BlockSpec((tm, D_MODEL), lambda i: (i, 0)),
                  pl.BlockSpec((None, D_MODEL, D_NSA), lambda i: (layer, 0, EV_Q // D_NSA), pipeline_mode=once),
                  pl.BlockSpec((None, D_MODEL, cw), lambda i: (layer, 0, EV_KC // cw), pipeline_mode=once),
                  pl.BlockSpec((None, D_MODEL, cw), lambda i: (layer, 0, EV_KS // cw), pipeline_mode=once),
                  pl.BlockSpec((None, D_MODEL, cw), lambda i: (layer, 0, EV_KS // cw + 1), pipeline_mode=once),
                  pl.BlockSpec((D_MODEL, HA_W - HA_BZ), lambda i: (0, 0), pipeline_mode=once)],
        out_specs=tuple(out_specs),
        compiler_params=_cparams("parallel"),
        name="nsa_proj",
    )(x, w, w, w, w, w_tail)


def _compress_kernel(tok_ref, w1_ref, w2_ref, pos_ref, o_ref, b_ref, *, rows):
    half = CMP_STRIDE * HEAD_DIM
    x2 = jnp.concatenate(
        [tok_ref[pl.ds(l, rows, stride=CMP_STRIDE), :] for l in range(CMP_STRIDE)], axis=1).astype(BF16)
    lo = _dot(x2, w1_ref[0, 0:half, :])
    hi = _dot(x2, w1_ref[0, half:2 * half, :])
    b_ref[0:rows, :] = hi
    b_ref[rows:rows + SUBLANE, :] = jnp.zeros((SUBLANE, HEAD_DIM), F32)
    posb = _dot(jnp.broadcast_to(pos_ref[0], (SUBLANE, 2 * half)), w1_ref[0])[0:1, :]
    pre = lo + b_ref[1:rows + 1, :] + posb
    o_ref[0, 0] = _dot(_silu(pre).astype(BF16), w2_ref[0]).astype(BF16)


def _compress(ha, w1, w2, pos, batch, seq):
    rows = seq // CMP_STRIDE
    nkv = 2 * NSA_KV_GROUPS
    kc_block = HA_KC // HEAD_DIM
    return pl.pallas_call(
        functools.partial(_compress_kernel, rows=rows),
        out_shape=jax.ShapeDtypeStruct((batch, nkv, rows, HEAD_DIM), BF16),
        grid=(batch, nkv),
        in_specs=[pl.BlockSpec((seq, HEAD_DIM), lambda b, c: (b, kc_block + c)),
                  pl.BlockSpec((1, CMP_BLOCK * HEAD_DIM, HEAD_DIM), lambda b, c: (c // NSA_KV_GROUPS, 0, 0)),
                  pl.BlockSpec((1, HEAD_DIM, HEAD_DIM), lambda b, c: (c // NSA_KV_GROUPS, 0, 0)),
                  pl.BlockSpec((1, 1, CMP_BLOCK * HEAD_DIM), lambda b, c: (c // NSA_KV_GROUPS, 0, 0))],
        out_specs=pl.BlockSpec((1, 1, rows, HEAD_DIM), lambda b, c: (b, c, 0, 0)),
        scratch_shapes=[pltpu.VMEM((rows + SUBLANE, HEAD_DIM), F32)],
        compiler_params=_cparams("parallel", "arbitrary"),
        name="cmp_blocks",
    )(ha, w1, w2, pos)


def _t5_bucket(dist):
    n = jnp.maximum(dist, 0)
    max_exact = REL_BUCKETS // 2
    large = max_exact + (jnp.log(jnp.maximum(n, 1).astype(F32) / max_exact)
                         / math.log(REL_MAX_DIST / max_exact) * (REL_BUCKETS - max_exact)).astype(jnp.int32)
    large = jnp.minimum(large, REL_BUCKETS - 1)
    return jnp.where(n < max_exact, n, large)


def _table_lookup(dist, tab_ref, head):
    bkt = _t5_bucket(dist)
    acc = jnp.zeros(dist.shape, F32)
    for b in range(REL_BUCKETS):
        acc = jnp.where(bkt == b, tab_ref[b, head], acc)
    return acc


def _rel_bias(dist, valid, tab_ref, head):
    rows, cols = dist.shape
    if cols % LANE or REL_MAX_DIST > LANE:
        return jnp.where(valid, _table_lookup(dist, tab_ref, head), NEG_INF)
    lane_dist = lax.broadcasted_iota(jnp.int32, (SUBLANE, LANE), 1)
    near = jnp.broadcast_to(_table_lookup(lane_dist, tab_ref, head)[0:1, :], (rows, LANE))
    far = _table_lookup(jnp.full((SUBLANE, LANE), REL_MAX_DIST, jnp.int32), tab_ref, head)[0:1, 0:1]
    parts = []
    for c in range(cols // LANE):
        d = dist[:, c * LANE:(c + 1) * LANE]
        g = jnp.take_along_axis(near, jnp.clip(d, 0, LANE - 1), axis=1)
        parts.append(jnp.where(d >= REL_MAX_DIST, far, g))
    return jnp.where(valid, jnp.concatenate(parts, axis=1), NEG_INF)


def _bias_cmp_kernel(tab_ref, o_ref, *, tb, rows, n_cmp):
    head = pl.program_id(0)
    t = pl.program_id(1) * tb + lax.broadcasted_iota(jnp.int32, (tb, rows), 0)
    n = lax.broadcasted_iota(jnp.int32, (tb, rows), 1)
    dist = t - (n * CMP_STRIDE + CMP_BLOCK - 1)
    o_ref[0] = _rel_bias(dist, (dist >= 0) & (n < n_cmp), tab_ref, head) * LOG2E


def _bias_cmp(table, seq, tb):
    rows = seq // CMP_STRIDE
    n_cmp = (seq - CMP_BLOCK) // CMP_STRIDE + 1
    return pl.pallas_call(
        functools.partial(_bias_cmp_kernel, tb=tb, rows=rows, n_cmp=n_cmp),
        out_shape=jax.ShapeDtypeStruct((NSA_HEADS, seq, rows), F32),
        grid=(NSA_HEADS, seq // tb),
        in_specs=[pl.BlockSpec(memory_space=pltpu.SMEM)],
        out_specs=pl.BlockSpec((1, tb, rows), lambda h, i: (h, i, 0)),
        compiler_params=_cparams("parallel", "arbitrary"),
        name="bias_cmp",
    )(table)


def _bias_tiles_kernel(tab_ref, o_ref, *, tq):
    head = pl.program_id(0)
    kind = pl.program_id(1)
    ij = (lax.broadcasted_iota(jnp.int32, (tq, tq), 0) - lax.broadcasted_iota(jnp.int32, (tq, tq), 1))
    dist = jnp.where(kind == KIND_DIAG, ij, jnp.where(kind == KIND_SUB, tq + ij, WINDOW + ij))
    lo = jnp.where(kind == KIND_DIAG, 0, -tq)
    hi = jnp.where(kind == KIND_CORNER, 0, tq)
    far_dist = jnp.full((SUBLANE, LANE), tq + 1, jnp.int32)
    far = _rel_bias(far_dist, far_dist > 0, tab_ref, head)[0:1, 0:1]
    o_ref[0, 0] = (_rel_bias(dist, (ij >= lo) & (ij < hi), tab_ref, head) - far) * LOG2E


def _bias_tiles(table, tq):
    assert WINDOW % tq == 0 and tq + 1 >= REL_MAX_DIST
    return pl.pallas_call(
        functools.partial(_bias_tiles_kernel, tq=tq),
        out_shape=jax.ShapeDtypeStruct((NSA_KV_GROUPS, N_BIAS_KINDS, NSA_HPG * tq, tq), F32),
        grid=(NSA_HEADS, N_BIAS_KINDS),
        in_specs=[pl.BlockSpec(memory_space=pltpu.SMEM)],
        out_specs=pl.BlockSpec((1, 1, tq, tq), lambda h, k: (h // NSA_HPG, k, h % NSA_HPG, 0)),
        compiler_params=_cparams("parallel", "arbitrary"),
        name="bias_tiles",
    )(table)


def _cmp_attn_kernel(q_ref, kc_ref, vc_ref, bias_ref, ov_ref, oc_ref, sel_ref, *, tq, n_sel, n_top):
    kc = kc_ref[0, 0]
    vc = vc_ref[0, 0]
    psum = None
    for j in range(NSA_HPG):
        hs = slice(j * HEAD_DIM, (j + 1) * HEAD_DIM)
        bias = bias_ref[j]
        s = _dot_nt(q_ref[:, hs], kc) + bias
        m = jnp.max(s, axis=-1, keepdims=True)
        e = jnp.exp2(s - m)
        p = e / jnp.sum(e, axis=-1, keepdims=True)
        p = jnp.where(bias > 0.5 * NEG_INF, p, 0.0)
        oc_ref[:, hs] = _dot(p.astype(BF16), vc)
        psum = p if psum is None else psum + p
    ov = ov_ref[...]
    imp = None
    rem = psum
    for _ in range(3):
        piece = rem.astype(BF16)
        part = _dot_nt(ov, piece)
        imp = part if imp is None else imp + part
        rem = rem - piece.astype(F32)
    t = pl.program_id(1) * tq + lax.broadcasted_iota(jnp.int32, (n_sel, tq), 1)
    cur = jnp.right_shift(t, int(math.log2(SEL_BLOCK)))
    blk = lax.broadcasted_iota(jnp.int32, (n_sel, tq), 0)
    forced = (blk == 0) | (blk == cur) | (blk == cur - 1)
    imp = jnp.where(blk > cur, -1.0, jnp.where(forced, FORCED_SCORE, imp))
    groups = [imp[g * SUBLANE:(g + 1) * SUBLANE, :] for g in range(n_sel // SUBLANE)]
    ranks = [jnp.zeros((SUBLANE, tq), jnp.int32) for _ in groups]
    sub = lax.broadcasted_iota(jnp.int32, (SUBLANE, tq), 0)
    for i in range(n_sel):
        row = imp[i:i + 1, :]
        for g, x in enumerate(groups):
            if g * SUBLANE > i:
                ahead = row >= x
            elif (g + 1) * SUBLANE - 1 < i:
                ahead = row > x
            else:
                ahead = (row > x) | ((row == x) & (sub > i - g * SUBLANE))
            ranks[g] = ranks[g] + ahead.astype(jnp.int32)
    sel_t = jnp.where(jnp.concatenate(ranks, axis=0) < n_top, 0.0, NEG_INF)
    if n_sel < LANE:
        sel_t = jnp.concatenate([sel_t, jnp.zeros((LANE - n_sel, tq), F32)], axis=0)
    sel_ref[0, 0] = sel_t.T.astype(BF16)


def _cmp_attn(hb, kcv, bias_c, ov, batch, seq, tq):
    rows = seq // CMP_STRIDE
    n_sel = seq // SEL_BLOCK
    n_top = min(N_SELECT, n_sel)
    assert n_sel <= LANE and n_sel % SUBLANE == 0
    nq = seq // tq
    g_n = NSA_KV_GROUPS
    return pl.pallas_call(
        functools.partial(_cmp_attn_kernel, tq=tq, n_sel=n_sel, n_top=n_top),
        out_shape=(jax.ShapeDtypeStruct((batch * seq, D_NSA), F32),
                   jax.ShapeDtypeStruct((batch, g_n, seq, LANE), BF16)),
        grid=(g_n, nq, batch),
        in_specs=[pl.BlockSpec((tq, GROUP_W), lambda g, i, b: (b * nq + i, HB_Q // GROUP_W + g)),
                  pl.BlockSpec((1, 1, rows, HEAD_DIM), lambda g, i, b: (b, g, 0, 0)),
                  pl.BlockSpec((1, 1, rows, HEAD_DIM), lambda g, i, b: (b, g_n + g, 0, 0)),
                  pl.BlockSpec((NSA_HPG, tq, rows), lambda g, i, b: (g, i, 0)),
                  pl.BlockSpec((n_sel, rows), lambda g, i, b: (0, 0))],
        out_specs=(pl.BlockSpec((tq, GROUP_W), lambda g, i, b: (b * nq + i, g)),
                   pl.BlockSpec((1, 1, tq, LANE), lambda g, i, b: (b, g, i, 0))),
        compiler_params=_cparams("parallel", "parallel", "arbitrary"),
        name="cmp_attn",
    )(hb, kcv, kcv, bias_c, ov)


FLASH_ROWS = 128


def _flash_init(state):
    _, _, m_ref, _, acc_ref = state
    m_ref[...] = jnp.full(m_ref.shape, -3e38, F32)
    acc_ref[...] = jnp.zeros(acc_ref.shape, F32)


def _flash_scores(qa_ref, kdim, k, s_ref, h, tq):
    hr = slice(h * tq, (h + 1) * tq)
    s_ref[hr, :] = _dot_nt(qa_ref[hr, 0:kdim], k)


def _flash_step(qa_ref, bias_ref, kind, v, nxt, state, tq):
    s_ref, p_ref, m_ref, a_ref, acc_ref = state
    tk = s_ref.shape[1]
    for h in range(NSA_HPG):
        hr = slice(h * tq, (h + 1) * tq)
        for r in range(tq // FLASH_ROWS):
            rs = slice(h * tq + r * FLASH_ROWS, h * tq + (r + 1) * FLASH_ROWS)
            s = s_ref[rs, :]
            if kind is not None:
                s = s + bias_ref[0, kind, rs, :]
            m_old = m_ref[rs, :]
            m_new = jnp.maximum(m_old, jnp.max(s, axis=-1, keepdims=True))
            p_ref[rs, :] = jnp.exp2(s - jnp.tile(m_new, (1, tk // LANE))).astype(BF16)
            a_ref[rs, :] = jnp.exp2(m_old - m_new)
            m_ref[rs, :] = m_new
        if nxt is not None:
            _flash_scores(qa_ref, nxt[0], nxt[1], s_ref, h, tq)
        acc_ref[hr, :] = jnp.tile(a_ref[hr, :], (1, 2)) * acc_ref[hr, :] + _dot(p_ref[hr, :], v)


def _selwin_kernel(q_ref, ks_ref, vs_ref, kw_ref, vw_ref, oh_ref, sel_ref, bias_ref, oc_ref, gt_ref, bz_ref,
                   o_ref, qa_ref, s_ref, p_ref, m_ref, a_ref, acc_ref, os_ref, *, tq):
    qi = pl.program_id(2)
    state = (s_ref, p_ref, m_ref, a_ref, acc_ref)
    aug = 2 * HEAD_DIM
    ones = jnp.ones((tq, HEAD_DIM), BF16)
    for j in range(NSA_HPG):
        qa_ref[j * tq:(j + 1) * tq, 0:HEAD_DIM] = q_ref[:, j * HEAD_DIM:(j + 1) * HEAD_DIM]
        qa_ref[j * tq:(j + 1) * tq, HEAD_DIM:aug] = sel_ref[0, 0]

    def rows_of(kt):
        return pl.ds(pl.multiple_of(kt * tq, tq), tq)

    def sel_keys(kt):
        return aug, jnp.concatenate([ks_ref[rows_of(kt), :], oh_ref[rows_of(kt), :]], axis=1)

    def win_keys(kt):
        return HEAD_DIM, kw_ref[rows_of(kt), :]

    def values(v_ref, kt):
        return jnp.concatenate([v_ref[rows_of(kt), :], ones], axis=1)

    _flash_init(state)
    for h in range(NSA_HPG):
        _flash_scores(qa_ref, *sel_keys(0), s_ref, h, tq)

    def far_body(kt, carry):
        _flash_step(qa_ref, bias_ref, None, values(vs_ref, kt), sel_keys(kt + 1), state, tq)
        return carry

    lax.fori_loop(0, jnp.maximum(qi - 1, 0), far_body, 0)

    def near_tiles(first):
        if not first:
            _flash_step(qa_ref, bias_ref, KIND_SUB, values(vs_ref, qi - 1), sel_keys(qi), state, tq)
        _flash_step(qa_ref, bias_ref, KIND_DIAG, values(vs_ref, qi), win_keys(jnp.maximum(qi - 1, 0)), state, tq)
        os_ref[...] = acc_ref[:, 0:HEAD_DIM] / acc_ref[:, HEAD_DIM:aug]
        _flash_init(state)
        if not first:
            _flash_step(qa_ref, bias_ref, KIND_CORNER, values(vw_ref, qi - 1), win_keys(qi), state, tq)
        _flash_step(qa_ref, bias_ref, KIND_DIAG, values(vw_ref, qi), None, state, tq)
        gate = _sigmoid(gt_ref[...])
        for j in range(NSA_HPG):
            hs = slice(j * HEAD_DIM, (j + 1) * HEAD_DIM)
            rs = slice(j * tq, (j + 1) * tq)
            o_w = acc_ref[rs, 0:HEAD_DIM] / acc_ref[rs, HEAD_DIM:aug]
            o = (gate[:, j:j + 1] * oc_ref[:, hs]
                 + gate[:, NSA_HPG + j:NSA_HPG + j + 1] * os_ref[rs, :]
                 + gate[:, 2 * NSA_HPG + j:2 * NSA_HPG + j + 1] * o_w)
            o_ref[:, hs] = (o * _silu(bz_ref[:, hs])).astype(BF16)

    pl.when(qi == 0)(functools.partial(near_tiles, True))
    pl.when(qi > 0)(functools.partial(near_tiles, False))


def _selwin_attn(hb, ha, onehot, sel, bias_t, o_c, batch, seq, tq):
    nq = seq // tq
    rows = NSA_HPG * tq
    assert tq == WINDOW
    once = pl.Buffered(1)
    kv_spec = lambda base: pl.BlockSpec((seq, HEAD_DIM), lambda g, b, i: (b, base // HEAD_DIM + g))
    row_g = lambda g, b, i: (b * nq + i, g)
    return pl.pallas_call(
        functools.partial(_selwin_kernel, tq=tq),
        out_shape=jax.ShapeDtypeStruct((batch * seq, D_NSA), BF16),
        grid=(NSA_KV_GROUPS, batch, nq),
        in_specs=[pl.BlockSpec((tq, GROUP_W), lambda g, b, i: (b * nq + i, HB_Q // GROUP_W + g)),
                  kv_spec(HB_KS), kv_spec(HB_VS), kv_spec(HB_KW), kv_spec(HB_VW),
                  pl.BlockSpec((seq, LANE), lambda g, b, i: (0, 0), pipeline_mode=once),
                  pl.BlockSpec((1, 1, tq, LANE), lambda g, b, i: (b, g, i, 0)),
                  pl.BlockSpec((1, N_BIAS_KINDS, rows, tq), lambda g, b, i: (g, 0, 0, 0), pipeline_mode=once),
                  pl.BlockSpec((tq, GROUP_W), row_g),
                  pl.BlockSpec((tq, LANE), lambda g, b, i: (b * nq + i, HA_GT // LANE + g)),
                  pl.BlockSpec((tq, GROUP_W), lambda g, b, i: (b * nq + i, HA_BZ // GROUP_W + g))],
        out_specs=pl.BlockSpec((tq, GROUP_W), row_g),
        scratch_shapes=[pltpu.VMEM((rows, 2 * HEAD_DIM), BF16),
                        pltpu.VMEM((rows, tq), F32),
                        pltpu.VMEM((rows, tq), BF16),
                        pltpu.VMEM((rows, LANE), F32),
                        pltpu.VMEM((rows, LANE), F32),
                        pltpu.VMEM((rows, 2 * HEAD_DIM), F32),
                        pltpu.VMEM((rows, HEAD_DIM), F32)],
        compiler_params=_cparams("parallel", "parallel", "arbitrary"),
        name="selwin_attn",
    )(hb, hb, hb, hb, hb, onehot, sel, bias_t, o_c, ha, ha)


def _even_tail_weights(w):
    w_g = w[:, EV_GT:EV_BZ].reshape(D_MODEL, N_BRANCH, NSA_KV_GROUPS, NSA_HPG)
    w_g = jnp.transpose(w_g, (0, 2, 1, 3)).reshape(D_MODEL, NSA_KV_GROUPS, N_BRANCH * NSA_HPG)
    w_g = jnp.pad(w_g, ((0, 0), (0, 0), (0, LANE - N_BRANCH * NSA_HPG))).reshape(D_MODEL, NSA_KV_GROUPS * LANE)
    return jnp.concatenate([w[:, EV_BZ:EV_END], w_g], axis=1).astype(BF16)


def _overlap_matrix(seq):
    rows = seq // CMP_STRIDE
    n_cmp = (seq - CMP_BLOCK) // CMP_STRIDE + 1
    n_sel = seq // SEL_BLOCK
    cstart = np.arange(rows)[None, :] * CMP_STRIDE
    sstart = np.arange(n_sel)[:, None] * SEL_BLOCK
    ov = (cstart < sstart + SEL_BLOCK) & (cstart + CMP_BLOCK > sstart) & (np.arange(rows)[None, :] < n_cmp)
    return jnp.asarray(ov.astype(np.float32), dtype=BF16)


def _block_onehot(seq):
    blk = np.arange(seq)[:, None] // SEL_BLOCK
    return jnp.asarray((blk == np.arange(LANE)[None, :]).astype(np.float32), dtype=BF16)


def kernel(x, rel_bias_table, ln_g, ln_b, ev_w_in, ev_conv_w, ev_cmp_pos, ev_cmp_w1, ev_cmp_w2, ev_w_out,
           od_w_in, od_ln_g, od_ln_b, od_sgu_w, od_sgu_b, od_w_out):
    batch, seq, d = x.shape
    depth = ln_g.shape[0]
    alpha = (2 * depth) ** 0.25
    m = batch * seq
    tq = ATT_TILE
    assert d == D_MODEL and seq % 1024 == 0 and ev_w_in.shape[-1] == EV_END

    xf = x.reshape(m, d)
    xb = None
    bias_c = _bias_cmp(rel_bias_table, seq, tb=1024)
    bias_t = _bias_tiles(rel_bias_table, tq)
    ov = _overlap_matrix(seq)
    onehot = _block_onehot(seq)
    ev_w, ev_wo = ev_w_in.astype(BF16), ev_w_out.astype(BF16)
    od_wo = od_w_out.astype(BF16)
    od_w = od_w_in.astype(BF16).reshape(od_w_in.shape[0], D_MODEL, -1, PROJ_CHUNK).transpose(0, 2, 1, 3)

    for layer in range(depth):
        i = layer // 2
        g = ln_g[layer].reshape(1, d)
        b = ln_b[layer].reshape(1, d)
        if layer % 2 == 0:
            w_tail = _even_tail_weights(ev_w_in[i])
            hb, ha, *cast = _nsa_proj(xf if xb is None else xb, ev_w, i, w_tail, tm=512)
            xb = cast[0] if cast else xb
            y_a = _conv_proj(xb, ev_w, i, ev_conv_w[i], seq, tm=1024, tc=256)
            kcv = _compress(ha, ev_cmp_w1[i].astype(BF16), ev_cmp_w2[i].astype(BF16),
                            ev_cmp_pos[i].reshape(2, 1, CMP_BLOCK * HEAD_DIM).astype(BF16), batch, seq)
            o_c, sel = _cmp_attn(hb, kcv, bias_c, ov, batch, seq, CMP_TILE)
            y_b = _selwin_attn(hb, ha, onehot, sel, bias_t, o_c, batch, seq, tq)
            xf, xb = _outproj_ln(y_a, 0, y_b, 0, ev_wo, i, xf, g, b, alpha, 512, "outproj_even")
        else:
            y = _odd_mixer(xb, od_w, i, od_ln_g[i].reshape(1, d), od_ln_b[i].reshape(1, d),
                           od_sgu_w[i], od_sgu_b[i].reshape(SGU_GROUPS, SGU_CHUNK, 1), tm=1024)
            xf, xb = _outproj_ln(y, 0, y, 1, od_wo, i, xf, g, b, alpha, 512, "outproj_odd")
    return xf.reshape(batch, seq, d)
```

```python
import functools
import math

import numpy as np
import jax
import jax.numpy as jnp
from jax import lax
from jax.experimental import pallas as pl
from jax.experimental.pallas import tpu as pltpu

F32 = jnp.float32
BF16 = jnp.bfloat16

D_MODEL = 2048
CONV_WIDTH = 3
D_CONV = 1024
NSA_HEADS = 8
NSA_KV_GROUPS = 2
NSA_HPG = NSA_HEADS // NSA_KV_GROUPS
HEAD_DIM = 128
D_NSA = NSA_HEADS * HEAD_DIM
D_KV = NSA_KV_GROUPS * HEAD_DIM
CMP_BLOCK = 32
CMP_STRIDE = 16
SEL_BLOCK = 64
N_SELECT = 16
WINDOW = 512
N_BRANCH = 3
D_SGU = D_MODEL
SGU_GROUPS = 8
SGU_CHUNK = 128
SGU_GROUP_DIM = D_SGU // SGU_GROUPS
REL_BUCKETS = 32
REL_MAX_DIST = 128
LN_EPS = 1e-5
NEG_INF = -1e30
FORCED_SCORE = 1e9
GROUP_W = NSA_HPG * HEAD_DIM
LOG2E = math.log2(math.e)
Q_SCALE = HEAD_DIM ** -0.5 * LOG2E

LANE = 128
SUBLANE = 8
VMEM_LIMIT = 56 * 1024 * 1024

EV_A = 0
EV_Q = 4 * D_CONV
EV_KC = EV_Q + D_NSA
EV_KS = EV_KC + 2 * D_KV
EV_GT = EV_KS + 4 * D_KV
EV_BZ = EV_GT + N_BRANCH * NSA_HEADS
EV_END = EV_BZ + D_NSA

HB_Q, HB_KS, HB_VS, HB_KW, HB_VW = 0, 1024, 1280, 1536, 1792
HA_KC, HA_BZ, HA_GT = 0, 512, 1536
HB_W = 2048
HA_W = HA_GT + NSA_KV_GROUPS * LANE
PROJ_CHUNK = 512

CMP_TILE = 512
ROW_SUB = 256
OUT_SUB = 128
ATT_TILE = 512
KIND_DIAG, KIND_SUB, KIND_CORNER = 0, 1, 2
N_BIAS_KINDS = 3


def _cparams(*sem):
    return pltpu.CompilerParams(dimension_semantics=sem, vmem_limit_bytes=VMEM_LIMIT)


def _sigmoid(x):
    return 1.0 / (1.0 + jnp.exp(-x))


def _silu(x):
    return x * _sigmoid(x)


def _gelu_tanh(x):
    c = math.sqrt(2.0 / math.pi)
    return x * (0.5 * (1.0 + jnp.tanh(c * (x + 0.044715 * (x * x * x)))))


def _dot_nt(a, b):
    return lax.dot_general(a, b, (((1,), (1,)), ((), ())), preferred_element_type=F32)


def _dot(a, b):
    return jnp.dot(a, b, preferred_element_type=F32)


def _layer_norm(z, g, b):
    mu = jnp.mean(z, axis=-1, keepdims=True)
    zc = z - mu
    var = jnp.mean(zc * zc, axis=-1, keepdims=True)
    return zc * lax.rsqrt(var + LN_EPS) * g + b


def _outproj_kernel(y1_ref, y2_ref, w1_ref, w2_ref, x_ref, g_ref, b_ref, o_ref, ob_ref, *, alpha):
    for r in range(x_ref.shape[0] // OUT_SUB):
        rs = slice(r * OUT_SUB, (r + 1) * OUT_SUB)
        y = _dot(y1_ref[rs, :], w1_ref[...]) + _dot(y2_ref[rs, :], w2_ref[...])
        out = _layer_norm(alpha * x_ref[rs, :] + y, g_ref[...], b_ref[...])
        o_ref[rs, :] = out
        ob_ref[rs, :] = out.astype(BF16)


def _outproj_ln(y1, y1_col, y2, y2_col, w_out, layer, x, g, b, alpha, tm, name):
    m, d = x.shape
    kh = w_out.shape[1] // 2
    return pl.pallas_call(
        functools.partial(_outproj_kernel, alpha=alpha),
        out_shape=(jax.ShapeDtypeStruct((m, d), F32), jax.ShapeDtypeStruct((m, d), BF16)),
        grid=(m // tm,),
        in_specs=[pl.BlockSpec((tm, kh), lambda i: (i, y1_col)),
                  pl.BlockSpec((tm, kh), lambda i: (i, y2_col)),
                  pl.BlockSpec((None, kh, d), lambda i: (layer, 0, 0)),
                  pl.BlockSpec((None, kh, d), lambda i: (layer, 1, 0)),
                  pl.BlockSpec((tm, d), lambda i: (i, 0)),
                  pl.BlockSpec((1, d), lambda i: (0, 0)),
                  pl.BlockSpec((1, d), lambda i: (0, 0))],
        out_specs=(pl.BlockSpec((tm, d), lambda i: (i, 0)),
                   pl.BlockSpec((tm, d), lambda i: (i, 0))),
        compiler_params=_cparams("parallel"),
        name=name,
    )(y1, y2, w_out, w_out, x, g, b)


def _odd_kernel(x_ref, wv_ref, wu_ref, wz_ref, g_ref, b_ref, sw_ref, sb_ref, o_ref, vs_ref, mu_ref, rstd_ref,
                *, tm):
    step = pl.program_id(1)
    nchunk = D_SGU // PROJ_CHUNK
    ngrp = PROJ_CHUNK // SGU_GROUP_DIM

    def project_v():
        for r in range(tm // ROW_SUB):
            rs = slice(r * ROW_SUB, (r + 1) * ROW_SUB)
            x = x_ref[rs, :]
            tot = None
            for c in range(nchunk):
                v = _gelu_tanh(_dot(x, wv_ref[c]))
                vs_ref[c, rs, :] = v
                part = jnp.sum(v, axis=-1, keepdims=True)
                tot = part if tot is None else tot + part
            mu = tot * (1.0 / D_SGU)
            sq = None
            for c in range(nchunk):
                vc = vs_ref[c, rs, :] - mu
                part = jnp.sum(vc * vc, axis=-1, keepdims=True)
                sq = part if sq is None else sq + part
            mu_ref[rs, :] = mu
            rstd_ref[rs, :] = lax.rsqrt(sq * (1.0 / D_SGU) + LN_EPS)

    def mix_columns(chunk):
        row = lax.broadcasted_iota(jnp.int32, (SGU_CHUNK, SGU_CHUNK), 0)
        col = lax.broadcasted_iota(jnp.int32, (SGU_CHUNK, SGU_CHUNK), 1)
        wgs = [jnp.where(col <= row, sw_ref[g], 0.0).astype(BF16) for g in range(ngrp)]
        for r in range(tm // ROW_SUB):
            rs = slice(r * ROW_SUB, (r + 1) * ROW_SUB)
            x = x_ref[rs, :]
            u = _gelu_tanh(_dot(x, wu_ref[...]))
            z = _dot(x, wz_ref[...])
            vn = ((vs_ref[chunk, rs, :] - mu_ref[rs, :]) * rstd_ref[rs, :] * g_ref[...] + b_ref[...]).astype(BF16)
            for g in range(ngrp):
                cs = slice(g * SGU_GROUP_DIM, (g + 1) * SGU_GROUP_DIM)
                for c in range(ROW_SUB // SGU_CHUNK):
                    ls = slice(c * SGU_CHUNK, (c + 1) * SGU_CHUNK)
                    os_ = slice(r * ROW_SUB + c * SGU_CHUNK, r * ROW_SUB + (c + 1) * SGU_CHUNK)
                    mixed = _dot(wgs[g], vn[ls, cs]) + sb_ref[g]
                    o_ref[os_, cs] = (u[ls, cs] * mixed * _silu(z[ls, cs])).astype(BF16)

    @pl.when(step == 0)
    def _():
        project_v()
        mix_columns(0)

    @pl.when(step > 0)
    def _():
        mix_columns(step)


def _odd_mixer(xb, w_in, layer, ln_g, ln_b, sgu_w, sgu_b, tm):
    m = xb.shape[0]
    cw = PROJ_CHUNK
    nstep = D_SGU // cw
    gps = cw // SGU_GROUP_DIM
    return pl.pallas_call(
        functools.partial(_odd_kernel, tm=tm),
        out_shape=jax.ShapeDtypeStruct((m, D_SGU), BF16),
        grid=(m // tm, nstep),
        in_specs=[pl.BlockSpec((tm, D_MODEL), lambda i, s: (i, 0)),
                  pl.BlockSpec((None, nstep, D_MODEL, cw), lambda i, s: (layer, 1, 0, 0)),
                  pl.BlockSpec((None, None, D_MODEL, cw), lambda i, s: (layer, s, 0, 0)),
                  pl.BlockSpec((None, None, D_MODEL, cw), lambda i, s: (layer, 2 * nstep + s, 0, 0)),
                  pl.BlockSpec((1, cw), lambda i, s: (0, s)),
                  pl.BlockSpec((1, cw), lambda i, s: (0, s)),
                  pl.BlockSpec((gps, SGU_CHUNK, SGU_CHUNK), lambda i, s: (s, 0, 0)),
                  pl.BlockSpec((gps, SGU_CHUNK, 1), lambda i, s: (s, 0, 0))],
        out_specs=pl.BlockSpec((tm, cw), lambda i, s: (i, s)),
        scratch_shapes=[pltpu.VMEM((nstep, tm, cw), F32),
                        pltpu.VMEM((tm, 1), F32),
                        pltpu.VMEM((tm, 1), F32)],
        compiler_params=_cparams("parallel", "arbitrary"),
        name="odd_mixer",
    )(xb, w_in, w_in, w_in, ln_g, ln_b, sgu_w, sgu_b)


def _conv_proj_kernel(x_ref, wh_ref, wb_ref, wc_ref, wz_ref, cw_ref, o_ref, u_ref, *, tm, tiles_per_seq):
    i = pl.program_id(1)
    nsub = tm // ROW_SUB

    @pl.when(i % tiles_per_seq == 0)
    def _():
        u_ref[0:SUBLANE, :] = jnp.zeros((SUBLANE, u_ref.shape[1]), F32)

    for r in range(nsub):
        rs = slice(r * ROW_SUB, (r + 1) * ROW_SUB)
        x = x_ref[rs, :]
        u_ref[SUBLANE + r * ROW_SUB:SUBLANE + (r + 1) * ROW_SUB, :] = _dot(x, wc_ref[...]) * _dot(x, wh_ref[...])
    for r in range(nsub):
        rs = slice(r * ROW_SUB, (r + 1) * ROW_SUB)
        x = x_ref[rs, :]
        conv = cw_ref[CONV_WIDTH - 1:CONV_WIDTH, :] * u_ref[SUBLANE + r * ROW_SUB:SUBLANE + (r + 1) * ROW_SUB, :]
        for k in range(CONV_WIDTH - 1):
            lo = SUBLANE + r * ROW_SUB - (CONV_WIDTH - 1 - k)
            conv = conv + cw_ref[k:k + 1, :] * u_ref[lo:lo + ROW_SUB, :]
        o_ref[rs, :] = (_dot(x, wb_ref[...]) * conv * _silu(_dot(x, wz_ref[...]))).astype(BF16)
    u_ref[0:SUBLANE, :] = u_ref[tm:tm + SUBLANE, :]


def _conv_proj(xb, w, layer, conv_w, seq, tm, tc):
    m = xb.shape[0]
    nct = D_CONV // tc

    def wspec(part):
        return pl.BlockSpec((None, D_MODEL, tc), lambda j, i: (layer, 0, (EV_A + part * D_CONV) // tc + j))

    return pl.pallas_call(
        functools.partial(_conv_proj_kernel, tm=tm, tiles_per_seq=seq // tm),
        out_shape=jax.ShapeDtypeStruct((m, D_CONV), BF16),
        grid=(nct, m // tm),
        in_specs=[pl.BlockSpec((tm, D_MODEL), lambda j, i: (i, 0)),
                  wspec(0), wspec(1), wspec(2), wspec(3),
                  pl.BlockSpec((CONV_WIDTH, tc), lambda j, i: (0, j))],
        out_specs=pl.BlockSpec((tm, tc), lambda j, i: (i, j)),
        scratch_shapes=[pltpu.VMEM((tm + SUBLANE, tc), F32)],
        compiler_params=_cparams("parallel", "arbitrary"),
        name="conv_proj",
    )(xb, w, w, w, w, conv_w)


def _nsa_proj_kernel(x_ref, wq_ref, wkc_ref, wks_ref, wkw_ref, wt_ref, hb_ref, ha_ref, *xb_ref):
    x = x_ref[...].astype(BF16)
    if xb_ref:
        xb_ref[0][...] = x
    cw = PROJ_CHUNK
    for c in range(D_NSA // cw):
        hb_ref[:, c * cw:(c + 1) * cw] = (_dot(x, wq_ref[:, c * cw:(c + 1) * cw]) * Q_SCALE).astype(BF16)
    hb_ref[:, HB_KS:HB_KS + cw] = _dot(x, wks_ref[...]).astype(BF16)
    hb_ref[:, HB_KW:HB_KW + cw] = _dot(x, wkw_ref[...]).astype(BF16)
    ha_ref[:, HA_KC:HA_KC + cw] = _dot(x, wkc_ref[...])
    for lo in range(0, HA_W - HA_BZ, cw):
        hi = min(lo + cw, HA_W - HA_BZ)
        ha_ref[:, HA_BZ + lo:HA_BZ + hi] = _dot(x, wt_ref[:, lo:hi])


def _nsa_proj(x, w, layer, w_tail, tm):
    m = x.shape[0]
    once = pl.Buffered(1)
    cw = PROJ_CHUNK
    assert EV_Q % D_NSA == 0 and EV_KC % cw == 0 and EV_KS % cw == 0 and 2 * D_KV == cw
    out_shape = [jax.ShapeDtypeStruct((m, HB_W), BF16), jax.ShapeDtypeStruct((m, HA_W), F32)]
    out_specs = [pl.BlockSpec((tm, HB_W), lambda i: (i, 0)), pl.BlockSpec((tm, HA_W), lambda i: (i, 0))]
    if x.dtype != BF16:
        out_shape.append(jax.ShapeDtypeStruct((m, D_MODEL), BF16))
        out_specs.append(pl.BlockSpec((tm, D_MODEL), lambda i: (i, 0)))
    return pl.pallas_call(
        _nsa_proj_kernel,
        out_shape=tuple(out_shape),
        grid=(m // tm,),
        in_specs=[pl.BlockSpec((tm, D_MODEL), lambda i: (i, 0)),
                  pl.BlockSpec((None, D_MODEL, D_NSA), lambda i: (layer, 0, EV_Q // D_NSA), pipeline_mode=once),
                  pl.BlockSpec((None, D_MODEL, cw), lambda i: (layer, 0, EV_KC // cw), pipeline_mode=once),
                  pl.BlockSpec((None, D_MODEL, cw), lambda i: (layer, 0, EV_KS // cw), pipeline_mode=once),
                  pl.BlockSpec((None, D_MODEL, cw), lambda i: (layer, 0, EV_KS // cw + 1), pipeline_mode=once),
                  pl.BlockSpec((D_MODEL, HA_W - HA_BZ), lambda i: (0, 0), pipeline_mode=once)],
        out_specs=tuple(out_specs),
        compiler_params=_cparams("parallel"),
        name="nsa_proj",
    )(x, w, w, w, w, w_tail)


def _compress_kernel(tok_ref, w1_ref, w2_ref, pos_ref, o_ref, b_ref, *, rows):
    half = CMP_STRIDE * HEAD_DIM
    x2 = jnp.concatenate(
        [tok_ref[pl.ds(l, rows, stride=CMP_STRIDE), :] for l in range(CMP_STRIDE)], axis=1).astype(BF16)
    lo = _dot(x2, w1_ref[0, 0:half, :])
    hi = _dot(x2, w1_ref[0, half:2 * half, :])
    b_ref[0:rows, :] = hi
    b_ref[rows:rows + SUBLANE, :] = jnp.zeros((SUBLANE, HEAD_DIM), F32)
    posb = _dot(jnp.broadcast_to(pos_ref[0], (SUBLANE, 2 * half)), w1_ref[0])[0:1, :]
    pre = lo + b_ref[1:rows + 1, :] + posb
    o_ref[0, 0] = _dot(_silu(pre).astype(BF16), w2_ref[0]).astype(BF16)


def _compress(ha, w1, w2, pos, batch, seq):
    rows = seq // CMP_STRIDE
    nkv = 2 * NSA_KV_GROUPS
    kc_block = HA_KC // HEAD_DIM
    return pl.pallas_call(
        functools.partial(_compress_kernel, rows=rows),
        out_shape=jax.ShapeDtypeStruct((batch, nkv, rows, HEAD_DIM), BF16),
        grid=(batch, nkv),
        in_specs=[pl.BlockSpec((seq, HEAD_DIM), lambda b, c: (b, kc_block + c)),
                  pl.BlockSpec((1, CMP_BLOCK * HEAD_DIM, HEAD_DIM), lambda b, c: (c // NSA_KV_GROUPS, 0, 0)),
                  pl.BlockSpec((1, HEAD_DIM, HEAD_DIM), lambda b, c: (c // NSA_KV_GROUPS, 0, 0)),
                  pl.BlockSpec((1, 1, CMP_BLOCK * HEAD_DIM), lambda b, c: (c // NSA_KV_GROUPS, 0, 0))],
        out_specs=pl.BlockSpec((1, 1, rows, HEAD_DIM), lambda b, c: (b, c, 0, 0)),
        scratch_shapes=[pltpu.VMEM((rows + SUBLANE, HEAD_DIM), F32)],
        compiler_params=_cparams("parallel", "arbitrary"),
        name="cmp_blocks",
    )(ha, w1, w2, pos)


def _t5_bucket(dist):
    n = jnp.maximum(dist, 0)
    max_exact = REL_BUCKETS // 2
    large = max_exact + (jnp.log(jnp.maximum(n, 1).astype(F32) / max_exact)
                         / math.log(REL_MAX_DIST / max_exact) * (REL_BUCKETS - max_exact)).astype(jnp.int32)
    large = jnp.minimum(large, REL_BUCKETS - 1)
    return jnp.where(n < max_exact, n, large)


def _table_lookup(dist, tab_ref, head):
    bkt = _t5_bucket(dist)
    acc = jnp.zeros(dist.shape, F32)
    for b in range(REL_BUCKETS):
        acc = jnp.where(bkt == b, tab_ref[b, head], acc)
    return acc


def _rel_bias(dist, valid, tab_ref, head):
    rows, cols = dist.shape
    if cols % LANE or REL_MAX_DIST > LANE:
        return jnp.where(valid, _table_lookup(dist, tab_ref, head), NEG_INF)
    lane_dist = lax.broadcasted_iota(jnp.int32, (SUBLANE, LANE), 1)
    near = jnp.broadcast_to(_table_lookup(lane_dist, tab_ref, head)[0:1, :], (rows, LANE))
    far = _table_lookup(jnp.full((SUBLANE, LANE), REL_MAX_DIST, jnp.int32), tab_ref, head)[0:1, 0:1]
    parts = []
    for c in range(cols // LANE):
        d = dist[:, c * LANE:(c + 1) * LANE]
        g = jnp.take_along_axis(near, jnp.clip(d, 0, LANE - 1), axis=1)
        parts.append(jnp.where(d >= REL_MAX_DIST, far, g))
    return jnp.where(valid, jnp.concatenate(parts, axis=1), NEG_INF)


def _bias_cmp_kernel(tab_ref, o_ref, *, tb, rows, n_cmp):
    head = pl.program_id(0)
    t = pl.program_id(1) * tb + lax.broadcasted_iota(jnp.int32, (tb, rows), 0)
    n = lax.broadcasted_iota(jnp.int32, (tb, rows), 1)
    dist = t - (n * CMP_STRIDE + CMP_BLOCK - 1)
    o_ref[0] = _rel_bias(dist, (dist >= 0) & (n < n_cmp), tab_ref, head) * LOG2E


def _bias_cmp(table, seq, tb):
    rows = seq // CMP_STRIDE
    n_cmp = (seq - CMP_BLOCK) // CMP_STRIDE + 1
    return pl.pallas_call(
        functools.partial(_bias_cmp_kernel, tb=tb, rows=rows, n_cmp=n_cmp),
        out_shape=jax.ShapeDtypeStruct((NSA_HEADS, seq, rows), F32),
        grid=(NSA_HEADS, seq // tb),
        in_specs=[pl.BlockSpec(memory_space=pltpu.SMEM)],
        out_specs=pl.BlockSpec((1, tb, rows), lambda h, i: (h, i, 0)),
        compiler_params=_cparams("parallel", "arbitrary"),
        name="bias_cmp",
    )(table)


def _bias_tiles_kernel(tab_ref, o_ref, *, tq):
    head = pl.program_id(0)
    kind = pl.program_id(1)
    ij = (lax.broadcasted_iota(jnp.int32, (tq, tq), 0) - lax.broadcasted_iota(jnp.int32, (tq, tq), 1))
    dist = jnp.where(kind == KIND_DIAG, ij, jnp.where(kind == KIND_SUB, tq + ij, WINDOW + ij))
    lo = jnp.where(kind == KIND_DIAG, 0, -tq)
    hi = jnp.where(kind == KIND_CORNER, 0, tq)
    far_dist = jnp.full((SUBLANE, LANE), tq + 1, jnp.int32)
    far = _rel_bias(far_dist, far_dist > 0, tab_ref, head)[0:1, 0:1]
    o_ref[0, 0] = (_rel_bias(dist, (ij >= lo) & (ij < hi), tab_ref, head) - far) * LOG2E


def _bias_tiles(table, tq):
    assert WINDOW % tq == 0 and tq + 1 >= REL_MAX_DIST
    return pl.pallas_call(
        functools.partial(_bias_tiles_kernel, tq=tq),
        out_shape=jax.ShapeDtypeStruct((NSA_KV_GROUPS, N_BIAS_KINDS, NSA_HPG * tq, tq), F32),
        grid=(NSA_HEADS, N_BIAS_KINDS),
        in_specs=[pl.BlockSpec(memory_space=pltpu.SMEM)],
        out_specs=pl.BlockSpec((1, 1, tq, tq), lambda h, k: (h // NSA_HPG, k, h % NSA_HPG, 0)),
        compiler_params=_cparams("parallel", "arbitrary"),
        name="bias_tiles",
    )(table)


def _cmp_attn_kernel(q_ref, kc_ref, vc_ref, bias_ref, ov_ref, oc_ref, sel_ref, *, tq, n_sel, n_top):
    kc = kc_ref[0, 0]
    vc = vc_ref[0, 0]
    psum = None
    for j in range(NSA_HPG):
        hs = slice(j * HEAD_DIM, (j + 1) * HEAD_DIM)
        bias = bias_ref[j]
        s = _dot_nt(q_ref[:, hs], kc) + bias
        m = jnp.max(s, axis=-1, keepdims=True)
        e = jnp.exp2(s - m)
        p = e / jnp.sum(e, axis=-1, keepdims=True)
        p = jnp.where(bias > 0.5 * NEG_INF, p, 0.0)
        oc_ref[:, hs] = _dot(p.astype(BF16), vc)
        psum = p if psum is None else psum + p
    ov = ov_ref[...]
    imp = None
    rem = psum
    for _ in range(3):
        piece = rem.astype(BF16)
        part = _dot_nt(ov, piece)
        imp = part if imp is None else imp + part
        rem = rem - piece.astype(F32)
    t = pl.program_id(1) * tq + lax.broadcasted_iota(jnp.int32, (n_sel, tq), 1)
    cur = jnp.right_shift(t, int(math.log2(SEL_BLOCK)))
    blk = lax.broadcasted_iota(jnp.int32, (n_sel, tq), 0)
    forced = (blk == 0) | (blk == cur) | (blk == cur - 1)
    imp = jnp.where(blk > cur, -1.0, jnp.where(forced, FORCED_SCORE, imp))
    groups = [imp[g * SUBLANE:(g + 1) * SUBLANE, :] for g in range(n_sel // SUBLANE)]
    ranks = [jnp.zeros((SUBLANE, tq), jnp.int32) for _ in groups]
    sub = lax.broadcasted_iota(jnp.int32, (SUBLANE, tq), 0)
    for i in range(n_sel):
        row = imp[i:i + 1, :]
        for g, x in enumerate(groups):
            if g * SUBLANE > i:
                ahead = row >= x
            elif (g + 1) * SUBLANE - 1 < i:
                ahead = row > x
            else:
                ahead = (row > x) | ((row == x) & (sub > i - g * SUBLANE))
            ranks[g] = ranks[g] + ahead.astype(jnp.int32)
    sel_t = jnp.where(jnp.concatenate(ranks, axis=0) < n_top, 0.0, NEG_INF)
    if n_sel < LANE:
        sel_t = jnp.concatenate([sel_t, jnp.zeros((LANE - n_sel, tq), F32)], axis=0)
    sel_ref[0, 0] = sel_t.T.astype(BF16)


def _cmp_attn(hb, kcv, bias_c, ov, batch, seq, tq):
    rows = seq // CMP_STRIDE
    n_sel = seq // SEL_BLOCK
    n_top = min(N_SELECT, n_sel)
    assert n_sel <= LANE and n_sel % SUBLANE == 0
    nq = seq // tq
    g_n = NSA_KV_GROUPS
    return pl.pallas_call(
        functools.partial(_cmp_attn_kernel, tq=tq, n_sel=n_sel, n_top=n_top),
        out_shape=(jax.ShapeDtypeStruct((batch * seq, D_NSA), F32),
                   jax.ShapeDtypeStruct((batch, g_n, seq, LANE), BF16)),
        grid=(g_n, nq, batch),
        in_specs=[pl.BlockSpec((tq, GROUP_W), lambda g, i, b: (b * nq + i, HB_Q // GROUP_W + g)),
                  pl.BlockSpec((1, 1, rows, HEAD_DIM), lambda g, i, b: (b, g, 0, 0)),
                  pl.BlockSpec((1, 1, rows, HEAD_DIM), lambda g, i, b: (b, g_n + g, 0, 0)),
                  pl.BlockSpec((NSA_HPG, tq, rows), lambda g, i, b: (g, i, 0)),
                  pl.BlockSpec((n_sel, rows), lambda g, i, b: (0, 0))],
        out_specs=(pl.BlockSpec((tq, GROUP_W), lambda g, i, b: (b * nq + i, g)),
                   pl.BlockSpec((1, 1, tq, LANE), lambda g, i, b: (b, g, i, 0))),
        compiler_params=_cparams("parallel", "parallel", "arbitrary"),
        name="cmp_attn",
    )(hb, kcv, kcv, bias_c, ov)


FLASH_ROWS = 128


def _flash_init(state):
    _, _, m_ref, _, acc_ref = state
    m_ref[...] = jnp.full(m_ref.shape, -3e38, F32)
    acc_ref[...] = jnp.zeros(acc_ref.shape, F32)


def _flash_scores(qa_ref, kdim, k, s_ref, h, tq):
    hr = slice(h * tq, (h + 1) * tq)
    s_ref[hr, :] = _dot_nt(qa_ref[hr, 0:kdim], k)


def _flash_step(qa_ref, bias_ref, kind, v, nxt, state, tq):
    s_ref, p_ref, m_ref, a_ref, acc_ref = state
    tk = s_ref.shape[1]
    for h in range(NSA_HPG):
        hr = slice(h * tq, (h + 1) * tq)
        for r in range(tq // FLASH_ROWS):
            rs = slice(h * tq + r * FLASH_ROWS, h * tq + (r + 1) * FLASH_ROWS)
            s = s_ref[rs, :]
            if kind is not None:
                s = s + bias_ref[0, kind, rs, :]
            m_old = m_ref[rs, :]
            m_new = jnp.maximum(m_old, jnp.max(s, axis=-1, keepdims=True))
            p_ref[rs, :] = jnp.exp2(s - jnp.tile(m_new, (1, tk // LANE))).astype(BF16)
            a_ref[rs, :] = jnp.exp2(m_old - m_new)
            m_ref[rs, :] = m_new
        if nxt is not None:
            _flash_scores(qa_ref, nxt[0], nxt[1], s_ref, h, tq)
        acc_ref[hr, :] = jnp.tile(a_ref[hr, :], (1, 2)) * acc_ref[hr, :] + _dot(p_ref[hr, :], v)


def _selwin_kernel(q_ref, ks_ref, vs_ref, kw_ref, vw_ref, oh_ref, sel_ref, bias_ref, oc_ref, gt_ref, bz_ref,
                   o_ref, qa_ref, s_ref, p_ref, m_ref, a_ref, acc_ref, os_ref, *, tq):
    qi = pl.program_id(2)
    state = (s_ref, p_ref, m_ref, a_ref, acc_ref)
    aug = 2 * HEAD_DIM
    ones = jnp.ones((tq, HEAD_DIM), BF16)
    for j in range(NSA_HPG):
        qa_ref[j * tq:(j + 1) * tq, 0:HEAD_DIM] = q_ref[:, j * HEAD_DIM:(j + 1) * HEAD_DIM]
        qa_ref[j * tq:(j + 1) * tq, HEAD_DIM:aug] = sel_ref[0, 0]

    def rows_of(kt):
        return pl.ds(pl.multiple_of(kt * tq, tq), tq)

    def sel_keys(kt):
        return aug, jnp.concatenate([ks_ref[rows_of(kt), :], oh_ref[rows_of(kt), :]], axis=1)

    def win_keys(kt):
        return HEAD_DIM, kw_ref[rows_of(kt), :]

    def values(v_ref, kt):
        return jnp.concatenate([v_ref[rows_of(kt), :], ones], axis=1)

    _flash_init(state)
    for h in range(NSA_HPG):
        _flash_scores(qa_ref, *sel_keys(0), s_ref, h, tq)

    def far_body(kt, carry):
        _flash_step(qa_ref, bias_ref, None, values(vs_ref, kt), sel_keys(kt + 1), state, tq)
        return carry

    lax.fori_loop(0, jnp.maximum(qi - 1, 0), far_body, 0)

    def near_tiles(first):
        if not first:
            _flash_step(qa_ref, bias_ref, KIND_SUB, values(vs_ref, qi - 1), sel_keys(qi), state, tq)
        _flash_step(qa_ref, bias_ref, KIND_DIAG, values(vs_ref, qi), win_keys(jnp.maximum(qi - 1, 0)), state, tq)
        os_ref[...] = acc_ref[:, 0:HEAD_DIM] / acc_ref[:, HEAD_DIM:aug]
        _flash_init(state)
        if not first:
            _flash_step(qa_ref, bias_ref, KIND_CORNER, values(vw_ref, qi - 1), win_keys(qi), state, tq)
        _flash_step(qa_ref, bias_ref, KIND_DIAG, values(vw_ref, qi), None, state, tq)
        gate = _sigmoid(gt_ref[...])
        for j in range(NSA_HPG):
            hs = slice(j * HEAD_DIM, (j + 1) * HEAD_DIM)
            rs = slice(j * tq, (j + 1) * tq)
            o_w = acc_ref[rs, 0:HEAD_DIM] / acc_ref[rs, HEAD_DIM:aug]
            o = (gate[:, j:j + 1] * oc_ref[:, hs]
                 + gate[:, NSA_HPG + j:NSA_HPG + j + 1] * os_ref[rs, :]
                 + gate[:, 2 * NSA_HPG + j:2 * NSA_HPG + j + 1] * o_w)
            o_ref[:, hs] = (o * _silu(bz_ref[:, hs])).astype(BF16)

    pl.when(qi == 0)(functools.partial(near_tiles, True))
    pl.when(qi > 0)(functools.partial(near_tiles, False))


def _selwin_attn(hb, ha, onehot, sel, bias_t, o_c, batch, seq, tq):
    nq = seq // tq
    rows = NSA_HPG * tq
    assert tq == WINDOW
    once = pl.Buffered(1)
    kv_spec = lambda base: pl.BlockSpec((seq, HEAD_DIM), lambda g, b, i: (b, base // HEAD_DIM + g))
    row_g = lambda g, b, i: (b * nq + i, g)
    return pl.pallas_call(
        functools.partial(_selwin_kernel, tq=tq),
        out_shape=jax.ShapeDtypeStruct((batch * seq, D_NSA), BF16),
        grid=(NSA_KV_GROUPS, batch, nq),
        in_specs=[pl.BlockSpec((tq, GROUP_W), lambda g, b, i: (b * nq + i, HB_Q // GROUP_W + g)),
                  kv_spec(HB_KS), kv_spec(HB_VS), kv_spec(HB_KW), kv_spec(HB_VW),
                  pl.BlockSpec((seq, LANE), lambda g, b, i: (0, 0), pipeline_mode=once),
                  pl.BlockSpec((1, 1, tq, LANE), lambda g, b, i: (b, g, i, 0)),
                  pl.BlockSpec((1, N_BIAS_KINDS, rows, tq), lambda g, b, i: (g, 0, 0, 0), pipeline_mode=once),
                  pl.BlockSpec((tq, GROUP_W), row_g),
                  pl.BlockSpec((tq, LANE), lambda g, b, i: (b * nq + i, HA_GT // LANE + g)),
                  pl.BlockSpec((tq, GROUP_W), lambda g, b, i: (b * nq + i, HA_BZ // GROUP_W + g))],
        out_specs=pl.BlockSpec((tq, GROUP_W), row_g),
        scratch_shapes=[pltpu.VMEM((rows, 2 * HEAD_DIM), BF16),
                        pltpu.VMEM((rows, tq), F32),
                        pltpu.VMEM((rows, tq), BF16),
                        pltpu.VMEM((rows, LANE), F32),
                        pltpu.VMEM((rows, LANE), F32),
                        pltpu.VMEM((rows, 2 * HEAD_DIM), F32),
                        pltpu.VMEM((rows, HEAD_DIM), F32)],
        compiler_params=_cparams("parallel", "parallel", "arbitrary"),
        name="selwin_attn",
    )(hb, hb, hb, hb, hb, onehot, sel, bias_t, o_c, ha, ha)


def _even_tail_weights(w):
    w_g = w[:, EV_GT:EV_BZ].reshape(D_MODEL, N_BRANCH, NSA_KV_GROUPS, NSA_HPG)
    w_g = jnp.transpose(w_g, (0, 2, 1, 3)).reshape(D_MODEL, NSA_KV_GROUPS, N_BRANCH * NSA_HPG)
    w_g = jnp.pad(w_g, ((0, 0), (0, 0), (0, LANE - N_BRANCH * NSA_HPG))).reshape(D_MODEL, NSA_KV_GROUPS * LANE)
    return jnp.concatenate([w[:, EV_BZ:EV_END], w_g], axis=1).astype(BF16)


def _cast_chunk_kernel(w_ref, o_ref):
    o_ref[...] = w_ref[...].astype(BF16)


def _cast_column_chunks(w, cw):
    layers, k, n = w.shape
    return pl.pallas_call(
        _cast_chunk_kernel,
        out_shape=jax.ShapeDtypeStruct((layers, n // cw, k, cw), BF16),
        grid=(layers, n // cw),
        in_specs=[pl.BlockSpec((None, k, cw), lambda l, c: (l, 0, c))],
        out_specs=pl.BlockSpec((None, None, k, cw), lambda l, c: (l, c, 0, 0)),
        compiler_params=_cparams("parallel", "parallel"),
        name="cast_chunks",
    )(w)


def _overlap_matrix(seq):
    rows = seq // CMP_STRIDE
    n_cmp = (seq - CMP_BLOCK) // CMP_STRIDE + 1
    n_sel = seq // SEL_BLOCK
    cstart = np.arange(rows)[None, :] * CMP_STRIDE
    sstart = np.arange(n_sel)[:, None] * SEL_BLOCK
    ov = (cstart < sstart + SEL_BLOCK) & (cstart + CMP_BLOCK > sstart) & (np.arange(rows)[None, :] < n_cmp)
    return jnp.asarray(ov.astype(np.float32), dtype=BF16)


def _block_onehot(seq):
    blk = np.arange(seq)[:, None] // SEL_BLOCK
    return jnp.asarray((blk == np.arange(LANE)[None, :]).astype(np.float32), dtype=BF16)


def kernel(x, rel_bias_table, ln_g, ln_b, ev_w_in, ev_conv_w, ev_cmp_pos, ev_cmp_w1, ev_cmp_w2, ev_w_out,
           od_w_in, od_ln_g, od_ln_b, od_sgu_w, od_sgu_b, od_w_out):
    batch, seq, d = x.shape
    depth = ln_g.shape[0]
    alpha = (2 * depth) ** 0.25
    m = batch * seq
    tq = ATT_TILE
    assert d == D_MODEL and seq % 1024 == 0 and ev_w_in.shape[-1] == EV_END

    xf = x.reshape(m, d)
    xb = None
    bias_c = _bias_cmp(rel_bias_table, seq, tb=1024)
    bias_t = _bias_tiles(rel_bias_table, tq)
    ov = _overlap_matrix(seq)
    onehot = _block_onehot(seq)
    ev_w, ev_wo = ev_w_in.astype(BF16), ev_w_out.astype(BF16)
    od_wo = od_w_out.astype(BF16)
    od_w = _cast_column_chunks(od_w_in, PROJ_CHUNK)

    for layer in range(depth):
        i = layer // 2
        g = ln_g[layer].reshape(1, d)
        b = ln_b[layer].reshape(1, d)
        if layer % 2 == 0:
            w_tail = _even_tail_weights(ev_w[i])
            hb, ha, *cast = _nsa_proj(xf if xb is None else xb, ev_w, i, w_tail, tm=512)
            xb = cast[0] if cast else xb
            y_a = _conv_proj(xb, ev_w, i, ev_conv_w[i], seq, tm=1024, tc=256)
            kcv = _compress(ha, ev_cmp_w1[i].astype(BF16), ev_cmp_w2[i].astype(BF16),
                            ev_cmp_pos[i].reshape(2, 1, CMP_BLOCK * HEAD_DIM).astype(BF16), batch, seq)
            o_c, sel = _cmp_attn(hb, kcv, bias_c, ov, batch, seq, CMP_TILE)
            y_b = _selwin_attn(hb, ha, onehot, sel, bias_t, o_c, batch, seq, tq)
            xf, xb = _outproj_ln(y_a, 0, y_b, 0, ev_wo, i, xf, g, b, alpha, 512, "outproj_even")
        else:
            y = _odd_mixer(xb, od_w, i, od_ln_g[i].reshape(1, d), od_ln_b[i].reshape(1, d),
                           od_sgu_w[i], od_sgu_b[i].reshape(SGU_GROUPS, SGU_CHUNK, 1), tm=1024)
            xf, xb = _outproj_ln(y, 0, y, 1, od_wo, i, xf, g, b, alpha, 512, "outproj_odd")
    return xf.reshape(batch, seq, d)
```

```python
import functools
import math

import numpy as np
import jax
import jax.numpy as jnp
from jax import lax
from jax.experimental import pallas as pl
from jax.experimental.pallas import tpu as pltpu

F32 = jnp.float32
BF16 = jnp.bfloat16

D_MODEL = 2048
CONV_WIDTH = 3
D_CONV = 1024
NSA_HEADS = 8
NSA_KV_GROUPS = 2
NSA_HPG = NSA_HEADS // NSA_KV_GROUPS
HEAD_DIM = 128
D_NSA = NSA_HEADS * HEAD_DIM
D_KV = NSA_KV_GROUPS * HEAD_DIM
CMP_BLOCK = 32
CMP_STRIDE = 16
SEL_BLOCK = 64
N_SELECT = 16
WINDOW = 512
N_BRANCH = 3
D_SGU = D_MODEL
SGU_GROUPS = 8
SGU_CHUNK = 128
SGU_GROUP_DIM = D_SGU // SGU_GROUPS
REL_BUCKETS = 32
REL_MAX_DIST = 128
LN_EPS = 1e-5
NEG_INF = -1e30
FORCED_SCORE = 1e9
GROUP_W = NSA_HPG * HEAD_DIM
LOG2E = math.log2(math.e)
Q_SCALE = HEAD_DIM ** -0.5 * LOG2E

LANE = 128
SUBLANE = 8
V7X_VMEM_BYTES = 64 * 1024 * 1024
VMEM_LIMIT = V7X_VMEM_BYTES * 7 // 8

EV_A = 0
EV_Q = 4 * D_CONV
EV_KC = EV_Q + D_NSA
EV_KS = EV_KC + 2 * D_KV
EV_GT = EV_KS + 4 * D_KV
EV_BZ = EV_GT + N_BRANCH * NSA_HEADS
EV_END = EV_BZ + D_NSA

HB_Q, HB_KS, HB_VS, HB_KW, HB_VW = 0, 1024, 1280, 1536, 1792
HA_KC, HA_BZ, HA_GT = 0, 512, 1536
HB_W = 2048
HA_W = HA_GT + NSA_KV_GROUPS * LANE
PROJ_CHUNK = 512

NSA_ROWS = 512
CONV_ROWS, CONV_COLS = 1024, 256
ODD_ROWS = 1024
OUT_ROWS = 512
BIAS_ROWS = 1024
CMP_TILE = 512
ATT_TILE = 512
ROW_SUB = 256
OUT_SUB = 128
KIND_DIAG, KIND_SUB, KIND_CORNER = 0, 1, 2
N_BIAS_KINDS = 3


def _cparams(*sem):
    return pltpu.CompilerParams(dimension_semantics=sem, vmem_limit_bytes=VMEM_LIMIT)


def _sigmoid(x):
    return 1.0 / (1.0 + jnp.exp(-x))


def _silu(x):
    return x * _sigmoid(x)


def _gelu_tanh(x):
    c = math.sqrt(2.0 / math.pi)
    return x * (0.5 * (1.0 + jnp.tanh(c * (x + 0.044715 * (x * x * x)))))


def _dot_nt(a, b):
    return lax.dot_general(a, b, (((1,), (1,)), ((), ())), preferred_element_type=F32)


def _dot(a, b):
    return jnp.dot(a, b, preferred_element_type=F32)


def _layer_norm(z, g, b):
    mu = jnp.mean(z, axis=-1, keepdims=True)
    zc = z - mu
    var = jnp.mean(zc * zc, axis=-1, keepdims=True)
    return zc * lax.rsqrt(var + LN_EPS) * g + b


def _outproj_kernel(y1_ref, y2_ref, w1_ref, w2_ref, x_ref, g_ref, b_ref, o_ref, ob_ref, *, alpha):
    for r in range(x_ref.shape[0] // OUT_SUB):
        rs = slice(r * OUT_SUB, (r + 1) * OUT_SUB)
        y = _dot(y1_ref[rs, :], w1_ref[...]) + _dot(y2_ref[rs, :], w2_ref[...])
        out = _layer_norm(alpha * x_ref[rs, :] + y, g_ref[...], b_ref[...])
        o_ref[rs, :] = out
        ob_ref[rs, :] = out.astype(BF16)


def _outproj_ln(y1, y1_col, y2, y2_col, w_out, layer, x, g, b, alpha, tm, name):
    m, d = x.shape
    kh = w_out.shape[1] // 2
    return pl.pallas_call(
        functools.partial(_outproj_kernel, alpha=alpha),
        out_shape=(jax.ShapeDtypeStruct((m, d), F32), jax.ShapeDtypeStruct((m, d), BF16)),
        grid=(m // tm,),
        in_specs=[pl.BlockSpec((tm, kh), lambda i: (i, y1_col)),
                  pl.BlockSpec((tm, kh), lambda i: (i, y2_col)),
                  pl.BlockSpec((None, kh, d), lambda i: (layer, 0, 0)),
                  pl.BlockSpec((None, kh, d), lambda i: (layer, 1, 0)),
                  pl.BlockSpec((tm, d), lambda i: (i, 0)),
                  pl.BlockSpec((1, d), lambda i: (0, 0)),
                  pl.BlockSpec((1, d), lambda i: (0, 0))],
        out_specs=(pl.BlockSpec((tm, d), lambda i: (i, 0)),
                   pl.BlockSpec((tm, d), lambda i: (i, 0))),
        compiler_params=_cparams("parallel"),
        name=name,
    )(y1, y2, w_out, w_out, x, g, b)


def _odd_kernel(x_ref, wv_ref, wu_ref, wz_ref, g_ref, b_ref, sw_ref, sb_ref, o_ref, vs_ref, mu_ref, rstd_ref,
                *, tm):
    step = pl.program_id(1)
    nchunk = D_SGU // PROJ_CHUNK
    ngrp = PROJ_CHUNK // SGU_GROUP_DIM

    def project_v():
        for r in range(tm // ROW_SUB):
            rs = slice(r * ROW_SUB, (r + 1) * ROW_SUB)
            x = x_ref[rs, :]
            tot = None
            for c in range(nchunk):
                v = _gelu_tanh(_dot(x, wv_ref[c]))
                vs_ref[c, rs, :] = v
                part = jnp.sum(v, axis=-1, keepdims=True)
                tot = part if tot is None else tot + part
            mu = tot * (1.0 / D_SGU)
            sq = None
            for c in range(nchunk):
                vc = vs_ref[c, rs, :] - mu
                part = jnp.sum(vc * vc, axis=-1, keepdims=True)
                sq = part if sq is None else sq + part
            mu_ref[rs, :] = mu
            rstd_ref[rs, :] = lax.rsqrt(sq * (1.0 / D_SGU) + LN_EPS)

    def mix_columns(chunk):
        row = lax.broadcasted_iota(jnp.int32, (SGU_CHUNK, SGU_CHUNK), 0)
        col = lax.broadcasted_iota(jnp.int32, (SGU_CHUNK, SGU_CHUNK), 1)
        wgs = [jnp.where(col <= row, sw_ref[g], 0.0).astype(BF16) for g in range(ngrp)]
        for r in range(tm // ROW_SUB):
            rs = slice(r * ROW_SUB, (r + 1) * ROW_SUB)
            x = x_ref[rs, :]
            u = _gelu_tanh(_dot(x, wu_ref[...]))
            z = _dot(x, wz_ref[...])
            vn = ((vs_ref[chunk, rs, :] - mu_ref[rs, :]) * rstd_ref[rs, :] * g_ref[...] + b_ref[...]).astype(BF16)
            for g in range(ngrp):
                cs = slice(g * SGU_GROUP_DIM, (g + 1) * SGU_GROUP_DIM)
                for c in range(ROW_SUB // SGU_CHUNK):
                    ls = slice(c * SGU_CHUNK, (c + 1) * SGU_CHUNK)
                    os_ = slice(r * ROW_SUB + c * SGU_CHUNK, r * ROW_SUB + (c + 1) * SGU_CHUNK)
                    mixed = _dot(wgs[g], vn[ls, cs]) + sb_ref[g]
                    o_ref[os_, cs] = (u[ls, cs] * mixed * _silu(z[ls, cs])).astype(BF16)

    @pl.when(step == 0)
    def _():
        project_v()
        mix_columns(0)

    @pl.when(step > 0)
    def _():
        mix_columns(step)


def _odd_mixer(xb, w_in, layer, ln_g, ln_b, sgu_w, sgu_b, tm):
    m = xb.shape[0]
    cw = PROJ_CHUNK
    nstep = D_SGU // cw
    gps = cw // SGU_GROUP_DIM
    return pl.pallas_call(
        functools.partial(_odd_kernel, tm=tm),
        out_shape=jax.ShapeDtypeStruct((m, D_SGU), BF16),
        grid=(m // tm, nstep),
        in_specs=[pl.BlockSpec((tm, D_MODEL), lambda i, s: (i, 0)),
                  pl.BlockSpec((None, nstep, D_MODEL, cw), lambda i, s: (layer, 1, 0, 0)),
                  pl.BlockSpec((None, None, D_MODEL, cw), lambda i, s: (layer, s, 0, 0)),
                  pl.BlockSpec((None, None, D_MODEL, cw), lambda i, s: (layer, 2 * nstep + s, 0, 0)),
                  pl.BlockSpec((1, cw), lambda i, s: (0, s)),
                  pl.BlockSpec((1, cw), lambda i, s: (0, s)),
                  pl.BlockSpec((gps, SGU_CHUNK, SGU_CHUNK), lambda i, s: (s, 0, 0)),
                  pl.BlockSpec((gps, SGU_CHUNK, 1), lambda i, s: (s, 0, 0))],
        out_specs=pl.BlockSpec((tm, cw), lambda i, s: (i, s)),
        scratch_shapes=[pltpu.VMEM((nstep, tm, cw), F32),
                        pltpu.VMEM((tm, 1), F32),
                        pltpu.VMEM((tm, 1), F32)],
        compiler_params=_cparams("parallel", "arbitrary"),
        name="odd_mixer",
    )(xb, w_in, w_in, w_in, ln_g, ln_b, sgu_w, sgu_b)


def _conv_proj_kernel(x_ref, wh_ref, wb_ref, wc_ref, wz_ref, cw_ref, o_ref, u_ref, *, tm, tiles_per_seq):
    i = pl.program_id(1)
    nsub = tm // ROW_SUB

    @pl.when(i % tiles_per_seq == 0)
    def _():
        u_ref[0:SUBLANE, :] = jnp.zeros((SUBLANE, u_ref.shape[1]), F32)

    for r in range(nsub):
        rs = slice(r * ROW_SUB, (r + 1) * ROW_SUB)
        x = x_ref[rs, :]
        u_ref[SUBLANE + r * ROW_SUB:SUBLANE + (r + 1) * ROW_SUB, :] = _dot(x, wc_ref[...]) * _dot(x, wh_ref[...])
    for r in range(nsub):
        rs = slice(r * ROW_SUB, (r + 1) * ROW_SUB)
        x = x_ref[rs, :]
        conv = cw_ref[CONV_WIDTH - 1:CONV_WIDTH, :] * u_ref[SUBLANE + r * ROW_SUB:SUBLANE + (r + 1) * ROW_SUB, :]
        for k in range(CONV_WIDTH - 1):
            lo = SUBLANE + r * ROW_SUB - (CONV_WIDTH - 1 - k)
            conv = conv + cw_ref[k:k + 1, :] * u_ref[lo:lo + ROW_SUB, :]
        o_ref[rs, :] = (_dot(x, wb_ref[...]) * conv * _silu(_dot(x, wz_ref[...]))).astype(BF16)
    u_ref[0:SUBLANE, :] = u_ref[tm:tm + SUBLANE, :]


def _conv_proj(xb, w, layer, conv_w, seq, tm, tc):
    m = xb.shape[0]
    nct = D_CONV // tc

    def wspec(part):
        return pl.BlockSpec((None, D_MODEL, tc), lambda j, i: (layer, 0, (EV_A + part * D_CONV) // tc + j))

    return pl.pallas_call(
        functools.partial(_conv_proj_kernel, tm=tm, tiles_per_seq=seq // tm),
        out_shape=jax.ShapeDtypeStruct((m, D_CONV), BF16),
        grid=(nct, m // tm),
        in_specs=[pl.BlockSpec((tm, D_MODEL), lambda j, i: (i, 0)),
                  wspec(0), wspec(1), wspec(2), wspec(3),
                  pl.BlockSpec((CONV_WIDTH, tc), lambda j, i: (0, j))],
        out_specs=pl.BlockSpec((tm, tc), lambda j, i: (i, j)),
        scratch_shapes=[pltpu.VMEM((tm + SUBLANE, tc), F32)],
        compiler_params=_cparams("parallel", "arbitrary"),
        name="conv_proj",
    )(xb, w, w, w, w, conv_w)


def _nsa_proj_kernel(x_ref, wq_ref, wkc_ref, wks_ref, wkw_ref, wt_ref, hb_ref, ha_ref, *xb_ref):
    x = x_ref[...].astype(BF16)
    if xb_ref:
        xb_ref[0][...] = x
    cw = PROJ_CHUNK
    for c in range(D_NSA // cw):
        hb_ref[:, c * cw:(c + 1) * cw] = (_dot(x, wq_ref[:, c * cw:(c + 1) * cw]) * Q_SCALE).astype(BF16)
    hb_ref[:, HB_KS:HB_KS + cw] = _dot(x, wks_ref[...]).astype(BF16)
    hb_ref[:, HB_KW:HB_KW + cw] = _dot(x, wkw_ref[...]).astype(BF16)
    ha_ref[:, HA_KC:HA_KC + cw] = _dot(x, wkc_ref[...])
    for lo in range(0, HA_W - HA_BZ, cw):
        hi = min(lo + cw, HA_W - HA_BZ)
        ha_ref[:, HA_BZ + lo:HA_BZ + hi] = _dot(x, wt_ref[:, lo:hi])


def _nsa_proj(x, w, layer, w_tail, tm):
    m = x.shape[0]
    once = pl.Buffered(1)
    cw = PROJ_CHUNK
    assert EV_Q % D_NSA == 0 and EV_KC % cw == 0 and EV_KS % cw == 0 and 2 * D_KV == cw
    out_shape = [jax.ShapeDtypeStruct((m, HB_W), BF16), jax.ShapeDtypeStruct((m, HA_W), F32)]
    out_specs = [pl.BlockSpec((tm, HB_W), lambda i: (i, 0)), pl.BlockSpec((tm, HA_W), lambda i: (i, 0))]
    if x.dtype != BF16:
        out_shape.append(jax.ShapeDtypeStruct((m, D_MODEL), BF16))
        out_specs.append(pl.BlockSpec((tm, D_MODEL), lambda i: (i, 0)))
    return pl.pallas_call(
        _nsa_proj_kernel,
        out_shape=tuple(out_shape),
        grid=(m // tm,),
        in_specs=[pl.BlockSpec((tm, D_MODEL), lambda i: (i, 0)),
                  pl.BlockSpec((None, D_MODEL, D_NSA), lambda i: (layer, 0, EV_Q // D_NSA), pipeline_mode=once),
                  pl.BlockSpec((None, D_MODEL, cw), lambda i: (layer, 0, EV_KC // cw), pipeline_mode=once),
                  pl.BlockSpec((None, D_MODEL, cw), lambda i: (layer, 0, EV_KS // cw), pipeline_mode=once),
                  pl.BlockSpec((None, D_MODEL, cw), lambda i: (layer, 0, EV_KS // cw + 1), pipeline_mode=once),
                  pl.BlockSpec((D_MODEL, HA_W - HA_BZ), lambda i: (0, 0), pipeline_mode=once)],
        out_specs=tuple(out_specs),
        compiler_params=_cparams("parallel"),
        name="nsa_proj",
    )(x, w, w, w, w, w_tail)


def _compress_kernel(tok_ref, w1_ref, w2_ref, pos_ref, o_ref, b_ref, *, rows):
    half = CMP_STRIDE * HEAD_DIM
    x2 = jnp.concatenate(
        [tok_ref[pl.ds(l, rows, stride=CMP_STRIDE), :] for l in range(CMP_STRIDE)], axis=1).astype(BF16)
    lo = _dot(x2, w1_ref[0, 0:half, :])
    hi = _dot(x2, w1_ref[0, half:2 * half, :])
    b_ref[0:rows, :] = hi
    b_ref[rows:rows + SUBLANE, :] = jnp.zeros((SUBLANE, HEAD_DIM), F32)
    posb = _dot(jnp.broadcast_to(pos_ref[0], (SUBLANE, 2 * half)), w1_ref[0])[0:1, :]
    pre = lo + b_ref[1:rows + 1, :] + posb
    o_ref[0, 0] = _dot(_silu(pre).astype(BF16), w2_ref[0]).astype(BF16)


def _compress(ha, w1, w2, pos, batch, seq):
    rows = seq // CMP_STRIDE
    nkv = 2 * NSA_KV_GROUPS
    kc_block = HA_KC // HEAD_DIM
    return pl.pallas_call(
        functools.partial(_compress_kernel, rows=rows),
        out_shape=jax.ShapeDtypeStruct((batch, nkv, rows, HEAD_DIM), BF16),
        grid=(batch, nkv),
        in_specs=[pl.BlockSpec((seq, HEAD_DIM), lambda b, c: (b, kc_block + c)),
                  pl.BlockSpec((1, CMP_BLOCK * HEAD_DIM, HEAD_DIM), lambda b, c: (c // NSA_KV_GROUPS, 0, 0)),
                  pl.BlockSpec((1, HEAD_DIM, HEAD_DIM), lambda b, c: (c // NSA_KV_GROUPS, 0, 0)),
                  pl.BlockSpec((1, 1, CMP_BLOCK * HEAD_DIM), lambda b, c: (c // NSA_KV_GROUPS, 0, 0))],
        out_specs=pl.BlockSpec((1, 1, rows, HEAD_DIM), lambda b, c: (b, c, 0, 0)),
        scratch_shapes=[pltpu.VMEM((rows + SUBLANE, HEAD_DIM), F32)],
        compiler_params=_cparams("parallel", "arbitrary"),
        name="cmp_blocks",
    )(ha, w1, w2, pos)


def _t5_bucket(dist):
    n = jnp.maximum(dist, 0)
    max_exact = REL_BUCKETS // 2
    large = max_exact + (jnp.log(jnp.maximum(n, 1).astype(F32) / max_exact)
                         / math.log(REL_MAX_DIST / max_exact) * (REL_BUCKETS - max_exact)).astype(jnp.int32)
    large = jnp.minimum(large, REL_BUCKETS - 1)
    return jnp.where(n < max_exact, n, large)


def _table_lookup(dist, tab_ref, head):
    bkt = _t5_bucket(dist)
    acc = jnp.zeros(dist.shape, F32)
    for b in range(REL_BUCKETS):
        acc = jnp.where(bkt == b, tab_ref[b, head], acc)
    return acc


def _rel_bias(dist, valid, tab_ref, head):
    rows, cols = dist.shape
    if cols % LANE or REL_MAX_DIST > LANE:
        return jnp.where(valid, _table_lookup(dist, tab_ref, head), NEG_INF)
    lane_dist = lax.broadcasted_iota(jnp.int32, (SUBLANE, LANE), 1)
    near = jnp.broadcast_to(_table_lookup(lane_dist, tab_ref, head)[0:1, :], (rows, LANE))
    far = _table_lookup(jnp.full((SUBLANE, LANE), REL_MAX_DIST, jnp.int32), tab_ref, head)[0:1, 0:1]
    parts = []
    for c in range(cols // LANE):
        d = dist[:, c * LANE:(c + 1) * LANE]
        g = jnp.take_along_axis(near, jnp.clip(d, 0, LANE - 1), axis=1)
        parts.append(jnp.where(d >= REL_MAX_DIST, far, g))
    return jnp.where(valid, jnp.concatenate(parts, axis=1), NEG_INF)


def _bias_cmp_kernel(tab_ref, o_ref, *, tb, rows, n_cmp):
    head = pl.program_id(0)
    t = pl.program_id(1) * tb + lax.broadcasted_iota(jnp.int32, (tb, rows), 0)
    n = lax.broadcasted_iota(jnp.int32, (tb, rows), 1)
    dist = t - (n * CMP_STRIDE + CMP_BLOCK - 1)
    o_ref[0] = _rel_bias(dist, (dist >= 0) & (n < n_cmp), tab_ref, head) * LOG2E


def _bias_cmp(table, seq, tb):
    rows = seq // CMP_STRIDE
    n_cmp = (seq - CMP_BLOCK) // CMP_STRIDE + 1
    return pl.pallas_call(
        functools.partial(_bias_cmp_kernel, tb=tb, rows=rows, n_cmp=n_cmp),
        out_shape=jax.ShapeDtypeStruct((NSA_HEADS, seq, rows), F32),
        grid=(NSA_HEADS, seq // tb),
        in_specs=[pl.BlockSpec(memory_space=pltpu.SMEM)],
        out_specs=pl.BlockSpec((1, tb, rows), lambda h, i: (h, i, 0)),
        compiler_params=_cparams("parallel", "arbitrary"),
        name="bias_cmp",
    )(table)


def _bias_tiles_kernel(tab_ref, o_ref, *, tq):
    head = pl.program_id(0)
    kind = pl.program_id(1)
    ij = (lax.broadcasted_iota(jnp.int32, (tq, tq), 0) - lax.broadcasted_iota(jnp.int32, (tq, tq), 1))
    dist = jnp.where(kind == KIND_DIAG, ij, jnp.where(kind == KIND_SUB, tq + ij, WINDOW + ij))
    lo = jnp.where(kind == KIND_DIAG, 0, -tq)
    hi = jnp.where(kind == KIND_CORNER, 0, tq)
    far_dist = jnp.full((SUBLANE, LANE), tq + 1, jnp.int32)
    far = _rel_bias(far_dist, far_dist > 0, tab_ref, head)[0:1, 0:1]
    o_ref[0, 0] = (_rel_bias(dist, (ij >= lo) & (ij < hi), tab_ref, head) - far) * LOG2E


def _bias_tiles(table, tq):
    assert WINDOW % tq == 0 and tq + 1 >= REL_MAX_DIST
    return pl.pallas_call(
        functools.partial(_bias_tiles_kernel, tq=tq),
        out_shape=jax.ShapeDtypeStruct((NSA_KV_GROUPS, N_BIAS_KINDS, NSA_HPG * tq, tq), F32),
        grid=(NSA_HEADS, N_BIAS_KINDS),
        in_specs=[pl.BlockSpec(memory_space=pltpu.SMEM)],
        out_specs=pl.BlockSpec((1, 1, tq, tq), lambda h, k: (h // NSA_HPG, k, h % NSA_HPG, 0)),
        compiler_params=_cparams("parallel", "arbitrary"),
        name="bias_tiles",
    )(table)


def _cmp_attn_kernel(q_ref, kc_ref, vc_ref, bias_ref, ov_ref, oc_ref, sel_ref, *, tq, n_sel, n_top):
    kc = kc_ref[0, 0]
    vc = vc_ref[0, 0]
    psum = None
    for j in range(NSA_HPG):
        hs = slice(j * HEAD_DIM, (j + 1) * HEAD_DIM)
        bias = bias_ref[j]
        s = _dot_nt(q_ref[:, hs], kc) + bias
        m = jnp.max(s, axis=-1, keepdims=True)
        e = jnp.exp2(s - m)
        p = e / jnp.sum(e, axis=-1, keepdims=True)
        p = jnp.where(bias > 0.5 * NEG_INF, p, 0.0)
        oc_ref[:, hs] = _dot(p.astype(BF16), vc)
        psum = p if psum is None else psum + p
    ov = ov_ref[...]
    imp = None
    rem = psum
    for _ in range(3):
        piece = rem.astype(BF16)
        part = _dot_nt(ov, piece)
        imp = part if imp is None else imp + part
        rem = rem - piece.astype(F32)
    t = pl.program_id(1) * tq + lax.broadcasted_iota(jnp.int32, (n_sel, tq), 1)
    cur = jnp.right_shift(t, int(math.log2(SEL_BLOCK)))
    blk = lax.broadcasted_iota(jnp.int32, (n_sel, tq), 0)
    forced = (blk == 0) | (blk == cur) | (blk == cur - 1)
    imp = jnp.where(blk > cur, -1.0, jnp.where(forced, FORCED_SCORE, imp))
    groups = [imp[g * SUBLANE:(g + 1) * SUBLANE, :] for g in range(n_sel // SUBLANE)]
    ranks = [jnp.zeros((SUBLANE, tq), jnp.int32) for _ in groups]
    sub = lax.broadcasted_iota(jnp.int32, (SUBLANE, tq), 0)
    for i in range(n_sel):
        row = imp[i:i + 1, :]
        for g, x in enumerate(groups):
            if g * SUBLANE > i:
                ahead = row >= x
            elif (g + 1) * SUBLANE - 1 < i:
                ahead = row > x
            else:
                ahead = (row > x) | ((row == x) & (sub > i - g * SUBLANE))
            ranks[g] = ranks[g] + ahead.astype(jnp.int32)
    sel_t = jnp.where(jnp.concatenate(ranks, axis=0) < n_top, 0.0, NEG_INF)
    if n_sel < LANE:
        sel_t = jnp.concatenate([sel_t, jnp.zeros((LANE - n_sel, tq), F32)], axis=0)
    sel_ref[0, 0] = sel_t.T.astype(BF16)


def _cmp_attn(hb, kcv, bias_c, ov, batch, seq, tq):
    rows = seq // CMP_STRIDE
    n_sel = seq // SEL_BLOCK
    n_top = min(N_SELECT, n_sel)
    assert n_sel <= LANE and n_sel % SUBLANE == 0
    nq = seq // tq
    g_n = NSA_KV_GROUPS
    return pl.pallas_call(
        functools.partial(_cmp_attn_kernel, tq=tq, n_sel=n_sel, n_top=n_top),
        out_shape=(jax.ShapeDtypeStruct((batch * seq, D_NSA), F32),
                   jax.ShapeDtypeStruct((batch, g_n, seq, LANE), BF16)),
        grid=(g_n, nq, batch),
        in_specs=[pl.BlockSpec((tq, GROUP_W), lambda g, i, b: (b * nq + i, HB_Q // GROUP_W + g)),
                  pl.BlockSpec((1, 1, rows, HEAD_DIM), lambda g, i, b: (b, g, 0, 0)),
                  pl.BlockSpec((1, 1, rows, HEAD_DIM), lambda g, i, b: (b, g_n + g, 0, 0)),
                  pl.BlockSpec((NSA_HPG, tq, rows), lambda g, i, b: (g, i, 0)),
                  pl.BlockSpec((n_sel, rows), lambda g, i, b: (0, 0))],
        out_specs=(pl.BlockSpec((tq, GROUP_W), lambda g, i, b: (b * nq + i, g)),
                   pl.BlockSpec((1, 1, tq, LANE), lambda g, i, b: (b, g, i, 0))),
        compiler_params=_cparams("parallel", "parallel", "arbitrary"),
        name="cmp_attn",
    )(hb, kcv, kcv, bias_c, ov)


FLASH_ROWS = 128


def _flash_init(state):
    _, _, m_ref, _, acc_ref = state
    m_ref[...] = jnp.full(m_ref.shape, -3e38, F32)
    acc_ref[...] = jnp.zeros(acc_ref.shape, F32)


def _flash_scores(qa_ref, kdim, k, s_ref, h, tq):
    hr = slice(h * tq, (h + 1) * tq)
    s_ref[hr, :] = _dot_nt(qa_ref[hr, 0:kdim], k)


def _flash_step(qa_ref, bias_ref, kind, v, nxt, state, tq):
    s_ref, p_ref, m_ref, a_ref, acc_ref = state
    tk = s_ref.shape[1]
    for h in range(NSA_HPG):
        hr = slice(h * tq, (h + 1) * tq)
        for r in range(tq // FLASH_ROWS):
            rs = slice(h * tq + r * FLASH_ROWS, h * tq + (r + 1) * FLASH_ROWS)
            s = s_ref[rs, :]
            if kind is not None:
                s = s + bias_ref[0, kind, rs, :]
            m_old = m_ref[rs, :]
            m_new = jnp.maximum(m_old, jnp.max(s, axis=-1, keepdims=True))
            p_ref[rs, :] = jnp.exp2(s - jnp.tile(m_new, (1, tk // LANE))).astype(BF16)
            a_ref[rs, :] = jnp.exp2(m_old - m_new)
            m_ref[rs, :] = m_new
        if nxt is not None:
            _flash_scores(qa_ref, nxt[0], nxt[1], s_ref, h, tq)
        acc_ref[hr, :] = jnp.tile(a_ref[hr, :], (1, 2)) * acc_ref[hr, :] + _dot(p_ref[hr, :], v)


def _selwin_kernel(q_ref, ks_ref, vs_ref, kw_ref, vw_ref, oh_ref, sel_ref, bias_ref, oc_ref, gt_ref, bz_ref,
                   o_ref, qa_ref, s_ref, p_ref, m_ref, a_ref, acc_ref, os_ref, *, tq):
    qi = pl.program_id(2)
    state = (s_ref, p_ref, m_ref, a_ref, acc_ref)
    aug = 2 * HEAD_DIM
    ones = jnp.ones((tq, HEAD_DIM), BF16)
    for j in range(NSA_HPG):
        qa_ref[j * tq:(j + 1) * tq, 0:HEAD_DIM] = q_ref[:, j * HEAD_DIM:(j + 1) * HEAD_DIM]
        qa_ref[j * tq:(j + 1) * tq, HEAD_DIM:aug] = sel_ref[0, 0]

    def rows_of(kt):
        return pl.ds(pl.multiple_of(kt * tq, tq), tq)

    def sel_keys(kt):
        return aug, jnp.concatenate([ks_ref[rows_of(kt), :], oh_ref[rows_of(kt), :]], axis=1)

    def win_keys(kt):
        return HEAD_DIM, kw_ref[rows_of(kt), :]

    def values(v_ref, kt):
        return jnp.concatenate([v_ref[rows_of(kt), :], ones], axis=1)

    _flash_init(state)
    for h in range(NSA_HPG):
        _flash_scores(qa_ref, *sel_keys(0), s_ref, h, tq)

    def far_body(kt, carry):
        _flash_step(qa_ref, bias_ref, None, values(vs_ref, kt), sel_keys(kt + 1), state, tq)
        return carry

    lax.fori_loop(0, jnp.maximum(qi - 1, 0), far_body, 0)

    def near_tiles(first):
        if not first:
            _flash_step(qa_ref, bias_ref, KIND_SUB, values(vs_ref, qi - 1), sel_keys(qi), state, tq)
        _flash_step(qa_ref, bias_ref, KIND_DIAG, values(vs_ref, qi), win_keys(jnp.maximum(qi - 1, 0)), state, tq)
        os_ref[...] = acc_ref[:, 0:HEAD_DIM] / acc_ref[:, HEAD_DIM:aug]
        _flash_init(state)
        if not first:
            _flash_step(qa_ref, bias_ref, KIND_CORNER, values(vw_ref, qi - 1), win_keys(qi), state, tq)
        _flash_step(qa_ref, bias_ref, KIND_DIAG, values(vw_ref, qi), None, state, tq)
        gate = _sigmoid(gt_ref[...])
        for j in range(NSA_HPG):
            hs = slice(j * HEAD_DIM, (j + 1) * HEAD_DIM)
            rs = slice(j * tq, (j + 1) * tq)
            o_w = acc_ref[rs, 0:HEAD_DIM] / acc_ref[rs, HEAD_DIM:aug]
            o = (gate[:, j:j + 1] * oc_ref[:, hs]
                 + gate[:, NSA_HPG + j:NSA_HPG + j + 1] * os_ref[rs, :]
                 + gate[:, 2 * NSA_HPG + j:2 * NSA_HPG + j + 1] * o_w)
            o_ref[:, hs] = (o * _silu(bz_ref[:, hs])).astype(BF16)

    pl.when(qi == 0)(functools.partial(near_tiles, True))
    pl.when(qi > 0)(functools.partial(near_tiles, False))


def _selwin_attn(hb, ha, onehot, sel, bias_t, o_c, batch, seq, tq):
    nq = seq // tq
    rows = NSA_HPG * tq
    assert tq == WINDOW
    once = pl.Buffered(1)
    kv_spec = lambda base: pl.BlockSpec((seq, HEAD_DIM), lambda g, b, i: (b, base // HEAD_DIM + g))
    row_g = lambda g, b, i: (b * nq + i, g)
    return pl.pallas_call(
        functools.partial(_selwin_kernel, tq=tq),
        out_shape=jax.ShapeDtypeStruct((batch * seq, D_NSA), BF16),
        grid=(NSA_KV_GROUPS, batch, nq),
        in_specs=[pl.BlockSpec((tq, GROUP_W), lambda g, b, i: (b * nq + i, HB_Q // GROUP_W + g)),
                  kv_spec(HB_KS), kv_spec(HB_VS), kv_spec(HB_KW), kv_spec(HB_VW),
                  pl.BlockSpec((seq, LANE), lambda g, b, i: (0, 0), pipeline_mode=once),
                  pl.BlockSpec((1, 1, tq, LANE), lambda g, b, i: (b, g, i, 0)),
                  pl.BlockSpec((1, N_BIAS_KINDS, rows, tq), lambda g, b, i: (g, 0, 0, 0), pipeline_mode=once),
                  pl.BlockSpec((tq, GROUP_W), row_g),
                  pl.BlockSpec((tq, LANE), lambda g, b, i: (b * nq + i, HA_GT // LANE + g)),
                  pl.BlockSpec((tq, GROUP_W), lambda g, b, i: (b * nq + i, HA_BZ // GROUP_W + g))],
        out_specs=pl.BlockSpec((tq, GROUP_W), row_g),
        scratch_shapes=[pltpu.VMEM((rows, 2 * HEAD_DIM), BF16),
                        pltpu.VMEM((rows, tq), F32),
                        pltpu.VMEM((rows, tq), BF16),
                        pltpu.VMEM((rows, LANE), F32),
                        pltpu.VMEM((rows, LANE), F32),
                        pltpu.VMEM((rows, 2 * HEAD_DIM), F32),
                        pltpu.VMEM((rows, HEAD_DIM), F32)],
        compiler_params=_cparams("parallel", "parallel", "arbitrary"),
        name="selwin_attn",
    )(hb, hb, hb, hb, hb, onehot, sel, bias_t, o_c, ha, ha)


def _even_tail_weights(w):
    w_g = w[:, EV_GT:EV_BZ].reshape(D_MODEL, N_BRANCH, NSA_KV_GROUPS, NSA_HPG)
    w_g = jnp.transpose(w_g, (0, 2, 1, 3)).reshape(D_MODEL, NSA_KV_GROUPS, N_BRANCH * NSA_HPG)
    w_g = jnp.pad(w_g, ((0, 0), (0, 0), (0, LANE - N_BRANCH * NSA_HPG))).reshape(D_MODEL, NSA_KV_GROUPS * LANE)
    return jnp.concatenate([w[:, EV_BZ:EV_END], w_g], axis=1).astype(BF16)


def _cast_chunk_kernel(w_ref, o_ref):
    o_ref[...] = w_ref[...].astype(BF16)


def _cast_column_chunks(w, cw):
    layers, k, n = w.shape
    return pl.pallas_call(
        _cast_chunk_kernel,
        out_shape=jax.ShapeDtypeStruct((layers, n // cw, k, cw), BF16),
        grid=(layers, n // cw),
        in_specs=[pl.BlockSpec((None, k, cw), lambda l, c: (l, 0, c))],
        out_specs=pl.BlockSpec((None, None, k, cw), lambda l, c: (l, c, 0, 0)),
        compiler_params=_cparams("parallel", "parallel"),
        name="cast_chunks",
    )(w)


def _overlap_matrix(seq):
    rows = seq // CMP_STRIDE
    n_cmp = (seq - CMP_BLOCK) // CMP_STRIDE + 1
    n_sel = seq // SEL_BLOCK
    cstart = np.arange(rows)[None, :] * CMP_STRIDE
    sstart = np.arange(n_sel)[:, None] * SEL_BLOCK
    ov = (cstart < sstart + SEL_BLOCK) & (cstart + CMP_BLOCK > sstart) & (np.arange(rows)[None, :] < n_cmp)
    return jnp.asarray(ov.astype(np.float32), dtype=BF16)


def _block_onehot(seq):
    blk = np.arange(seq)[:, None] // SEL_BLOCK
    return jnp.asarray((blk == np.arange(LANE)[None, :]).astype(np.float32), dtype=BF16)


def kernel(x, rel_bias_table, ln_g, ln_b, ev_w_in, ev_conv_w, ev_cmp_pos, ev_cmp_w1, ev_cmp_w2, ev_w_out,
           od_w_in, od_ln_g, od_ln_b, od_sgu_w, od_sgu_b, od_w_out):
    batch, seq, d = x.shape
    depth = ln_g.shape[0]
    alpha = (2 * depth) ** 0.25
    m = batch * seq
    tq = ATT_TILE
    assert d == D_MODEL and ev_w_in.shape[-1] == EV_END
    assert seq % max(CONV_ROWS, ATT_TILE, CMP_TILE, BIAS_ROWS) == 0 and m % max(ODD_ROWS, OUT_ROWS, NSA_ROWS) == 0

    xf = x.reshape(m, d)
    xb = None
    bias_c = _bias_cmp(rel_bias_table, seq, tb=BIAS_ROWS)
    bias_t = _bias_tiles(rel_bias_table, tq)
    ov = _overlap_matrix(seq)
    onehot = _block_onehot(seq)
    ev_w, ev_wo = ev_w_in.astype(BF16), ev_w_out.astype(BF16)
    od_wo = od_w_out.astype(BF16)
    od_w = _cast_column_chunks(od_w_in, PROJ_CHUNK)

    for layer in range(depth):
        i = layer // 2
        g = ln_g[layer].reshape(1, d)
        b = ln_b[layer].reshape(1, d)
        if layer % 2 == 0:
            w_tail = _even_tail_weights(ev_w[i])
            hb, ha, *cast = _nsa_proj(xf if xb is None else xb, ev_w, i, w_tail, tm=NSA_ROWS)
            xb = cast[0] if cast else xb
            y_a = _conv_proj(xb, ev_w, i, ev_conv_w[i], seq, tm=CONV_ROWS, tc=CONV_COLS)
            kcv = _compress(ha, ev_cmp_w1[i].astype(BF16), ev_cmp_w2[i].astype(BF16),
                            ev_cmp_pos[i].reshape(2, 1, CMP_BLOCK * HEAD_DIM).astype(BF16), batch, seq)
            o_c, sel = _cmp_attn(hb, kcv, bias_c, ov, batch, seq, CMP_TILE)
            y_b = _selwin_attn(hb, ha, onehot, sel, bias_t, o_c, batch, seq, tq)
            xf, xb = _outproj_ln(y_a, 0, y_b, 0, ev_wo, i, xf, g, b, alpha, OUT_ROWS, "outproj_even")
        else:
            y = _odd_mixer(xb, od_w, i, od_ln_g[i].reshape(1, d), od_ln_b[i].reshape(1, d),
                           od_sgu_w[i], od_sgu_b[i].reshape(SGU_GROUPS, SGU_CHUNK, 1), tm=ODD_ROWS)
            xf, xb = _outproj_ln(y, 0, y, 1, od_wo, i, xf, g, b, alpha, OUT_ROWS, "outproj_odd")
    return xf.reshape(batch, seq, d)
```

```python
import functools
import math

import numpy as np
import jax
import jax.numpy as jnp
from jax import lax
from jax.experimental import pallas as pl
from jax.experimental.pallas import tpu as pltpu

F32 = jnp.float32
BF16 = jnp.bfloat16

D_MODEL = 2048
CONV_WIDTH = 3
D_CONV = 1024
NSA_HEADS = 8
NSA_KV_GROUPS = 2
NSA_HPG = NSA_HEADS // NSA_KV_GROUPS
HEAD_DIM = 128
D_NSA = NSA_HEADS * HEAD_DIM
D_KV = NSA_KV_GROUPS * HEAD_DIM
CMP_BLOCK = 32
CMP_STRIDE = 16
SEL_BLOCK = 64
N_SELECT = 16
WINDOW = 512
N_BRANCH = 3
D_SGU = D_MODEL
SGU_GROUPS = 8
SGU_CHUNK = 128
SGU_GROUP_DIM = D_SGU // SGU_GROUPS
REL_BUCKETS = 32
REL_MAX_DIST = 128
LN_EPS = 1e-5
NEG_INF = -1e30
FORCED_SCORE = 1e9
GROUP_W = NSA_HPG * HEAD_DIM
LOG2E = math.log2(math.e)
Q_SCALE = HEAD_DIM ** -0.5 * LOG2E

LANE = 128
SUBLANE = 8
V7X_VMEM_BYTES = 64 * 1024 * 1024
VMEM_LIMIT = V7X_VMEM_BYTES * 7 // 8

EV_A = 0
EV_Q = 4 * D_CONV
EV_KC = EV_Q + D_NSA
EV_KS = EV_KC + 2 * D_KV
EV_GT = EV_KS + 4 * D_KV
EV_BZ = EV_GT + N_BRANCH * NSA_HEADS
EV_END = EV_BZ + D_NSA

HB_Q, HB_KS, HB_VS, HB_KW, HB_VW = 0, 1024, 1280, 1536, 1792
HA_KC, HA_BZ, HA_GT = 0, 512, 1536
HB_W = 2048
HA_W = HA_GT + NSA_KV_GROUPS * LANE
PROJ_CHUNK = 512

NSA_ROWS = 512
CONV_ROWS, CONV_COLS = 1024, 256
ODD_ROWS = 1024
OUT_ROWS = 512
BIAS_ROWS = 1024
CMP_TILE = 512
ATT_TILE = 512
ROW_SUB = 256
OUT_SUB = 128
KIND_DIAG, KIND_SUB, KIND_CORNER = 0, 1, 2
N_BIAS_KINDS = 3


def _cparams(*sem):
    return pltpu.CompilerParams(dimension_semantics=sem, vmem_limit_bytes=VMEM_LIMIT)


def _sigmoid(x):
    return 1.0 / (1.0 + jnp.exp(-x))


def _silu(x):
    return x * _sigmoid(x)


def _gelu_tanh(x):
    c = math.sqrt(2.0 / math.pi)
    return x * (0.5 * (1.0 + jnp.tanh(c * (x + 0.044715 * (x * x * x)))))


def _dot_nt(a, b):
    return lax.dot_general(a, b, (((1,), (1,)), ((), ())), preferred_element_type=F32)


def _dot(a, b):
    return jnp.dot(a, b, preferred_element_type=F32)


def _layer_norm(z, g, b):
    mu = jnp.mean(z, axis=-1, keepdims=True)
    zc = z - mu
    var = jnp.mean(zc * zc, axis=-1, keepdims=True)
    return zc * lax.rsqrt(var + LN_EPS) * g + b


def _outproj_kernel(y1_ref, y2_ref, w1_ref, w2_ref, x_ref, g_ref, b_ref, o_ref, ob_ref, *, alpha):
    for r in range(x_ref.shape[0] // OUT_SUB):
        rs = slice(r * OUT_SUB, (r + 1) * OUT_SUB)
        y = _dot(y1_ref[rs, :], w1_ref[...]) + _dot(y2_ref[rs, :], w2_ref[...])
        out = _layer_norm(alpha * x_ref[rs, :] + y, g_ref[...], b_ref[...])
        o_ref[rs, :] = out
        ob_ref[rs, :] = out.astype(BF16)


def _outproj_ln(y1, y1_col, y2, y2_col, w_out, layer, x, g, b, alpha, tm, name):
    m, d = x.shape
    kh = w_out.shape[1] // 2
    return pl.pallas_call(
        functools.partial(_outproj_kernel, alpha=alpha),
        out_shape=(jax.ShapeDtypeStruct((m, d), F32), jax.ShapeDtypeStruct((m, d), BF16)),
        grid=(m // tm,),
        in_specs=[pl.BlockSpec((tm, kh), lambda i: (i, y1_col)),
                  pl.BlockSpec((tm, kh), lambda i: (i, y2_col)),
                  pl.BlockSpec((None, kh, d), lambda i: (layer, 0, 0)),
                  pl.BlockSpec((None, kh, d), lambda i: (layer, 1, 0)),
                  pl.BlockSpec((tm, d), lambda i: (i, 0)),
                  pl.BlockSpec((1, d), lambda i: (0, 0)),
                  pl.BlockSpec((1, d), lambda i: (0, 0))],
        out_specs=(pl.BlockSpec((tm, d), lambda i: (i, 0)),
                   pl.BlockSpec((tm, d), lambda i: (i, 0))),
        compiler_params=_cparams("parallel"),
        name=name,
    )(y1, y2, w_out, w_out, x, g, b)


def _odd_kernel(x_ref, wv_ref, wu_ref, wz_ref, g_ref, b_ref, sw_ref, sb_ref, o_ref, vs_ref, mu_ref, rstd_ref,
                *, tm):
    step = pl.program_id(1)
    nchunk = D_SGU // PROJ_CHUNK
    ngrp = PROJ_CHUNK // SGU_GROUP_DIM

    def project_v():
        for r in range(tm // ROW_SUB):
            rs = slice(r * ROW_SUB, (r + 1) * ROW_SUB)
            x = x_ref[rs, :]
            tot = None
            for c in range(nchunk):
                v = _gelu_tanh(_dot(x, wv_ref[c]))
                vs_ref[c, rs, :] = v
                part = jnp.sum(v, axis=-1, keepdims=True)
                tot = part if tot is None else tot + part
            mu = tot * (1.0 / D_SGU)
            sq = None
            for c in range(nchunk):
                vc = vs_ref[c, rs, :] - mu
                part = jnp.sum(vc * vc, axis=-1, keepdims=True)
                sq = part if sq is None else sq + part
            mu_ref[rs, :] = mu
            rstd_ref[rs, :] = lax.rsqrt(sq * (1.0 / D_SGU) + LN_EPS)

    def mix_columns(chunk):
        row = lax.broadcasted_iota(jnp.int32, (SGU_CHUNK, SGU_CHUNK), 0)
        col = lax.broadcasted_iota(jnp.int32, (SGU_CHUNK, SGU_CHUNK), 1)
        nper = ROW_SUB // SGU_CHUNK
        zero = jnp.zeros((SGU_CHUNK, SGU_CHUNK), BF16)
        wbd, bias = [], []
        for g in range(ngrp):
            wg = jnp.where(col <= row, sw_ref[g], 0.0).astype(BF16)
            wbd.append(jnp.concatenate(
                [jnp.concatenate([wg if a == b else zero for b in range(nper)], axis=1) for a in range(nper)], axis=0))
            bias.append(jnp.concatenate([sb_ref[g]] * nper, axis=0))
        for r in range(tm // ROW_SUB):
            rs = slice(r * ROW_SUB, (r + 1) * ROW_SUB)
            x = x_ref[rs, :]
            u = _gelu_tanh(_dot(x, wu_ref[...]))
            z = _dot(x, wz_ref[...])
            vn = ((vs_ref[chunk, rs, :] - mu_ref[rs, :]) * rstd_ref[rs, :] * g_ref[...] + b_ref[...]).astype(BF16)
            for g in range(ngrp):
                cs = slice(g * SGU_GROUP_DIM, (g + 1) * SGU_GROUP_DIM)
                mixed = _dot(wbd[g], vn[:, cs]) + bias[g]
                o_ref[rs, cs] = (u[:, cs] * mixed * _silu(z[:, cs])).astype(BF16)

    @pl.when(step == 0)
    def _():
        project_v()
        mix_columns(0)

    @pl.when(step > 0)
    def _():
        mix_columns(step)


def _odd_mixer(xb, w_in, layer, ln_g, ln_b, sgu_w, sgu_b, tm):
    m = xb.shape[0]
    cw = PROJ_CHUNK
    nstep = D_SGU // cw
    gps = cw // SGU_GROUP_DIM
    return pl.pallas_call(
        functools.partial(_odd_kernel, tm=tm),
        out_shape=jax.ShapeDtypeStruct((m, D_SGU), BF16),
        grid=(m // tm, nstep),
        in_specs=[pl.BlockSpec((tm, D_MODEL), lambda i, s: (i, 0)),
                  pl.BlockSpec((None, nstep, D_MODEL, cw), lambda i, s: (layer, 1, 0, 0)),
                  pl.BlockSpec((None, None, D_MODEL, cw), lambda i, s: (layer, s, 0, 0)),
                  pl.BlockSpec((None, None, D_MODEL, cw), lambda i, s: (layer, 2 * nstep + s, 0, 0)),
                  pl.BlockSpec((1, cw), lambda i, s: (0, s)),
                  pl.BlockSpec((1, cw), lambda i, s: (0, s)),
                  pl.BlockSpec((gps, SGU_CHUNK, SGU_CHUNK), lambda i, s: (s, 0, 0)),
                  pl.BlockSpec((gps, SGU_CHUNK, 1), lambda i, s: (s, 0, 0))],
        out_specs=pl.BlockSpec((tm, cw), lambda i, s: (i, s)),
        scratch_shapes=[pltpu.VMEM((nstep, tm, cw), F32),
                        pltpu.VMEM((tm, 1), F32),
                        pltpu.VMEM((tm, 1), F32)],
        compiler_params=_cparams("parallel", "arbitrary"),
        name="odd_mixer",
    )(xb, w_in, w_in, w_in, ln_g, ln_b, sgu_w, sgu_b)


def _conv_proj_kernel(x_ref, wh_ref, wb_ref, wc_ref, wz_ref, cw_ref, o_ref, u_ref, *, tm, tiles_per_seq):
    i = pl.program_id(1)
    nsub = tm // ROW_SUB

    @pl.when(i % tiles_per_seq == 0)
    def _():
        u_ref[0:SUBLANE, :] = jnp.zeros((SUBLANE, u_ref.shape[1]), F32)

    for r in range(nsub):
        rs = slice(r * ROW_SUB, (r + 1) * ROW_SUB)
        x = x_ref[rs, :]
        u_ref[SUBLANE + r * ROW_SUB:SUBLANE + (r + 1) * ROW_SUB, :] = _dot(x, wc_ref[...]) * _dot(x, wh_ref[...])
    for r in range(nsub):
        rs = slice(r * ROW_SUB, (r + 1) * ROW_SUB)
        x = x_ref[rs, :]
        conv = cw_ref[CONV_WIDTH - 1:CONV_WIDTH, :] * u_ref[SUBLANE + r * ROW_SUB:SUBLANE + (r + 1) * ROW_SUB, :]
        for k in range(CONV_WIDTH - 1):
            lo = SUBLANE + r * ROW_SUB - (CONV_WIDTH - 1 - k)
            conv = conv + cw_ref[k:k + 1, :] * u_ref[lo:lo + ROW_SUB, :]
        o_ref[rs, :] = (_dot(x, wb_ref[...]) * conv * _silu(_dot(x, wz_ref[...]))).astype(BF16)
    u_ref[0:SUBLANE, :] = u_ref[tm:tm + SUBLANE, :]


def _conv_proj(xb, w, layer, conv_w, seq, tm, tc):
    m = xb.shape[0]
    nct = D_CONV // tc

    def wspec(part):
        return pl.BlockSpec((None, D_MODEL, tc), lambda j, i: (layer, 0, (EV_A + part * D_CONV) // tc + j))

    return pl.pallas_call(
        functools.partial(_conv_proj_kernel, tm=tm, tiles_per_seq=seq // tm),
        out_shape=jax.ShapeDtypeStruct((m, D_CONV), BF16),
        grid=(nct, m // tm),
        in_specs=[pl.BlockSpec((tm, D_MODEL), lambda j, i: (i, 0)),
                  wspec(0), wspec(1), wspec(2), wspec(3),
                  pl.BlockSpec((CONV_WIDTH, tc), lambda j, i: (0, j))],
        out_specs=pl.BlockSpec((tm, tc), lambda j, i: (i, j)),
        scratch_shapes=[pltpu.VMEM((tm + SUBLANE, tc), F32)],
        compiler_params=_cparams("parallel", "arbitrary"),
        name="conv_proj",
    )(xb, w, w, w, w, conv_w)


def _nsa_proj_kernel(x_ref, wq_ref, wkc_ref, wks_ref, wkw_ref, wt_ref, hb_ref, ha_ref, *xb_ref):
    x = x_ref[...].astype(BF16)
    if xb_ref:
        xb_ref[0][...] = x
    cw = PROJ_CHUNK
    for c in range(D_NSA // cw):
        hb_ref[:, c * cw:(c + 1) * cw] = (_dot(x, wq_ref[:, c * cw:(c + 1) * cw]) * Q_SCALE).astype(BF16)
    hb_ref[:, HB_KS:HB_KS + cw] = _dot(x, wks_ref[...]).astype(BF16)
    hb_ref[:, HB_KW:HB_KW + cw] = _dot(x, wkw_ref[...]).astype(BF16)
    ha_ref[:, HA_KC:HA_KC + cw] = _dot(x, wkc_ref[...])
    for lo in range(0, HA_W - HA_BZ, cw):
        hi = min(lo + cw, HA_W - HA_BZ)
        ha_ref[:, HA_BZ + lo:HA_BZ + hi] = _dot(x, wt_ref[:, lo:hi])


def _nsa_proj(x, w, layer, w_tail, tm):
    m = x.shape[0]
    once = pl.Buffered(1)
    cw = PROJ_CHUNK
    assert EV_Q % D_NSA == 0 and EV_KC % cw == 0 and EV_KS % cw == 0 and 2 * D_KV == cw
    out_shape = [jax.ShapeDtypeStruct((m, HB_W), BF16), jax.ShapeDtypeStruct((m, HA_W), F32)]
    out_specs = [pl.BlockSpec((tm, HB_W), lambda i: (i, 0)), pl.BlockSpec((tm, HA_W), lambda i: (i, 0))]
    if x.dtype != BF16:
        out_shape.append(jax.ShapeDtypeStruct((m, D_MODEL), BF16))
        out_specs.append(pl.BlockSpec((tm, D_MODEL), lambda i: (i, 0)))
    return pl.pallas_call(
        _nsa_proj_kernel,
        out_shape=tuple(out_shape),
        grid=(m // tm,),
        in_specs=[pl.BlockSpec((tm, D_MODEL), lambda i: (i, 0)),
                  pl.BlockSpec((None, D_MODEL, D_NSA), lambda i: (layer, 0, EV_Q // D_NSA), pipeline_mode=once),
                  pl.BlockSpec((None, D_MODEL, cw), lambda i: (layer, 0, EV_KC // cw), pipeline_mode=once),
                  pl.BlockSpec((None, D_MODEL, cw), lambda i: (layer, 0, EV_KS // cw), pipeline_mode=once),
                  pl.BlockSpec((None, D_MODEL, cw), lambda i: (layer, 0, EV_KS // cw + 1), pipeline_mode=once),
                  pl.BlockSpec((D_MODEL, HA_W - HA_BZ), lambda i: (0, 0), pipeline_mode=once)],
        out_specs=tuple(out_specs),
        compiler_params=_cparams("parallel"),
        name="nsa_proj",
    )(x, w, w, w, w, w_tail)


def _compress_kernel(tok_ref, w1_ref, w2_ref, pos_ref, o_ref, b_ref, *, rows):
    half = CMP_STRIDE * HEAD_DIM
    x2 = jnp.concatenate(
        [tok_ref[pl.ds(l, rows, stride=CMP_STRIDE), :] for l in range(CMP_STRIDE)], axis=1).astype(BF16)
    lo = _dot(x2, w1_ref[0, 0:half, :])
    hi = _dot(x2, w1_ref[0, half:2 * half, :])
    b_ref[0:rows, :] = hi
    b_ref[rows:rows + SUBLANE, :] = jnp.zeros((SUBLANE, HEAD_DIM), F32)
    posb = _dot(jnp.broadcast_to(pos_ref[0], (SUBLANE, 2 * half)), w1_ref[0])[0:1, :]
    pre = lo + b_ref[1:rows + 1, :] + posb
    o_ref[0, 0] = _dot(_silu(pre).astype(BF16), w2_ref[0]).astype(BF16)


def _compress(ha, w1, w2, pos, batch, seq):
    rows = seq // CMP_STRIDE
    nkv = 2 * NSA_KV_GROUPS
    kc_block = HA_KC // HEAD_DIM
    return pl.pallas_call(
        functools.partial(_compress_kernel, rows=rows),
        out_shape=jax.ShapeDtypeStruct((batch, nkv, rows, HEAD_DIM), BF16),
        grid=(batch, nkv),
        in_specs=[pl.BlockSpec((seq, HEAD_DIM), lambda b, c: (b, kc_block + c)),
                  pl.BlockSpec((1, CMP_BLOCK * HEAD_DIM, HEAD_DIM), lambda b, c: (c // NSA_KV_GROUPS, 0, 0)),
                  pl.BlockSpec((1, HEAD_DIM, HEAD_DIM), lambda b, c: (c // NSA_KV_GROUPS, 0, 0)),
                  pl.BlockSpec((1, 1, CMP_BLOCK * HEAD_DIM), lambda b, c: (c // NSA_KV_GROUPS, 0, 0))],
        out_specs=pl.BlockSpec((1, 1, rows, HEAD_DIM), lambda b, c: (b, c, 0, 0)),
        scratch_shapes=[pltpu.VMEM((rows + SUBLANE, HEAD_DIM), F32)],
        compiler_params=_cparams("parallel", "arbitrary"),
        name="cmp_blocks",
    )(ha, w1, w2, pos)


def _t5_bucket(dist):
    n = jnp.maximum(dist, 0)
    max_exact = REL_BUCKETS // 2
    large = max_exact + (jnp.log(jnp.maximum(n, 1).astype(F32) / max_exact)
                         / math.log(REL_MAX_DIST / max_exact) * (REL_BUCKETS - max_exact)).astype(jnp.int32)
    large = jnp.minimum(large, REL_BUCKETS - 1)
    return jnp.where(n < max_exact, n, large)


def _table_lookup(dist, tab_ref, head):
    bkt = _t5_bucket(dist)
    acc = jnp.zeros(dist.shape, F32)
    for b in range(REL_BUCKETS):
        acc = jnp.where(bkt == b, tab_ref[b, head], acc)
    return acc


def _rel_bias(dist, valid, tab_ref, head):
    rows, cols = dist.shape
    if cols % LANE or REL_MAX_DIST > LANE:
        return jnp.where(valid, _table_lookup(dist, tab_ref, head), NEG_INF)
    lane_dist = lax.broadcasted_iota(jnp.int32, (SUBLANE, LANE), 1)
    near = jnp.broadcast_to(_table_lookup(lane_dist, tab_ref, head)[0:1, :], (rows, LANE))
    far = _table_lookup(jnp.full((SUBLANE, LANE), REL_MAX_DIST, jnp.int32), tab_ref, head)[0:1, 0:1]
    parts = []
    for c in range(cols // LANE):
        d = dist[:, c * LANE:(c + 1) * LANE]
        g = jnp.take_along_axis(near, jnp.clip(d, 0, LANE - 1), axis=1)
        parts.append(jnp.where(d >= REL_MAX_DIST, far, g))
    return jnp.where(valid, jnp.concatenate(parts, axis=1), NEG_INF)


def _bias_cmp_kernel(tab_ref, o_ref, *, tb, rows, n_cmp):
    head = pl.program_id(0)
    t = pl.program_id(1) * tb + lax.broadcasted_iota(jnp.int32, (tb, rows), 0)
    n = lax.broadcasted_iota(jnp.int32, (tb, rows), 1)
    dist = t - (n * CMP_STRIDE + CMP_BLOCK - 1)
    o_ref[0] = _rel_bias(dist, (dist >= 0) & (n < n_cmp), tab_ref, head) * LOG2E


def _bias_cmp(table, seq, tb):
    rows = seq // CMP_STRIDE
    n_cmp = (seq - CMP_BLOCK) // CMP_STRIDE + 1
    return pl.pallas_call(
        functools.partial(_bias_cmp_kernel, tb=tb, rows=rows, n_cmp=n_cmp),
        out_shape=jax.ShapeDtypeStruct((NSA_HEADS, seq, rows), F32),
        grid=(NSA_HEADS, seq // tb),
        in_specs=[pl.BlockSpec(memory_space=pltpu.SMEM)],
        out_specs=pl.BlockSpec((1, tb, rows), lambda h, i: (h, i, 0)),
        compiler_params=_cparams("parallel", "arbitrary"),
        name="bias_cmp",
    )(table)


def _bias_tiles_kernel(tab_ref, o_ref, *, tq):
    head = pl.program_id(0)
    kind = pl.program_id(1)
    ij = (lax.broadcasted_iota(jnp.int32, (tq, tq), 0) - lax.broadcasted_iota(jnp.int32, (tq, tq), 1))
    dist = jnp.where(kind == KIND_DIAG, ij, jnp.where(kind == KIND_SUB, tq + ij, WINDOW + ij))
    lo = jnp.where(kind == KIND_DIAG, 0, -tq)
    hi = jnp.where(kind == KIND_CORNER, 0, tq)
    far_dist = jnp.full((SUBLANE, LANE), tq + 1, jnp.int32)
    far = _rel_bias(far_dist, far_dist > 0, tab_ref, head)[0:1, 0:1]
    o_ref[0, 0] = (_rel_bias(dist, (ij >= lo) & (ij < hi), tab_ref, head) - far) * LOG2E


def _bias_tiles(table, tq):
    assert WINDOW % tq == 0 and tq + 1 >= REL_MAX_DIST
    return pl.pallas_call(
        functools.partial(_bias_tiles_kernel, tq=tq),
        out_shape=jax.ShapeDtypeStruct((NSA_KV_GROUPS, N_BIAS_KINDS, NSA_HPG * tq, tq), F32),
        grid=(NSA_HEADS, N_BIAS_KINDS),
        in_specs=[pl.BlockSpec(memory_space=pltpu.SMEM)],
        out_specs=pl.BlockSpec((1, 1, tq, tq), lambda h, k: (h // NSA_HPG, k, h % NSA_HPG, 0)),
        compiler_params=_cparams("parallel", "arbitrary"),
        name="bias_tiles",
    )(table)


def _cmp_attn_kernel(q_ref, kc_ref, vc_ref, bias_ref, ov_ref, oc_ref, sel_ref, *, tq, n_sel, n_top):
    kc = kc_ref[0, 0]
    vc = vc_ref[0, 0]
    psum = None
    for j in range(NSA_HPG):
        hs = slice(j * HEAD_DIM, (j + 1) * HEAD_DIM)
        bias = bias_ref[j]
        s = _dot_nt(q_ref[:, hs], kc) + bias
        m = jnp.max(s, axis=-1, keepdims=True)
        e = jnp.exp2(s - m)
        p = e / jnp.sum(e, axis=-1, keepdims=True)
        p = jnp.where(bias > 0.5 * NEG_INF, p, 0.0)
        oc_ref[:, hs] = _dot(p.astype(BF16), vc)
        psum = p if psum is None else psum + p
    ov = ov_ref[...]
    imp = None
    rem = psum
    for _ in range(3):
        piece = rem.astype(BF16)
        part = _dot_nt(ov, piece)
        imp = part if imp is None else imp + part
        rem = rem - piece.astype(F32)
    t = pl.program_id(1) * tq + lax.broadcasted_iota(jnp.int32, (n_sel, tq), 1)
    cur = jnp.right_shift(t, int(math.log2(SEL_BLOCK)))
    blk = lax.broadcasted_iota(jnp.int32, (n_sel, tq), 0)
    forced = (blk == 0) | (blk == cur) | (blk == cur - 1)
    imp = jnp.where(blk > cur, -1.0, jnp.where(forced, FORCED_SCORE, imp))
    groups = [imp[g * SUBLANE:(g + 1) * SUBLANE, :] for g in range(n_sel // SUBLANE)]
    ranks = [jnp.zeros((SUBLANE, tq), jnp.int32) for _ in groups]
    sub = lax.broadcasted_iota(jnp.int32, (SUBLANE, tq), 0)
    for i in range(n_sel):
        row = imp[i:i + 1, :]
        for g, x in enumerate(groups):
            if g * SUBLANE > i:
                ahead = row >= x
            elif (g + 1) * SUBLANE - 1 < i:
                ahead = row > x
            else:
                ahead = (row > x) | ((row == x) & (sub > i - g * SUBLANE))
            ranks[g] = ranks[g] + ahead.astype(jnp.int32)
    sel_t = jnp.where(jnp.concatenate(ranks, axis=0) < n_top, 0.0, NEG_INF)
    if n_sel < LANE:
        sel_t = jnp.concatenate([sel_t, jnp.zeros((LANE - n_sel, tq), F32)], axis=0)
    sel_ref[0, 0] = sel_t.T.astype(BF16)


def _cmp_attn(hb, kcv, bias_c, ov, batch, seq, tq):
    rows = seq // CMP_STRIDE
    n_sel = seq // SEL_BLOCK
    n_top = min(N_SELECT, n_sel)
    assert n_sel <= LANE and n_sel % SUBLANE == 0
    nq = seq // tq
    g_n = NSA_KV_GROUPS
    return pl.pallas_call(
        functools.partial(_cmp_attn_kernel, tq=tq, n_sel=n_sel, n_top=n_top),
        out_shape=(jax.ShapeDtypeStruct((batch * seq, D_NSA), F32),
                   jax.ShapeDtypeStruct((batch, g_n, seq, LANE), BF16)),
        grid=(g_n, nq, batch),
        in_specs=[pl.BlockSpec((tq, GROUP_W), lambda g, i, b: (b * nq + i, HB_Q // GROUP_W + g)),
                  pl.BlockSpec((1, 1, rows, HEAD_DIM), lambda g, i, b: (b, g, 0, 0)),
                  pl.BlockSpec((1, 1, rows, HEAD_DIM), lambda g, i, b: (b, g_n + g, 0, 0)),
                  pl.BlockSpec((NSA_HPG, tq, rows), lambda g, i, b: (g, i, 0)),
                  pl.BlockSpec((n_sel, rows), lambda g, i, b: (0, 0))],
        out_specs=(pl.BlockSpec((tq, GROUP_W), lambda g, i, b: (b * nq + i, g)),
                   pl.BlockSpec((1, 1, tq, LANE), lambda g, i, b: (b, g, i, 0))),
        compiler_params=_cparams("parallel", "parallel", "arbitrary"),
        name="cmp_attn",
    )(hb, kcv, kcv, bias_c, ov)


FLASH_ROWS = 128


def _flash_init(state):
    _, _, m_ref, _, acc_ref = state
    m_ref[...] = jnp.full(m_ref.shape, -3e38, F32)
    acc_ref[...] = jnp.zeros(acc_ref.shape, F32)


def _flash_scores(qa_ref, kdim, k, s_ref, h, tq):
    hr = slice(h * tq, (h + 1) * tq)
    s_ref[hr, :] = _dot_nt(qa_ref[hr, 0:kdim], k)


def _flash_step(qa_ref, bias_ref, kind, v, nxt, state, tq):
    s_ref, p_ref, m_ref, a_ref, acc_ref = state
    tk = s_ref.shape[1]
    for h in range(NSA_HPG):
        hr = slice(h * tq, (h + 1) * tq)
        for r in range(tq // FLASH_ROWS):
            rs = slice(h * tq + r * FLASH_ROWS, h * tq + (r + 1) * FLASH_ROWS)
            s = s_ref[rs, :]
            if kind is not None:
                s = s + bias_ref[0, kind, rs, :]
            m_old = m_ref[rs, :]
            m_new = jnp.maximum(m_old, jnp.max(s, axis=-1, keepdims=True))
            p_ref[rs, :] = jnp.exp2(s - jnp.tile(m_new, (1, tk // LANE))).astype(BF16)
            a_ref[rs, :] = jnp.exp2(m_old - m_new)
            m_ref[rs, :] = m_new
        if nxt is not None:
            _flash_scores(qa_ref, nxt[0], nxt[1], s_ref, h, tq)
        acc_ref[hr, :] = jnp.tile(a_ref[hr, :], (1, 2)) * acc_ref[hr, :] + _dot(p_ref[hr, :], v)


def _selwin_kernel(q_ref, ks_ref, vs_ref, kw_ref, vw_ref, oh_ref, sel_ref, bias_ref, oc_ref, gt_ref, bz_ref,
                   o_ref, qa_ref, s_ref, p_ref, m_ref, a_ref, acc_ref, os_ref, *, tq):
    qi = pl.program_id(2)
    state = (s_ref, p_ref, m_ref, a_ref, acc_ref)
    aug = 2 * HEAD_DIM
    ones = jnp.ones((tq, HEAD_DIM), BF16)
    for j in range(NSA_HPG):
        qa_ref[j * tq:(j + 1) * tq, 0:HEAD_DIM] = q_ref[:, j * HEAD_DIM:(j + 1) * HEAD_DIM]
        qa_ref[j * tq:(j + 1) * tq, HEAD_DIM:aug] = sel_ref[0, 0]

    def rows_of(kt):
        return pl.ds(pl.multiple_of(kt * tq, tq), tq)

    def sel_keys(kt):
        return aug, jnp.concatenate([ks_ref[rows_of(kt), :], oh_ref[rows_of(kt), :]], axis=1)

    def win_keys(kt):
        return HEAD_DIM, kw_ref[rows_of(kt), :]

    def values(v_ref, kt):
        return jnp.concatenate([v_ref[rows_of(kt), :], ones], axis=1)

    _flash_init(state)
    for h in range(NSA_HPG):
        _flash_scores(qa_ref, *sel_keys(0), s_ref, h, tq)

    def far_body(kt, carry):
        _flash_step(qa_ref, bias_ref, None, values(vs_ref, kt), sel_keys(kt + 1), state, tq)
        return carry

    lax.fori_loop(0, jnp.maximum(qi - 1, 0), far_body, 0)

    def near_tiles(first):
        if not first:
            _flash_step(qa_ref, bias_ref, KIND_SUB, values(vs_ref, qi - 1), sel_keys(qi), state, tq)
        _flash_step(qa_ref, bias_ref, KIND_DIAG, values(vs_ref, qi), win_keys(jnp.maximum(qi - 1, 0)), state, tq)
        os_ref[...] = acc_ref[:, 0:HEAD_DIM] / acc_ref[:, HEAD_DIM:aug]
        _flash_init(state)
        if not first:
            _flash_step(qa_ref, bias_ref, KIND_CORNER, values(vw_ref, qi - 1), win_keys(qi), state, tq)
        _flash_step(qa_ref, bias_ref, KIND_DIAG, values(vw_ref, qi), None, state, tq)
        gate = _sigmoid(gt_ref[...])
        for j in range(NSA_HPG):
            hs = slice(j * HEAD_DIM, (j + 1) * HEAD_DIM)
            rs = slice(j * tq, (j + 1) * tq)
            o_w = acc_ref[rs, 0:HEAD_DIM] / acc_ref[rs, HEAD_DIM:aug]
            o = (gate[:, j:j + 1] * oc_ref[:, hs]
                 + gate[:, NSA_HPG + j:NSA_HPG + j + 1] * os_ref[rs, :]
                 + gate[:, 2 * NSA_HPG + j:2 * NSA_HPG + j + 1] * o_w)
            o_ref[:, hs] = (o * _silu(bz_ref[:, hs])).astype(BF16)

    pl.when(qi == 0)(functools.partial(near_tiles, True))
    pl.when(qi > 0)(functools.partial(near_tiles, False))


def _selwin_attn(hb, ha, onehot, sel, bias_t, o_c, batch, seq, tq):
    nq = seq // tq
    rows = NSA_HPG * tq
    assert tq == WINDOW
    once = pl.Buffered(1)
    kv_spec = lambda base: pl.BlockSpec((seq, HEAD_DIM), lambda g, b, i: (b, base // HEAD_DIM + g))
    row_g = lambda g, b, i: (b * nq + i, g)
    return pl.pallas_call(
        functools.partial(_selwin_kernel, tq=tq),
        out_shape=jax.ShapeDtypeStruct((batch * seq, D_NSA), BF16),
        grid=(NSA_KV_GROUPS, batch, nq),
        in_specs=[pl.BlockSpec((tq, GROUP_W), lambda g, b, i: (b * nq + i, HB_Q // GROUP_W + g)),
                  kv_spec(HB_KS), kv_spec(HB_VS), kv_spec(HB_KW), kv_spec(HB_VW),
                  pl.BlockSpec((seq, LANE), lambda g, b, i: (0, 0), pipeline_mode=once),
                  pl.BlockSpec((1, 1, tq, LANE), lambda g, b, i: (b, g, i, 0)),
                  pl.BlockSpec((1, N_BIAS_KINDS, rows, tq), lambda g, b, i: (g, 0, 0, 0), pipeline_mode=once),
                  pl.BlockSpec((tq, GROUP_W), row_g),
                  pl.BlockSpec((tq, LANE), lambda g, b, i: (b * nq + i, HA_GT // LANE + g)),
                  pl.BlockSpec((tq, GROUP_W), lambda g, b, i: (b * nq + i, HA_BZ // GROUP_W + g))],
        out_specs=pl.BlockSpec((tq, GROUP_W), row_g),
        scratch_shapes=[pltpu.VMEM((rows, 2 * HEAD_DIM), BF16),
                        pltpu.VMEM((rows, tq), F32),
                        pltpu.VMEM((rows, tq), BF16),
                        pltpu.VMEM((rows, LANE), F32),
                        pltpu.VMEM((rows, LANE), F32),
                        pltpu.VMEM((rows, 2 * HEAD_DIM), F32),
                        pltpu.VMEM((rows, HEAD_DIM), F32)],
        compiler_params=_cparams("parallel", "parallel", "arbitrary"),
        name="selwin_attn",
    )(hb, hb, hb, hb, hb, onehot, sel, bias_t, o_c, ha, ha)


def _even_tail_weights(w):
    w_g = w[:, EV_GT:EV_BZ].reshape(D_MODEL, N_BRANCH, NSA_KV_GROUPS, NSA_HPG)
    w_g = jnp.transpose(w_g, (0, 2, 1, 3)).reshape(D_MODEL, NSA_KV_GROUPS, N_BRANCH * NSA_HPG)
    w_g = jnp.pad(w_g, ((0, 0), (0, 0), (0, LANE - N_BRANCH * NSA_HPG))).reshape(D_MODEL, NSA_KV_GROUPS * LANE)
    return jnp.concatenate([w[:, EV_BZ:EV_END], w_g], axis=1).astype(BF16)


def _cast_chunk_kernel(w_ref, o_ref):
    o_ref[...] = w_ref[...].astype(BF16)


def _cast_column_chunks(w, cw):
    layers, k, n = w.shape
    return pl.pallas_call(
        _cast_chunk_kernel,
        out_shape=jax.ShapeDtypeStruct((layers, n // cw, k, cw), BF16),
        grid=(layers, n // cw),
        in_specs=[pl.BlockSpec((None, k, cw), lambda l, c: (l, 0, c))],
        out_specs=pl.BlockSpec((None, None, k, cw), lambda l, c: (l, c, 0, 0)),
        compiler_params=_cparams("parallel", "parallel"),
        name="cast_chunks",
    )(w)


def _overlap_matrix(seq):
    rows = seq // CMP_STRIDE
    n_cmp = (seq - CMP_BLOCK) // CMP_STRIDE + 1
    n_sel = seq // SEL_BLOCK
    cstart = np.arange(rows)[None, :] * CMP_STRIDE
    sstart = np.arange(n_sel)[:, None] * SEL_BLOCK
    ov = (cstart < sstart + SEL_BLOCK) & (cstart + CMP_BLOCK > sstart) & (np.arange(rows)[None, :] < n_cmp)
    return jnp.asarray(ov.astype(np.float32), dtype=BF16)


def _block_onehot(seq):
    blk = np.arange(seq)[:, None] // SEL_BLOCK
    return jnp.asarray((blk == np.arange(LANE)[None, :]).astype(np.float32), dtype=BF16)


def kernel(x, rel_bias_table, ln_g, ln_b, ev_w_in, ev_conv_w, ev_cmp_pos, ev_cmp_w1, ev_cmp_w2, ev_w_out,
           od_w_in, od_ln_g, od_ln_b, od_sgu_w, od_sgu_b, od_w_out):
    batch, seq, d = x.shape
    depth = ln_g.shape[0]
    alpha = (2 * depth) ** 0.25
    m = batch * seq
    tq = ATT_TILE
    assert d == D_MODEL and ev_w_in.shape[-1] == EV_END
    assert seq % max(CONV_ROWS, ATT_TILE, CMP_TILE, BIAS_ROWS) == 0 and m % max(ODD_ROWS, OUT_ROWS, NSA_ROWS) == 0

    xf = x.reshape(m, d)
    xb = None
    bias_c = _bias_cmp(rel_bias_table, seq, tb=BIAS_ROWS)
    bias_t = _bias_tiles(rel_bias_table, tq)
    ov = _overlap_matrix(seq)
    onehot = _block_onehot(seq)
    ev_w, ev_wo = ev_w_in.astype(BF16), ev_w_out.astype(BF16)
    od_wo = od_w_out.astype(BF16)
    od_w = _cast_column_chunks(od_w_in, PROJ_CHUNK)

    for layer in range(depth):
        i = layer // 2
        g = ln_g[layer].reshape(1, d)
        b = ln_b[layer].reshape(1, d)
        if layer % 2 == 0:
            w_tail = _even_tail_weights(ev_w[i])
            hb, ha, *cast = _nsa_proj(xf if xb is None else xb, ev_w, i, w_tail, tm=NSA_ROWS)
            xb = cast[0] if cast else xb
            y_a = _conv_proj(xb, ev_w, i, ev_conv_w[i], seq, tm=CONV_ROWS, tc=CONV_COLS)
            kcv = _compress(ha, ev_cmp_w1[i].astype(BF16), ev_cmp_w2[i].astype(BF16),
                            ev_cmp_pos[i].reshape(2, 1, CMP_BLOCK * HEAD_DIM).astype(BF16), batch, seq)
            o_c, sel = _cmp_attn(hb, kcv, bias_c, ov, batch, seq, CMP_TILE)
            y_b = _selwin_attn(hb, ha, onehot, sel, bias_t, o_c, batch, seq, tq)
            xf, xb = _outproj_ln(y_a, 0, y_b, 0, ev_wo, i, xf, g, b, alpha, OUT_ROWS, "outproj_even")
        else:
            y = _odd_mixer(xb, od_w, i, od_ln_g[i].reshape(1, d), od_ln_b[i].reshape(1, d),
                           od_sgu_w[i], od_sgu_b[i].reshape(SGU_GROUPS, SGU_CHUNK, 1), tm=ODD_ROWS)
            xf, xb = _outproj_ln(y, 0, y, 1, od_wo, i, xf, g, b, alpha, OUT_ROWS, "outproj_odd")
    return xf.reshape(batch, seq, d)
```

```python
import functools
import math

import numpy as np
import jax
import jax.numpy as jnp
from jax import lax
from jax.experimental import pallas as pl
from jax.experimental.pallas import tpu as pltpu

F32 = jnp.float32
BF16 = jnp.bfloat16

D_MODEL = 2048
CONV_WIDTH = 3
D_CONV = 1024
NSA_HEADS = 8
NSA_KV_GROUPS = 2
NSA_HPG = NSA_HEADS // NSA_KV_GROUPS
HEAD_DIM = 128
D_NSA = NSA_HEADS * HEAD_DIM
D_KV = NSA_KV_GROUPS * HEAD_DIM
CMP_BLOCK = 32
CMP_STRIDE = 16
SEL_BLOCK = 64
N_SELECT = 16
WINDOW = 512
N_BRANCH = 3
D_SGU = D_MODEL
SGU_GROUPS = 8
SGU_CHUNK = 128
SGU_GROUP_DIM = D_SGU // SGU_GROUPS
REL_BUCKETS = 32
REL_MAX_DIST = 128
LN_EPS = 1e-5
NEG_INF = -1e30
FORCED_SCORE = 1e9
GROUP_W = NSA_HPG * HEAD_DIM
LOG2E = math.log2(math.e)
Q_SCALE = HEAD_DIM ** -0.5 * LOG2E

LANE = 128
SUBLANE = 8
V7X_VMEM_BYTES = 64 * 1024 * 1024
VMEM_LIMIT = V7X_VMEM_BYTES * 7 // 8

EV_A = 0
EV_Q = 4 * D_CONV
EV_KC = EV_Q + D_NSA
EV_KS = EV_KC + 2 * D_KV
EV_GT = EV_KS + 4 * D_KV
EV_BZ = EV_GT + N_BRANCH * NSA_HEADS
EV_END = EV_BZ + D_NSA

HB_Q, HB_KS, HB_VS, HB_KW, HB_VW = 0, 1024, 1280, 1536, 1792
HA_KC, HA_BZ, HA_GT = 0, 512, 1536
HB_W = 2048
HA_W = HA_GT + NSA_KV_GROUPS * LANE
PROJ_CHUNK = 512

NSA_ROWS = 512
CONV_ROWS, CONV_COLS = 1024, 256
ODD_ROWS = 1024
OUT_ROWS = 512
BIAS_ROWS = 1024
CMP_TILE = 512
ATT_TILE = 512
ROW_SUB = 256
OUT_SUB = 128
KIND_DIAG, KIND_SUB, KIND_CORNER = 0, 1, 2
N_BIAS_KINDS = 3


def _cparams(*sem):
    return pltpu.CompilerParams(dimension_semantics=sem, vmem_limit_bytes=VMEM_LIMIT)


def _sigmoid(x):
    return 1.0 / (1.0 + jnp.exp(-x))


def _silu(x):
    return x * _sigmoid(x)


def _gelu_tanh(x):
    c = math.sqrt(2.0 / math.pi)
    return x * (0.5 * (1.0 + jnp.tanh(c * (x + 0.044715 * (x * x * x)))))


def _dot_nt(a, b):
    return lax.dot_general(a, b, (((1,), (1,)), ((), ())), preferred_element_type=F32)


def _dot(a, b):
    return jnp.dot(a, b, preferred_element_type=F32)


def _layer_norm(z, g, b):
    mu = jnp.mean(z, axis=-1, keepdims=True)
    zc = z - mu
    var = jnp.mean(zc * zc, axis=-1, keepdims=True)
    return zc * lax.rsqrt(var + LN_EPS) * g + b


def _outproj_kernel(y1_ref, y2_ref, w1_ref, w2_ref, x_ref, g_ref, b_ref, o_ref, ob_ref, *, alpha):
    for r in range(x_ref.shape[0] // OUT_SUB):
        rs = slice(r * OUT_SUB, (r + 1) * OUT_SUB)
        y = _dot(y1_ref[rs, :], w1_ref[...]) + _dot(y2_ref[rs, :], w2_ref[...])
        out = _layer_norm(alpha * x_ref[rs, :] + y, g_ref[...], b_ref[...])
        o_ref[rs, :] = out
        ob_ref[rs, :] = out.astype(BF16)


def _outproj_ln(y1, y1_col, y2, y2_col, w_out, layer, x, g, b, alpha, tm, name):
    m, d = x.shape
    kh = w_out.shape[1] // 2
    return pl.pallas_call(
        functools.partial(_outproj_kernel, alpha=alpha),
        out_shape=(jax.ShapeDtypeStruct((m, d), F32), jax.ShapeDtypeStruct((m, d), BF16)),
        grid=(m // tm,),
        in_specs=[pl.BlockSpec((tm, kh), lambda i: (i, y1_col)),
                  pl.BlockSpec((tm, kh), lambda i: (i, y2_col)),
                  pl.BlockSpec((None, kh, d), lambda i: (layer, 0, 0)),
                  pl.BlockSpec((None, kh, d), lambda i: (layer, 1, 0)),
                  pl.BlockSpec((tm, d), lambda i: (i, 0)),
                  pl.BlockSpec((1, d), lambda i: (0, 0)),
                  pl.BlockSpec((1, d), lambda i: (0, 0))],
        out_specs=(pl.BlockSpec((tm, d), lambda i: (i, 0)),
                   pl.BlockSpec((tm, d), lambda i: (i, 0))),
        compiler_params=_cparams("parallel"),
        name=name,
    )(y1, y2, w_out, w_out, x, g, b)


def _odd_kernel(x_ref, wv_ref, wu_ref, wz_ref, g_ref, b_ref, sw_ref, sb_ref, o_ref, vs_ref, mu_ref, rstd_ref,
                *, tm):
    step = pl.program_id(1)
    nchunk = D_SGU // PROJ_CHUNK
    ngrp = PROJ_CHUNK // SGU_GROUP_DIM

    def project_v():
        for r in range(tm // ROW_SUB):
            rs = slice(r * ROW_SUB, (r + 1) * ROW_SUB)
            x = x_ref[rs, :]
            tot = None
            for c in range(nchunk):
                v = _gelu_tanh(_dot(x, wv_ref[c]))
                vs_ref[c, rs, :] = v
                part = jnp.sum(v, axis=-1, keepdims=True)
                tot = part if tot is None else tot + part
            mu = tot * (1.0 / D_SGU)
            sq = None
            for c in range(nchunk):
                vc = vs_ref[c, rs, :] - mu
                part = jnp.sum(vc * vc, axis=-1, keepdims=True)
                sq = part if sq is None else sq + part
            mu_ref[rs, :] = mu
            rstd_ref[rs, :] = lax.rsqrt(sq * (1.0 / D_SGU) + LN_EPS)

    def mix_columns(chunk):
        row = lax.broadcasted_iota(jnp.int32, (SGU_CHUNK, SGU_CHUNK), 0)
        col = lax.broadcasted_iota(jnp.int32, (SGU_CHUNK, SGU_CHUNK), 1)
        wgs = [jnp.where(col <= row, sw_ref[g], 0.0).astype(BF16) for g in range(ngrp)]
        for r in range(tm // ROW_SUB):
            rs = slice(r * ROW_SUB, (r + 1) * ROW_SUB)
            x = x_ref[rs, :]
            u = _gelu_tanh(_dot(x, wu_ref[...]))
            z = _dot(x, wz_ref[...])
            vn = ((vs_ref[chunk, rs, :] - mu_ref[rs, :]) * rstd_ref[rs, :] * g_ref[...] + b_ref[...]).astype(BF16)
            for g in range(ngrp):
                cs = slice(g * SGU_GROUP_DIM, (g + 1) * SGU_GROUP_DIM)
                for c in range(ROW_SUB // SGU_CHUNK):
                    ls = slice(c * SGU_CHUNK, (c + 1) * SGU_CHUNK)
                    os_ = slice(r * ROW_SUB + c * SGU_CHUNK, r * ROW_SUB + (c + 1) * SGU_CHUNK)
                    mixed = _dot(wgs[g], vn[ls, cs]) + sb_ref[g]
                    o_ref[os_, cs] = (u[ls, cs] * mixed * _silu(z[ls, cs])).astype(BF16)

    @pl.when(step == 0)
    def _():
        project_v()
        mix_columns(0)

    @pl.when(step > 0)
    def _():
        mix_columns(step)


def _odd_mixer(xb, w_in, layer, ln_g, ln_b, sgu_w, sgu_b, tm):
    m = xb.shape[0]
    cw = PROJ_CHUNK
    nstep = D_SGU // cw
    gps = cw // SGU_GROUP_DIM
    return pl.pallas_call(
        functools.partial(_odd_kernel, tm=tm),
        out_shape=jax.ShapeDtypeStruct((m, D_SGU), BF16),
        grid=(m // tm, nstep),
        in_specs=[pl.BlockSpec((tm, D_MODEL), lambda i, s: (i, 0)),
                  pl.BlockSpec((None, nstep, D_MODEL, cw), lambda i, s: (layer, 1, 0, 0)),
                  pl.BlockSpec((None, None, D_MODEL, cw), lambda i, s: (layer, s, 0, 0)),
                  pl.BlockSpec((None, None, D_MODEL, cw), lambda i, s: (layer, 2 * nstep + s, 0, 0)),
                  pl.BlockSpec((1, cw), lambda i, s: (0, s)),
                  pl.BlockSpec((1, cw), lambda i, s: (0, s)),
                  pl.BlockSpec((gps, SGU_CHUNK, SGU_CHUNK), lambda i, s: (s, 0, 0)),
                  pl.BlockSpec((gps, SGU_CHUNK, 1), lambda i, s: (s, 0, 0))],
        out_specs=pl.BlockSpec((tm, cw), lambda i, s: (i, s)),
        scratch_shapes=[pltpu.VMEM((nstep, tm, cw), F32),
                        pltpu.VMEM((tm, 1), F32),
                        pltpu.VMEM((tm, 1), F32)],
        compiler_params=_cparams("parallel", "arbitrary"),
        name="odd_mixer",
    )(xb, w_in, w_in, w_in, ln_g, ln_b, sgu_w, sgu_b)


def _conv_carry_reset(u_ref, tiles_per_seq):
    @pl.when(pl.program_id(1) % tiles_per_seq == 0)
    def _():
        u_ref[0:SUBLANE, :] = jnp.zeros((SUBLANE, u_ref.shape[1]), F32)


def _conv_inputs(x_ref, wh_ref, wc_ref, u_ref, *, tm):
    for r in range(tm // ROW_SUB):
        rs = slice(r * ROW_SUB, (r + 1) * ROW_SUB)
        x = x_ref[rs, :]
        u_ref[SUBLANE + r * ROW_SUB:SUBLANE + (r + 1) * ROW_SUB, :] = _dot(x, wc_ref[...]) * _dot(x, wh_ref[...])


def _conv_outputs(x_ref, wb_ref, wz_ref, cw_ref, o_ref, u_ref, *, tm):
    for r in range(tm // ROW_SUB):
        rs = slice(r * ROW_SUB, (r + 1) * ROW_SUB)
        x = x_ref[rs, :]
        conv = cw_ref[CONV_WIDTH - 1:CONV_WIDTH, :] * u_ref[SUBLANE + r * ROW_SUB:SUBLANE + (r + 1) * ROW_SUB, :]
        for k in range(CONV_WIDTH - 1):
            lo = SUBLANE + r * ROW_SUB - (CONV_WIDTH - 1 - k)
            conv = conv + cw_ref[k:k + 1, :] * u_ref[lo:lo + ROW_SUB, :]
        o_ref[rs, :] = (_dot(x, wb_ref[...]) * conv * _silu(_dot(x, wz_ref[...]))).astype(BF16)
    u_ref[0:SUBLANE, :] = u_ref[tm:tm + SUBLANE, :]


def _nsa_proj_kernel(x_ref, wq_ref, wkc_ref, wks_ref, wkw_ref, wt_ref, hb_ref, ha_ref, *xb_ref):
    x = x_ref[...].astype(BF16)
    if xb_ref:
        xb_ref[0][...] = x
    cw = PROJ_CHUNK
    for c in range(D_NSA // cw):
        hb_ref[:, c * cw:(c + 1) * cw] = (_dot(x, wq_ref[:, c * cw:(c + 1) * cw]) * Q_SCALE).astype(BF16)
    hb_ref[:, HB_KS:HB_KS + cw] = _dot(x, wks_ref[...]).astype(BF16)
    hb_ref[:, HB_KW:HB_KW + cw] = _dot(x, wkw_ref[...]).astype(BF16)
    ha_ref[:, HA_KC:HA_KC + cw] = _dot(x, wkc_ref[...])
    for lo in range(0, HA_W - HA_BZ, cw):
        hi = min(lo + cw, HA_W - HA_BZ)
        ha_ref[:, HA_BZ + lo:HA_BZ + hi] = _dot(x, wt_ref[:, lo:hi])


def _nsa_proj(x, w, layer, w_tail, tm):
    m = x.shape[0]
    once = pl.Buffered(1)
    cw = PROJ_CHUNK
    assert EV_Q % D_NSA == 0 and EV_KC % cw == 0 and EV_KS % cw == 0 and 2 * D_KV == cw
    out_shape = [jax.ShapeDtypeStruct((m, HB_W), BF16), jax.ShapeDtypeStruct((m, HA_W), F32)]
    out_specs = [pl.BlockSpec((tm, HB_W), lambda i: (i, 0)), pl.BlockSpec((tm, HA_W), lambda i: (i, 0))]
    if x.dtype != BF16:
        out_shape.append(jax.ShapeDtypeStruct((m, D_MODEL), BF16))
        out_specs.append(pl.BlockSpec((tm, D_MODEL), lambda i: (i, 0)))
    return pl.pallas_call(
        _nsa_proj_kernel,
        out_shape=tuple(out_shape),
        grid=(m // tm,),
        in_specs=[pl.BlockSpec((tm, D_MODEL), lambda i: (i, 0)),
                  pl.BlockSpec((None, D_MODEL, D_NSA), lambda i: (layer, 0, EV_Q // D_NSA), pipeline_mode=once),
                  pl.BlockSpec((None, D_MODEL, cw), lambda i: (layer, 0, EV_KC // cw), pipeline_mode=once),
                  pl.BlockSpec((None, D_MODEL, cw), lambda i: (layer, 0, EV_KS // cw), pipeline_mode=once),
                  pl.BlockSpec((None, D_MODEL, cw), lambda i: (layer, 0, EV_KS // cw + 1), pipeline_mode=once),
                  pl.BlockSpec((D_MODEL, HA_W - HA_BZ), lambda i: (0, 0), pipeline_mode=once)],
        out_specs=tuple(out_specs),
        compiler_params=_cparams("parallel"),
        name="nsa_proj",
    )(x, w, w, w, w, w_tail)


def _compress_kernel(tok_ref, w1_ref, w2_ref, pos_ref, o_ref, b_ref, *, rows):
    half = CMP_STRIDE * HEAD_DIM
    x2 = jnp.concatenate(
        [tok_ref[pl.ds(l, rows, stride=CMP_STRIDE), :] for l in range(CMP_STRIDE)], axis=1).astype(BF16)
    lo = _dot(x2, w1_ref[0, 0:half, :])
    hi = _dot(x2, w1_ref[0, half:2 * half, :])
    b_ref[0:rows, :] = hi
    b_ref[rows:rows + SUBLANE, :] = jnp.zeros((SUBLANE, HEAD_DIM), F32)
    posb = _dot(jnp.broadcast_to(pos_ref[0], (SUBLANE, 2 * half)), w1_ref[0])[0:1, :]
    pre = lo + b_ref[1:rows + 1, :] + posb
    o_ref[0, 0] = _dot(_silu(pre).astype(BF16), w2_ref[0]).astype(BF16)


def _compress(ha, w1, w2, pos, batch, seq):
    rows = seq // CMP_STRIDE
    nkv = 2 * NSA_KV_GROUPS
    kc_block = HA_KC // HEAD_DIM
    return pl.pallas_call(
        functools.partial(_compress_kernel, rows=rows),
        out_shape=jax.ShapeDtypeStruct((batch, nkv, rows, HEAD_DIM), BF16),
        grid=(batch, nkv),
        in_specs=[pl.BlockSpec((seq, HEAD_DIM), lambda b, c: (b, kc_block + c)),
                  pl.BlockSpec((1, CMP_BLOCK * HEAD_DIM, HEAD_DIM), lambda b, c: (c // NSA_KV_GROUPS, 0, 0)),
                  pl.BlockSpec((1, HEAD_DIM, HEAD_DIM), lambda b, c: (c // NSA_KV_GROUPS, 0, 0)),
                  pl.BlockSpec((1, 1, CMP_BLOCK * HEAD_DIM), lambda b, c: (c // NSA_KV_GROUPS, 0, 0))],
        out_specs=pl.BlockSpec((1, 1, rows, HEAD_DIM), lambda b, c: (b, c, 0, 0)),
        scratch_shapes=[pltpu.VMEM((rows + SUBLANE, HEAD_DIM), F32)],
        compiler_params=_cparams("parallel", "arbitrary"),
        name="cmp_blocks",
    )(ha, w1, w2, pos)


def _t5_bucket(dist):
    n = jnp.maximum(dist, 0)
    max_exact = REL_BUCKETS // 2
    large = max_exact + (jnp.log(jnp.maximum(n, 1).astype(F32) / max_exact)
                         / math.log(REL_MAX_DIST / max_exact) * (REL_BUCKETS - max_exact)).astype(jnp.int32)
    large = jnp.minimum(large, REL_BUCKETS - 1)
    return jnp.where(n < max_exact, n, large)


def _table_lookup(dist, tab_ref, head):
    bkt = _t5_bucket(dist)
    acc = jnp.zeros(dist.shape, F32)
    for b in range(REL_BUCKETS):
        acc = jnp.where(bkt == b, tab_ref[b, head], acc)
    return acc


def _rel_bias(dist, valid, tab_ref, head):
    rows, cols = dist.shape
    if cols % LANE or REL_MAX_DIST > LANE:
        return jnp.where(valid, _table_lookup(dist, tab_ref, head), NEG_INF)
    lane_dist = lax.broadcasted_iota(jnp.int32, (SUBLANE, LANE), 1)
    near = jnp.broadcast_to(_table_lookup(lane_dist, tab_ref, head)[0:1, :], (rows, LANE))
    far = _table_lookup(jnp.full((SUBLANE, LANE), REL_MAX_DIST, jnp.int32), tab_ref, head)[0:1, 0:1]
    parts = []
    for c in range(cols // LANE):
        d = dist[:, c * LANE:(c + 1) * LANE]
        g = jnp.take_along_axis(near, jnp.clip(d, 0, LANE - 1), axis=1)
        parts.append(jnp.where(d >= REL_MAX_DIST, far, g))
    return jnp.where(valid, jnp.concatenate(parts, axis=1), NEG_INF)


def _bias_cmp_kernel(tab_ref, o_ref, *, tb, rows, n_cmp):
    head = pl.program_id(0)
    t = pl.program_id(1) * tb + lax.broadcasted_iota(jnp.int32, (tb, rows), 0)
    n = lax.broadcasted_iota(jnp.int32, (tb, rows), 1)
    dist = t - (n * CMP_STRIDE + CMP_BLOCK - 1)
    o_ref[0] = _rel_bias(dist, (dist >= 0) & (n < n_cmp), tab_ref, head) * LOG2E


def _bias_cmp(table, seq, tb):
    rows = seq // CMP_STRIDE
    n_cmp = (seq - CMP_BLOCK) // CMP_STRIDE + 1
    return pl.pallas_call(
        functools.partial(_bias_cmp_kernel, tb=tb, rows=rows, n_cmp=n_cmp),
        out_shape=jax.ShapeDtypeStruct((NSA_HEADS, seq, rows), F32),
        grid=(NSA_HEADS, seq // tb),
        in_specs=[pl.BlockSpec(memory_space=pltpu.SMEM)],
        out_specs=pl.BlockSpec((1, tb, rows), lambda h, i: (h, i, 0)),
        compiler_params=_cparams("parallel", "arbitrary"),
        name="bias_cmp",
    )(table)


def _bias_tiles_kernel(tab_ref, o_ref, *, tq):
    head = pl.program_id(0)
    kind = pl.program_id(1)
    ij = (lax.broadcasted_iota(jnp.int32, (tq, tq), 0) - lax.broadcasted_iota(jnp.int32, (tq, tq), 1))
    dist = jnp.where(kind == KIND_DIAG, ij, jnp.where(kind == KIND_SUB, tq + ij, WINDOW + ij))
    lo = jnp.where(kind == KIND_DIAG, 0, -tq)
    hi = jnp.where(kind == KIND_CORNER, 0, tq)
    far_dist = jnp.full((SUBLANE, LANE), tq + 1, jnp.int32)
    far = _rel_bias(far_dist, far_dist > 0, tab_ref, head)[0:1, 0:1]
    o_ref[0, 0] = (_rel_bias(dist, (ij >= lo) & (ij < hi), tab_ref, head) - far) * LOG2E


def _bias_tiles(table, tq):
    assert WINDOW % tq == 0 and tq + 1 >= REL_MAX_DIST
    return pl.pallas_call(
        functools.partial(_bias_tiles_kernel, tq=tq),
        out_shape=jax.ShapeDtypeStruct((NSA_KV_GROUPS, N_BIAS_KINDS, NSA_HPG * tq, tq), F32),
        grid=(NSA_HEADS, N_BIAS_KINDS),
        in_specs=[pl.BlockSpec(memory_space=pltpu.SMEM)],
        out_specs=pl.BlockSpec((1, 1, tq, tq), lambda h, k: (h // NSA_HPG, k, h % NSA_HPG, 0)),
        compiler_params=_cparams("parallel", "arbitrary"),
        name="bias_tiles",
    )(table)


def _cmp_scores(q_ref, kc_ref, bias_ref):
    kc = kc_ref[0, 0]
    return [_dot_nt(q_ref[:, j * HEAD_DIM:(j + 1) * HEAD_DIM], kc) + bias_ref[j] for j in range(NSA_HPG)]


def _cmp_attend(scores, vc_ref, bias_ref, ov_ref, oc_ref):
    vc = vc_ref[0, 0]
    psum = None
    for j, s in enumerate(scores):
        hs = slice(j * HEAD_DIM, (j + 1) * HEAD_DIM)
        bias = bias_ref[j]
        m = jnp.max(s, axis=-1, keepdims=True)
        e = jnp.exp2(s - m)
        p = e / jnp.sum(e, axis=-1, keepdims=True)
        p = jnp.where(bias > 0.5 * NEG_INF, p, 0.0)
        oc_ref[:, hs] = _dot(p.astype(BF16), vc)
        psum = p if psum is None else psum + p
    ov = ov_ref[...]
    imp = None
    rem = psum
    for _ in range(3):
        piece = rem.astype(BF16)
        part = _dot_nt(ov, piece)
        imp = part if imp is None else imp + part
        rem = rem - piece.astype(F32)
    return imp


def _cmp_select(imp, qi, sel_ref, *, tq, n_sel, n_top):
    t = qi * tq + lax.broadcasted_iota(jnp.int32, (n_sel, tq), 1)
    cur = jnp.right_shift(t, int(math.log2(SEL_BLOCK)))
    blk = lax.broadcasted_iota(jnp.int32, (n_sel, tq), 0)
    forced = (blk == 0) | (blk == cur) | (blk == cur - 1)
    imp = jnp.where(blk > cur, -1.0, jnp.where(forced, FORCED_SCORE, imp))
    groups = [imp[g * SUBLANE:(g + 1) * SUBLANE, :] for g in range(n_sel // SUBLANE)]
    ranks = [jnp.zeros((SUBLANE, tq), jnp.int32) for _ in groups]
    sub = lax.broadcasted_iota(jnp.int32, (SUBLANE, tq), 0)
    for i in range(n_sel):
        row = imp[i:i + 1, :]
        for g, x in enumerate(groups):
            if g * SUBLANE > i:
                ahead = row >= x
            elif (g + 1) * SUBLANE - 1 < i:
                ahead = row > x
            else:
                ahead = (row > x) | ((row == x) & (sub > i - g * SUBLANE))
            ranks[g] = ranks[g] + ahead.astype(jnp.int32)
    sel_t = jnp.where(jnp.concatenate(ranks, axis=0) < n_top, 0.0, NEG_INF)
    if n_sel < LANE:
        sel_t = jnp.concatenate([sel_t, jnp.zeros((LANE - n_sel, tq), F32)], axis=0)
    sel_ref[0, 0] = sel_t.T.astype(BF16)


def _conv_cmp_kernel(x_ref, wh_ref, wb_ref, wc_ref, wz_ref, cw_ref, q_ref, kc_ref, vc_ref, bias_ref, ov_ref,
                     ya_ref, oc_ref, sel_ref, u_ref, *, tm, tiles_per_seq, tq, n_sel, n_top, nq, batch):
    flat = pl.program_id(0) * pl.num_programs(1) + pl.program_id(1)
    _conv_carry_reset(u_ref, tiles_per_seq)
    scores = _cmp_scores(q_ref, kc_ref, bias_ref)
    _conv_inputs(x_ref, wh_ref, wc_ref, u_ref, tm=tm)
    imp = _cmp_attend(scores, vc_ref, bias_ref, ov_ref, oc_ref)
    _conv_outputs(x_ref, wb_ref, wz_ref, cw_ref, ya_ref, u_ref, tm=tm)
    _cmp_select(imp, (flat // batch) % nq, sel_ref, tq=tq, n_sel=n_sel, n_top=n_top)


def _conv_cmp(xb, w, layer, conv_w, hb, kcv, bias_c, ov, batch, seq, tm, tc, tq):
    m = xb.shape[0]
    nct = D_CONV // tc
    ni = m // tm
    rows = seq // CMP_STRIDE
    n_sel = seq // SEL_BLOCK
    n_top = min(N_SELECT, n_sel)
    nq = seq // tq
    g_n = NSA_KV_GROUPS
    assert n_sel <= LANE and n_sel % SUBLANE == 0 and nct * ni == g_n * nq * batch

    def wspec(part):
        return pl.BlockSpec((None, D_MODEL, tc), lambda j, i: (layer, 0, (EV_A + part * D_CONV) // tc + j))

    flat = lambda j, i: j * ni + i
    grp = lambda j, i: flat(j, i) // (nq * batch)
    qt = lambda j, i: (flat(j, i) // batch) % nq
    bat = lambda j, i: flat(j, i) % batch
    return pl.pallas_call(
        functools.partial(_conv_cmp_kernel, tm=tm, tiles_per_seq=seq // tm, tq=tq, n_sel=n_sel, n_top=n_top,
                          nq=nq, batch=batch),
        out_shape=(jax.ShapeDtypeStruct((m, D_CONV), BF16),
                   jax.ShapeDtypeStruct((m, D_NSA), F32),
                   jax.ShapeDtypeStruct((batch, g_n, seq, LANE), BF16)),
        grid=(nct, ni),
        in_specs=[pl.BlockSpec((tm, D_MODEL), lambda j, i: (i, 0)),
                  wspec(0), wspec(1), wspec(2), wspec(3),
                  pl.BlockSpec((CONV_WIDTH, tc), lambda j, i: (0, j)),
                  pl.BlockSpec((tq, GROUP_W), lambda j, i: (bat(j, i) * nq + qt(j, i), HB_Q // GROUP_W + grp(j, i))),
                  pl.BlockSpec((1, 1, rows, HEAD_DIM), lambda j, i: (bat(j, i), grp(j, i), 0, 0)),
                  pl.BlockSpec((1, 1, rows, HEAD_DIM), lambda j, i: (bat(j, i), g_n + grp(j, i), 0, 0)),
                  pl.BlockSpec((NSA_HPG, tq, rows), lambda j, i: (grp(j, i), qt(j, i), 0)),
                  pl.BlockSpec((n_sel, rows), lambda j, i: (0, 0))],
        out_specs=(pl.BlockSpec((tm, tc), lambda j, i: (i, j)),
                   pl.BlockSpec((tq, GROUP_W), lambda j, i: (bat(j, i) * nq + qt(j, i), grp(j, i))),
                   pl.BlockSpec((1, 1, tq, LANE), lambda j, i: (bat(j, i), grp(j, i), qt(j, i), 0))),
        scratch_shapes=[pltpu.VMEM((tm + SUBLANE, tc), F32)],
        compiler_params=_cparams("arbitrary", "arbitrary"),
        name="conv_cmp",
    )(xb, w, w, w, w, conv_w, hb, kcv, kcv, bias_c, ov)


FLASH_ROWS = 128


def _flash_init(state):
    _, _, m_ref, _, acc_ref = state
    m_ref[...] = jnp.full(m_ref.shape, -3e38, F32)
    acc_ref[...] = jnp.zeros(acc_ref.shape, F32)


def _flash_scores(qa_ref, kdim, k, s_ref, h, tq):
    hr = slice(h * tq, (h + 1) * tq)
    s_ref[hr, :] = _dot_nt(qa_ref[hr, 0:kdim], k)


def _flash_step(qa_ref, bias_ref, kind, v, nxt, state, tq):
    s_ref, p_ref, m_ref, a_ref, acc_ref = state
    tk = s_ref.shape[1]
    for h in range(NSA_HPG):
        hr = slice(h * tq, (h + 1) * tq)
        for r in range(tq // FLASH_ROWS):
            rs = slice(h * tq + r * FLASH_ROWS, h * tq + (r + 1) * FLASH_ROWS)
            s = s_ref[rs, :]
            if kind is not None:
                s = s + bias_ref[0, kind, rs, :]
            m_old = m_ref[rs, :]
            m_new = jnp.maximum(m_old, jnp.max(s, axis=-1, keepdims=True))
            p_ref[rs, :] = jnp.exp2(s - jnp.tile(m_new, (1, tk // LANE))).astype(BF16)
            a_ref[rs, :] = jnp.exp2(m_old - m_new)
            m_ref[rs, :] = m_new
        if nxt is not None:
            _flash_scores(qa_ref, nxt[0], nxt[1], s_ref, h, tq)
        acc_ref[hr, :] = jnp.tile(a_ref[hr, :], (1, 2)) * acc_ref[hr, :] + _dot(p_ref[hr, :], v)


def _selwin_kernel(q_ref, ks_ref, vs_ref, kw_ref, vw_ref, oh_ref, sel_ref, bias_ref, oc_ref, gt_ref, bz_ref,
                   o_ref, qa_ref, s_ref, p_ref, m_ref, a_ref, acc_ref, os_ref, *, tq):
    qi = pl.program_id(2)
    state = (s_ref, p_ref, m_ref, a_ref, acc_ref)
    aug = 2 * HEAD_DIM
    ones = jnp.ones((tq, HEAD_DIM), BF16)
    for j in range(NSA_HPG):
        qa_ref[j * tq:(j + 1) * tq, 0:HEAD_DIM] = q_ref[:, j * HEAD_DIM:(j + 1) * HEAD_DIM]
        qa_ref[j * tq:(j + 1) * tq, HEAD_DIM:aug] = sel_ref[0, 0]

    def rows_of(kt):
        return pl.ds(pl.multiple_of(kt * tq, tq), tq)

    def sel_keys(kt):
        return aug, jnp.concatenate([ks_ref[rows_of(kt), :], oh_ref[rows_of(kt), :]], axis=1)

    def win_keys(kt):
        return HEAD_DIM, kw_ref[rows_of(kt), :]

    def values(v_ref, kt):
        return jnp.concatenate([v_ref[rows_of(kt), :], ones], axis=1)

    _flash_init(state)
    for h in range(NSA_HPG):
        _flash_scores(qa_ref, *sel_keys(0), s_ref, h, tq)

    def far_body(kt, carry):
        _flash_step(qa_ref, bias_ref, None, values(vs_ref, kt), sel_keys(kt + 1), state, tq)
        return carry

    lax.fori_loop(0, jnp.maximum(qi - 1, 0), far_body, 0)

    def near_tiles(first):
        if not first:
            _flash_step(qa_ref, bias_ref, KIND_SUB, values(vs_ref, qi - 1), sel_keys(qi), state, tq)
        _flash_step(qa_ref, bias_ref, KIND_DIAG, values(vs_ref, qi), win_keys(jnp.maximum(qi - 1, 0)), state, tq)
        os_ref[...] = acc_ref[:, 0:HEAD_DIM] / acc_ref[:, HEAD_DIM:aug]
        _flash_init(state)
        if not first:
            _flash_step(qa_ref, bias_ref, KIND_CORNER, values(vw_ref, qi - 1), win_keys(qi), state, tq)
        _flash_step(qa_ref, bias_ref, KIND_DIAG, values(vw_ref, qi), None, state, tq)
        gate = _sigmoid(gt_ref[...])
        for j in range(NSA_HPG):
            hs = slice(j * HEAD_DIM, (j + 1) * HEAD_DIM)
            rs = slice(j * tq, (j + 1) * tq)
            o_w = acc_ref[rs, 0:HEAD_DIM] / acc_ref[rs, HEAD_DIM:aug]
            o = (gate[:, j:j + 1] * oc_ref[:, hs]
                 + gate[:, NSA_HPG + j:NSA_HPG + j + 1] * os_ref[rs, :]
                 + gate[:, 2 * NSA_HPG + j:2 * NSA_HPG + j + 1] * o_w)
            o_ref[:, hs] = (o * _silu(bz_ref[:, hs])).astype(BF16)

    pl.when(qi == 0)(functools.partial(near_tiles, True))
    pl.when(qi > 0)(functools.partial(near_tiles, False))


def _selwin_attn(hb, ha, onehot, sel, bias_t, o_c, batch, seq, tq):
    nq = seq // tq
    rows = NSA_HPG * tq
    assert tq == WINDOW
    once = pl.Buffered(1)
    kv_spec = lambda base: pl.BlockSpec((seq, HEAD_DIM), lambda g, b, i: (b, base // HEAD_DIM + g))
    row_g = lambda g, b, i: (b * nq + i, g)
    return pl.pallas_call(
        functools.partial(_selwin_kernel, tq=tq),
        out_shape=jax.ShapeDtypeStruct((batch * seq, D_NSA), BF16),
        grid=(NSA_KV_GROUPS, batch, nq),
        in_specs=[pl.BlockSpec((tq, GROUP_W), lambda g, b, i: (b * nq + i, HB_Q // GROUP_W + g)),
                  kv_spec(HB_KS), kv_spec(HB_VS), kv_spec(HB_KW), kv_spec(HB_VW),
                  pl.BlockSpec((seq, LANE), lambda g, b, i: (0, 0), pipeline_mode=once),
                  pl.BlockSpec((1, 1, tq, LANE), lambda g, b, i: (b, g, i, 0)),
                  pl.BlockSpec((1, N_BIAS_KINDS, rows, tq), lambda g, b, i: (g, 0, 0, 0), pipeline_mode=once),
                  pl.BlockSpec((tq, GROUP_W), row_g),
                  pl.BlockSpec((tq, LANE), lambda g, b, i: (b * nq + i, HA_GT // LANE + g)),
                  pl.BlockSpec((tq, GROUP_W), lambda g, b, i: (b * nq + i, HA_BZ // GROUP_W + g))],
        out_specs=pl.BlockSpec((tq, GROUP_W), row_g),
        scratch_shapes=[pltpu.VMEM((rows, 2 * HEAD_DIM), BF16),
                        pltpu.VMEM((rows, tq), F32),
                        pltpu.VMEM((rows, tq), BF16),
                        pltpu.VMEM((rows, LANE), F32),
                        pltpu.VMEM((rows, LANE), F32),
                        pltpu.VMEM((rows, 2 * HEAD_DIM), F32),
                        pltpu.VMEM((rows, HEAD_DIM), F32)],
        compiler_params=_cparams("parallel", "parallel", "arbitrary"),
        name="selwin_attn",
    )(hb, hb, hb, hb, hb, onehot, sel, bias_t, o_c, ha, ha)


def _even_tail_weights(w):
    w_g = w[:, EV_GT:EV_BZ].reshape(D_MODEL, N_BRANCH, NSA_KV_GROUPS, NSA_HPG)
    w_g = jnp.transpose(w_g, (0, 2, 1, 3)).reshape(D_MODEL, NSA_KV_GROUPS, N_BRANCH * NSA_HPG)
    w_g = jnp.pad(w_g, ((0, 0), (0, 0), (0, LANE - N_BRANCH * NSA_HPG))).reshape(D_MODEL, NSA_KV_GROUPS * LANE)
    return jnp.concatenate([w[:, EV_BZ:EV_END], w_g], axis=1).astype(BF16)


def _cast_chunk_kernel(w_ref, o_ref):
    o_ref[...] = w_ref[...].astype(BF16)


def _cast_column_chunks(w, cw):
    layers, k, n = w.shape
    return pl.pallas_call(
        _cast_chunk_kernel,
        out_shape=jax.ShapeDtypeStruct((layers, n // cw, k, cw), BF16),
        grid=(layers, n // cw),
        in_specs=[pl.BlockSpec((None, k, cw), lambda l, c: (l, 0, c))],
        out_specs=pl.BlockSpec((None, None, k, cw), lambda l, c: (l, c, 0, 0)),
        compiler_params=_cparams("parallel", "parallel"),
        name="cast_chunks",
    )(w)


def _overlap_matrix(seq):
    rows = seq // CMP_STRIDE
    n_cmp = (seq - CMP_BLOCK) // CMP_STRIDE + 1
    n_sel = seq // SEL_BLOCK
    cstart = np.arange(rows)[None, :] * CMP_STRIDE
    sstart = np.arange(n_sel)[:, None] * SEL_BLOCK
    ov = (cstart < sstart + SEL_BLOCK) & (cstart + CMP_BLOCK > sstart) & (np.arange(rows)[None, :] < n_cmp)
    return jnp.asarray(ov.astype(np.float32), dtype=BF16)


def _block_onehot(seq):
    blk = np.arange(seq)[:, None] // SEL_BLOCK
    return jnp.asarray((blk == np.arange(LANE)[None, :]).astype(np.float32), dtype=BF16)


def kernel(x, rel_bias_table, ln_g, ln_b, ev_w_in, ev_conv_w, ev_cmp_pos, ev_cmp_w1, ev_cmp_w2, ev_w_out,
           od_w_in, od_ln_g, od_ln_b, od_sgu_w, od_sgu_b, od_w_out):
    batch, seq, d = x.shape
    depth = ln_g.shape[0]
    alpha = (2 * depth) ** 0.25
    m = batch * seq
    tq = ATT_TILE
    assert d == D_MODEL and ev_w_in.shape[-1] == EV_END
    assert seq % max(CONV_ROWS, ATT_TILE, CMP_TILE, BIAS_ROWS) == 0 and m % max(ODD_ROWS, OUT_ROWS, NSA_ROWS) == 0

    xf = x.reshape(m, d)
    xb = None
    bias_c = _bias_cmp(rel_bias_table, seq, tb=BIAS_ROWS)
    bias_t = _bias_tiles(rel_bias_table, tq)
    ov = _overlap_matrix(seq)
    onehot = _block_onehot(seq)
    ev_w, ev_wo = ev_w_in.astype(BF16), ev_w_out.astype(BF16)
    od_wo = od_w_out.astype(BF16)
    od_w = _cast_column_chunks(od_w_in, PROJ_CHUNK)

    for layer in range(depth):
        i = layer // 2
        g = ln_g[layer].reshape(1, d)
        b = ln_b[layer].reshape(1, d)
        if layer % 2 == 0:
            w_tail = _even_tail_weights(ev_w[i])
            hb, ha, *cast = _nsa_proj(xf if xb is None else xb, ev_w, i, w_tail, tm=NSA_ROWS)
            xb = cast[0] if cast else xb
            kcv = _compress(ha, ev_cmp_w1[i].astype(BF16), ev_cmp_w2[i].astype(BF16),
                            ev_cmp_pos[i].reshape(2, 1, CMP_BLOCK * HEAD_DIM).astype(BF16), batch, seq)
            y_a, o_c, sel = _conv_cmp(xb, ev_w, i, ev_conv_w[i], hb, kcv, bias_c, ov, batch, seq,
                                      tm=CONV_ROWS, tc=CONV_COLS, tq=CMP_TILE)
            y_b = _selwin_attn(hb, ha, onehot, sel, bias_t, o_c, batch, seq, tq)
            xf, xb = _outproj_ln(y_a, 0, y_b, 0, ev_wo, i, xf, g, b, alpha, OUT_ROWS, "outproj_even")
        else:
            y = _odd_mixer(xb, od_w, i, od_ln_g[i].reshape(1, d), od_ln_b[i].reshape(1, d),
                           od_sgu_w[i], od_sgu_b[i].reshape(SGU_GROUPS, SGU_CHUNK, 1), tm=ODD_ROWS)
            xf, xb = _outproj_ln(y, 0, y, 1, od_wo, i, xf, g, b, alpha, OUT_ROWS, "outproj_odd")
    return xf.reshape(batch, seq, d)
```

```python
import functools
import math

import numpy as np
import jax
import jax.numpy as jnp
from jax import lax
from jax.experimental import pallas as pl
from jax.experimental.pallas import tpu as pltpu

F32 = jnp.float32
BF16 = jnp.bfloat16

D_MODEL = 2048
CONV_WIDTH = 3
D_CONV = 1024
NSA_HEADS = 8
NSA_KV_GROUPS = 2
NSA_HPG = NSA_HEADS // NSA_KV_GROUPS
HEAD_DIM = 128
D_NSA = NSA_HEADS * HEAD_DIM
D_KV = NSA_KV_GROUPS * HEAD_DIM
CMP_BLOCK = 32
CMP_STRIDE = 16
SEL_BLOCK = 64
N_SELECT = 16
WINDOW = 512
N_BRANCH = 3
D_SGU = D_MODEL
SGU_GROUPS = 8
SGU_CHUNK = 128
SGU_GROUP_DIM = D_SGU // SGU_GROUPS
REL_BUCKETS = 32
REL_MAX_DIST = 128
LN_EPS = 1e-5
NEG_INF = -1e30
FORCED_SCORE = 1e9
GROUP_W = NSA_HPG * HEAD_DIM
LOG2E = math.log2(math.e)
Q_SCALE = HEAD_DIM ** -0.5 * LOG2E

LANE = 128
SUBLANE = 8
V7X_VMEM_BYTES = 64 * 1024 * 1024
VMEM_LIMIT = V7X_VMEM_BYTES * 7 // 8

EV_A = 0
EV_Q = 4 * D_CONV
EV_KC = EV_Q + D_NSA
EV_KS = EV_KC + 2 * D_KV
EV_GT = EV_KS + 4 * D_KV
EV_BZ = EV_GT + N_BRANCH * NSA_HEADS
EV_END = EV_BZ + D_NSA

HB_Q, HB_KS, HB_VS, HB_KW, HB_VW = 0, 1024, 1280, 1536, 1792
HA_KC, HA_BZ, HA_GT = 0, 512, 1536
HB_W = 2048
HA_W = HA_GT + NSA_KV_GROUPS * LANE
PROJ_CHUNK = 512

NSA_ROWS = 512
CONV_ROWS, CONV_COLS = 1024, 256
ODD_ROWS = 1024
OUT_ROWS = 512
BIAS_ROWS = 1024
CMP_TILE = 512
ATT_TILE = 512
ROW_SUB = 256
OUT_SUB = 128
KIND_DIAG, KIND_SUB, KIND_CORNER = 0, 1, 2
N_BIAS_KINDS = 3


def _cparams(*sem):
    return pltpu.CompilerParams(dimension_semantics=sem, vmem_limit_bytes=VMEM_LIMIT)


def _sigmoid(x):
    return 1.0 / (1.0 + jnp.exp(-x))


def _silu(x):
    return x * _sigmoid(x)


def _gelu_tanh(x):
    c = math.sqrt(2.0 / math.pi)
    return x * (0.5 * (1.0 + jnp.tanh(c * (x + 0.044715 * (x * x * x)))))


def _dot_nt(a, b):
    return lax.dot_general(a, b, (((1,), (1,)), ((), ())), preferred_element_type=F32)


def _dot(a, b):
    return jnp.dot(a, b, preferred_element_type=F32)


def _layer_norm(z, g, b):
    mu = jnp.mean(z, axis=-1, keepdims=True)
    zc = z - mu
    var = jnp.mean(zc * zc, axis=-1, keepdims=True)
    return zc * lax.rsqrt(var + LN_EPS) * g + b


def _ordering_zero(v):
    top = jnp.max(v[:, 0:LANE], axis=0, keepdims=True)
    bits = lax.bitcast_convert_type(jnp.broadcast_to(top, (2 * SUBLANE, LANE)), jnp.uint32)
    sixteen = jnp.full(bits.shape, 16, jnp.uint32)
    zero = lax.shift_right_logical(lax.shift_right_logical(bits, sixteen), sixteen)
    return zero.astype(F32).astype(BF16)


def _outproj_kernel(y1_ref, y2_ref, w1_ref, w2_ref, x_ref, g_ref, b_ref, o_ref, ob_ref, *, alpha):
    zeros = []
    for r in range(x_ref.shape[0] // OUT_SUB):
        rs = slice(r * OUT_SUB, (r + 1) * OUT_SUB)
        y1 = y1_ref[rs, :]
        if r >= 2:
            first = y1[0:2 * SUBLANE, 0:LANE] + zeros[r - 2]
            y1 = jnp.concatenate(
                [jnp.concatenate([first, y1[0:2 * SUBLANE, LANE:]], axis=1), y1[2 * SUBLANE:, :]], axis=0)
        y = _dot(y1, w1_ref[...]) + _dot(y2_ref[rs, :], w2_ref[...])
        out = _layer_norm(alpha * x_ref[rs, :] + y, g_ref[...], b_ref[...])
        zeros.append(_ordering_zero(out))
        o_ref[rs, :] = out
        ob_ref[rs, :] = out.astype(BF16)


def _outproj_ln(y1, y1_col, y2, y2_col, w_out, layer, x, g, b, alpha, tm, name):
    m, d = x.shape
    kh = w_out.shape[1] // 2
    return pl.pallas_call(
        functools.partial(_outproj_kernel, alpha=alpha),
        out_shape=(jax.ShapeDtypeStruct((m, d), F32), jax.ShapeDtypeStruct((m, d), BF16)),
        grid=(m // tm,),
        in_specs=[pl.BlockSpec((tm, kh), lambda i: (i, y1_col)),
                  pl.BlockSpec((tm, kh), lambda i: (i, y2_col)),
                  pl.BlockSpec((None, kh, d), lambda i: (layer, 0, 0)),
                  pl.BlockSpec((None, kh, d), lambda i: (layer, 1, 0)),
                  pl.BlockSpec((tm, d), lambda i: (i, 0)),
                  pl.BlockSpec((1, d), lambda i: (0, 0)),
                  pl.BlockSpec((1, d), lambda i: (0, 0))],
        out_specs=(pl.BlockSpec((tm, d), lambda i: (i, 0)),
                   pl.BlockSpec((tm, d), lambda i: (i, 0))),
        compiler_params=_cparams("parallel"),
        name=name,
    )(y1, y2, w_out, w_out, x, g, b)


def _odd_kernel(x_ref, wv_ref, wu_ref, wz_ref, g_ref, b_ref, sw_ref, sb_ref, o_ref, vs_ref, mu_ref, rstd_ref,
                *, tm):
    step = pl.program_id(1)
    nchunk = D_SGU // PROJ_CHUNK
    ngrp = PROJ_CHUNK // SGU_GROUP_DIM

    def project_v():
        for r in range(tm // ROW_SUB):
            rs = slice(r * ROW_SUB, (r + 1) * ROW_SUB)
            x = x_ref[rs, :]
            tot = None
            for c in range(nchunk):
                v = _gelu_tanh(_dot(x, wv_ref[c]))
                vs_ref[c, rs, :] = v
                part = jnp.sum(v, axis=-1, keepdims=True)
                tot = part if tot is None else tot + part
            mu = tot * (1.0 / D_SGU)
            sq = None
            for c in range(nchunk):
                vc = vs_ref[c, rs, :] - mu
                part = jnp.sum(vc * vc, axis=-1, keepdims=True)
                sq = part if sq is None else sq + part
            mu_ref[rs, :] = mu
            rstd_ref[rs, :] = lax.rsqrt(sq * (1.0 / D_SGU) + LN_EPS)

    def mix_columns(chunk):
        row = lax.broadcasted_iota(jnp.int32, (SGU_CHUNK, SGU_CHUNK), 0)
        col = lax.broadcasted_iota(jnp.int32, (SGU_CHUNK, SGU_CHUNK), 1)
        wgs = [jnp.where(col <= row, sw_ref[g], 0.0).astype(BF16) for g in range(ngrp)]
        for r in range(tm // ROW_SUB):
            rs = slice(r * ROW_SUB, (r + 1) * ROW_SUB)
            x = x_ref[rs, :]
            u = _gelu_tanh(_dot(x, wu_ref[...]))
            z = _dot(x, wz_ref[...])
            vn = ((vs_ref[chunk, rs, :] - mu_ref[rs, :]) * rstd_ref[rs, :] * g_ref[...] + b_ref[...]).astype(BF16)
            for g in range(ngrp):
                cs = slice(g * SGU_GROUP_DIM, (g + 1) * SGU_GROUP_DIM)
                for c in range(ROW_SUB // SGU_CHUNK):
                    ls = slice(c * SGU_CHUNK, (c + 1) * SGU_CHUNK)
                    os_ = slice(r * ROW_SUB + c * SGU_CHUNK, r * ROW_SUB + (c + 1) * SGU_CHUNK)
                    mixed = _dot(wgs[g], vn[ls, cs]) + sb_ref[g]
                    o_ref[os_, cs] = (u[ls, cs] * mixed * _silu(z[ls, cs])).astype(BF16)

    @pl.when(step == 0)
    def _():
        project_v()
        mix_columns(0)

    @pl.when(step > 0)
    def _():
        mix_columns(step)


def _odd_mixer(xb, w_in, layer, ln_g, ln_b, sgu_w, sgu_b, tm):
    m = xb.shape[0]
    cw = PROJ_CHUNK
    nstep = D_SGU // cw
    gps = cw // SGU_GROUP_DIM
    return pl.pallas_call(
        functools.partial(_odd_kernel, tm=tm),
        out_shape=jax.ShapeDtypeStruct((m, D_SGU), BF16),
        grid=(m // tm, nstep),
        in_specs=[pl.BlockSpec((tm, D_MODEL), lambda i, s: (i, 0)),
                  pl.BlockSpec((None, nstep, D_MODEL, cw), lambda i, s: (layer, 1, 0, 0)),
                  pl.BlockSpec((None, None, D_MODEL, cw), lambda i, s: (layer, s, 0, 0)),
                  pl.BlockSpec((None, None, D_MODEL, cw), lambda i, s: (layer, 2 * nstep + s, 0, 0)),
                  pl.BlockSpec((1, cw), lambda i, s: (0, s)),
                  pl.BlockSpec((1, cw), lambda i, s: (0, s)),
                  pl.BlockSpec((gps, SGU_CHUNK, SGU_CHUNK), lambda i, s: (s, 0, 0)),
                  pl.BlockSpec((gps, SGU_CHUNK, 1), lambda i, s: (s, 0, 0))],
        out_specs=pl.BlockSpec((tm, cw), lambda i, s: (i, s)),
        scratch_shapes=[pltpu.VMEM((nstep, tm, cw), F32),
                        pltpu.VMEM((tm, 1), F32),
                        pltpu.VMEM((tm, 1), F32)],
        compiler_params=_cparams("parallel", "arbitrary"),
        name="odd_mixer",
    )(xb, w_in, w_in, w_in, ln_g, ln_b, sgu_w, sgu_b)


def _conv_carry_reset(u_ref, tiles_per_seq):
    @pl.when(pl.program_id(1) % tiles_per_seq == 0)
    def _():
        u_ref[0:SUBLANE, :] = jnp.zeros((SUBLANE, u_ref.shape[1]), F32)


def _conv_inputs(x_ref, wh_ref, wc_ref, u_ref, *, tm):
    for r in range(tm // ROW_SUB):
        rs = slice(r * ROW_SUB, (r + 1) * ROW_SUB)
        x = x_ref[rs, :]
        u_ref[SUBLANE + r * ROW_SUB:SUBLANE + (r + 1) * ROW_SUB, :] = _dot(x, wc_ref[...]) * _dot(x, wh_ref[...])


def _conv_outputs(x_ref, wb_ref, wz_ref, cw_ref, o_ref, u_ref, *, tm):
    for r in range(tm // ROW_SUB):
        rs = slice(r * ROW_SUB, (r + 1) * ROW_SUB)
        x = x_ref[rs, :]
        conv = cw_ref[CONV_WIDTH - 1:CONV_WIDTH, :] * u_ref[SUBLANE + r * ROW_SUB:SUBLANE + (r + 1) * ROW_SUB, :]
        for k in range(CONV_WIDTH - 1):
            lo = SUBLANE + r * ROW_SUB - (CONV_WIDTH - 1 - k)
            conv = conv + cw_ref[k:k + 1, :] * u_ref[lo:lo + ROW_SUB, :]
        o_ref[rs, :] = (_dot(x, wb_ref[...]) * conv * _silu(_dot(x, wz_ref[...]))).astype(BF16)
    u_ref[0:SUBLANE, :] = u_ref[tm:tm + SUBLANE, :]


def _nsa_proj_kernel(x_ref, wq_ref, wkc_ref, wks_ref, wkw_ref, wt_ref, hb_ref, ha_ref, *xb_ref):
    x = x_ref[...].astype(BF16)
    if xb_ref:
        xb_ref[0][...] = x
    cw = PROJ_CHUNK
    for c in range(D_NSA // cw):
        hb_ref[:, c * cw:(c + 1) * cw] = (_dot(x, wq_ref[:, c * cw:(c + 1) * cw]) * Q_SCALE).astype(BF16)
    hb_ref[:, HB_KS:HB_KS + cw] = _dot(x, wks_ref[...]).astype(BF16)
    hb_ref[:, HB_KW:HB_KW + cw] = _dot(x, wkw_ref[...]).astype(BF16)
    ha_ref[:, HA_KC:HA_KC + cw] = _dot(x, wkc_ref[...])
    for lo in range(0, HA_W - HA_BZ, cw):
        hi = min(lo + cw, HA_W - HA_BZ)
        ha_ref[:, HA_BZ + lo:HA_BZ + hi] = _dot(x, wt_ref[:, lo:hi])


def _nsa_proj(x, w, layer, w_tail, tm):
    m = x.shape[0]
    once = pl.Buffered(1)
    cw = PROJ_CHUNK
    assert EV_Q % D_NSA == 0 and EV_KC % cw == 0 and EV_KS % cw == 0 and 2 * D_KV == cw
    out_shape = [jax.ShapeDtypeStruct((m, HB_W), BF16), jax.ShapeDtypeStruct((m, HA_W), F32)]
    out_specs = [pl.BlockSpec((tm, HB_W), lambda i: (i, 0)), pl.BlockSpec((tm, HA_W), lambda i: (i, 0))]
    if x.dtype != BF16:
        out_shape.append(jax.ShapeDtypeStruct((m, D_MODEL), BF16))
        out_specs.append(pl.BlockSpec((tm, D_MODEL), lambda i: (i, 0)))
    return pl.pallas_call(
        _nsa_proj_kernel,
        out_shape=tuple(out_shape),
        grid=(m // tm,),
        in_specs=[pl.BlockSpec((tm, D_MODEL), lambda i: (i, 0)),
                  pl.BlockSpec((None, D_MODEL, D_NSA), lambda i: (layer, 0, EV_Q // D_NSA), pipeline_mode=once),
                  pl.BlockSpec((None, D_MODEL, cw), lambda i: (layer, 0, EV_KC // cw), pipeline_mode=once),
                  pl.BlockSpec((None, D_MODEL, cw), lambda i: (layer, 0, EV_KS // cw), pipeline_mode=once),
                  pl.BlockSpec((None, D_MODEL, cw), lambda i: (layer, 0, EV_KS // cw + 1), pipeline_mode=once),
                  pl.BlockSpec((D_MODEL, HA_W - HA_BZ), lambda i: (0, 0), pipeline_mode=once)],
        out_specs=tuple(out_specs),
        compiler_params=_cparams("parallel"),
        name="nsa_proj",
    )(x, w, w, w, w, w_tail)


def _compress_kernel(tok_ref, w1_ref, w2_ref, pos_ref, o_ref, b_ref, *, rows):
    half = CMP_STRIDE * HEAD_DIM
    x2 = jnp.concatenate(
        [tok_ref[pl.ds(l, rows, stride=CMP_STRIDE), :] for l in range(CMP_STRIDE)], axis=1).astype(BF16)
    lo = _dot(x2, w1_ref[0, 0:half, :])
    hi = _dot(x2, w1_ref[0, half:2 * half, :])
    b_ref[0:rows, :] = hi
    b_ref[rows:rows + SUBLANE, :] = jnp.zeros((SUBLANE, HEAD_DIM), F32)
    posb = _dot(jnp.broadcast_to(pos_ref[0], (SUBLANE, 2 * half)), w1_ref[0])[0:1, :]
    pre = lo + b_ref[1:rows + 1, :] + posb
    o_ref[0, 0] = _dot(_silu(pre).astype(BF16), w2_ref[0]).astype(BF16)


def _compress(ha, w1, w2, pos, batch, seq):
    rows = seq // CMP_STRIDE
    nkv = 2 * NSA_KV_GROUPS
    kc_block = HA_KC // HEAD_DIM
    return pl.pallas_call(
        functools.partial(_compress_kernel, rows=rows),
        out_shape=jax.ShapeDtypeStruct((batch, nkv, rows, HEAD_DIM), BF16),
        grid=(batch, nkv),
        in_specs=[pl.BlockSpec((seq, HEAD_DIM), lambda b, c: (b, kc_block + c)),
                  pl.BlockSpec((1, CMP_BLOCK * HEAD_DIM, HEAD_DIM), lambda b, c: (c // NSA_KV_GROUPS, 0, 0)),
                  pl.BlockSpec((1, HEAD_DIM, HEAD_DIM), lambda b, c: (c // NSA_KV_GROUPS, 0, 0)),
                  pl.BlockSpec((1, 1, CMP_BLOCK * HEAD_DIM), lambda b, c: (c // NSA_KV_GROUPS, 0, 0))],
        out_specs=pl.BlockSpec((1, 1, rows, HEAD_DIM), lambda b, c: (b, c, 0, 0)),
        scratch_shapes=[pltpu.VMEM((rows + SUBLANE, HEAD_DIM), F32)],
        compiler_params=_cparams("parallel", "arbitrary"),
        name="cmp_blocks",
    )(ha, w1, w2, pos)


def _t5_bucket(dist):
    n = jnp.maximum(dist, 0)
    max_exact = REL_BUCKETS // 2
    large = max_exact + (jnp.log(jnp.maximum(n, 1).astype(F32) / max_exact)
                         / math.log(REL_MAX_DIST / max_exact) * (REL_BUCKETS - max_exact)).astype(jnp.int32)
    large = jnp.minimum(large, REL_BUCKETS - 1)
    return jnp.where(n < max_exact, n, large)


def _table_lookup(dist, tab_ref, head):
    bkt = _t5_bucket(dist)
    acc = jnp.zeros(dist.shape, F32)
    for b in range(REL_BUCKETS):
        acc = jnp.where(bkt == b, tab_ref[b, head], acc)
    return acc


def _rel_bias(dist, valid, tab_ref, head):
    rows, cols = dist.shape
    if cols % LANE or REL_MAX_DIST > LANE:
        return jnp.where(valid, _table_lookup(dist, tab_ref, head), NEG_INF)
    lane_dist = lax.broadcasted_iota(jnp.int32, (SUBLANE, LANE), 1)
    near = jnp.broadcast_to(_table_lookup(lane_dist, tab_ref, head)[0:1, :], (rows, LANE))
    far = _table_lookup(jnp.full((SUBLANE, LANE), REL_MAX_DIST, jnp.int32), tab_ref, head)[0:1, 0:1]
    parts = []
    for c in range(cols // LANE):
        d = dist[:, c * LANE:(c + 1) * LANE]
        g = jnp.take_along_axis(near, jnp.clip(d, 0, LANE - 1), axis=1)
        parts.append(jnp.where(d >= REL_MAX_DIST, far, g))
    return jnp.where(valid, jnp.concatenate(parts, axis=1), NEG_INF)


def _bias_cmp_kernel(tab_ref, o_ref, *, tb, rows, n_cmp):
    head = pl.program_id(0)
    t = pl.program_id(1) * tb + lax.broadcasted_iota(jnp.int32, (tb, rows), 0)
    n = lax.broadcasted_iota(jnp.int32, (tb, rows), 1)
    dist = t - (n * CMP_STRIDE + CMP_BLOCK - 1)
    o_ref[0] = _rel_bias(dist, (dist >= 0) & (n < n_cmp), tab_ref, head) * LOG2E


def _bias_cmp(table, seq, tb):
    rows = seq // CMP_STRIDE
    n_cmp = (seq - CMP_BLOCK) // CMP_STRIDE + 1
    return pl.pallas_call(
        functools.partial(_bias_cmp_kernel, tb=tb, rows=rows, n_cmp=n_cmp),
        out_shape=jax.ShapeDtypeStruct((NSA_HEADS, seq, rows), F32),
        grid=(NSA_HEADS, seq // tb),
        in_specs=[pl.BlockSpec(memory_space=pltpu.SMEM)],
        out_specs=pl.BlockSpec((1, tb, rows), lambda h, i: (h, i, 0)),
        compiler_params=_cparams("parallel", "arbitrary"),
        name="bias_cmp",
    )(table)


def _bias_tiles_kernel(tab_ref, o_ref, *, tq):
    head = pl.program_id(0)
    kind = pl.program_id(1)
    ij = (lax.broadcasted_iota(jnp.int32, (tq, tq), 0) - lax.broadcasted_iota(jnp.int32, (tq, tq), 1))
    dist = jnp.where(kind == KIND_DIAG, ij, jnp.where(kind == KIND_SUB, tq + ij, WINDOW + ij))
    lo = jnp.where(kind == KIND_DIAG, 0, -tq)
    hi = jnp.where(kind == KIND_CORNER, 0, tq)
    far_dist = jnp.full((SUBLANE, LANE), tq + 1, jnp.int32)
    far = _rel_bias(far_dist, far_dist > 0, tab_ref, head)[0:1, 0:1]
    o_ref[0, 0] = (_rel_bias(dist, (ij >= lo) & (ij < hi), tab_ref, head) - far) * LOG2E


def _bias_tiles(table, tq):
    assert WINDOW % tq == 0 and tq + 1 >= REL_MAX_DIST
    return pl.pallas_call(
        functools.partial(_bias_tiles_kernel, tq=tq),
        out_shape=jax.ShapeDtypeStruct((NSA_KV_GROUPS, N_BIAS_KINDS, NSA_HPG * tq, tq), F32),
        grid=(NSA_HEADS, N_BIAS_KINDS),
        in_specs=[pl.BlockSpec(memory_space=pltpu.SMEM)],
        out_specs=pl.BlockSpec((1, 1, tq, tq), lambda h, k: (h // NSA_HPG, k, h % NSA_HPG, 0)),
        compiler_params=_cparams("parallel", "arbitrary"),
        name="bias_tiles",
    )(table)


def _cmp_scores(q_ref, kc_ref, bias_ref):
    kc = kc_ref[0, 0]
    return [_dot_nt(q_ref[:, j * HEAD_DIM:(j + 1) * HEAD_DIM], kc) + bias_ref[j] for j in range(NSA_HPG)]


def _cmp_attend(scores, vc_ref, bias_ref, ov_ref, oc_ref):
    vc = vc_ref[0, 0]
    psum = None
    for j, s in enumerate(scores):
        hs = slice(j * HEAD_DIM, (j + 1) * HEAD_DIM)
        bias = bias_ref[j]
        m = jnp.max(s, axis=-1, keepdims=True)
        e = jnp.exp2(s - m)
        p = e / jnp.sum(e, axis=-1, keepdims=True)
        p = jnp.where(bias > 0.5 * NEG_INF, p, 0.0)
        oc_ref[:, hs] = _dot(p.astype(BF16), vc)
        psum = p if psum is None else psum + p
    ov = ov_ref[...]
    imp = None
    rem = psum
    for _ in range(3):
        piece = rem.astype(BF16)
        part = _dot_nt(ov, piece)
        imp = part if imp is None else imp + part
        rem = rem - piece.astype(F32)
    return imp


def _cmp_select(imp, qi, sel_ref, *, tq, n_sel, n_top):
    t = qi * tq + lax.broadcasted_iota(jnp.int32, (n_sel, tq), 1)
    cur = jnp.right_shift(t, int(math.log2(SEL_BLOCK)))
    blk = lax.broadcasted_iota(jnp.int32, (n_sel, tq), 0)
    forced = (blk == 0) | (blk == cur) | (blk == cur - 1)
    imp = jnp.where(blk > cur, -1.0, jnp.where(forced, FORCED_SCORE, imp))
    groups = [imp[g * SUBLANE:(g + 1) * SUBLANE, :] for g in range(n_sel // SUBLANE)]
    ranks = [jnp.zeros((SUBLANE, tq), jnp.int32) for _ in groups]
    sub = lax.broadcasted_iota(jnp.int32, (SUBLANE, tq), 0)
    for i in range(n_sel):
        row = imp[i:i + 1, :]
        for g, x in enumerate(groups):
            if g * SUBLANE > i:
                ahead = row >= x
            elif (g + 1) * SUBLANE - 1 < i:
                ahead = row > x
            else:
                ahead = (row > x) | ((row == x) & (sub > i - g * SUBLANE))
            ranks[g] = ranks[g] + ahead.astype(jnp.int32)
    sel_t = jnp.where(jnp.concatenate(ranks, axis=0) < n_top, 0.0, NEG_INF)
    if n_sel < LANE:
        sel_t = jnp.concatenate([sel_t, jnp.zeros((LANE - n_sel, tq), F32)], axis=0)
    sel_ref[0, 0] = sel_t.T.astype(BF16)


def _conv_cmp_kernel(x_ref, wh_ref, wb_ref, wc_ref, wz_ref, cw_ref, q_ref, kc_ref, vc_ref, bias_ref, ov_ref,
                     ya_ref, oc_ref, sel_ref, u_ref, *, tm, tiles_per_seq, tq, n_sel, n_top, nq, batch):
    flat = pl.program_id(0) * pl.num_programs(1) + pl.program_id(1)
    _conv_carry_reset(u_ref, tiles_per_seq)
    scores = _cmp_scores(q_ref, kc_ref, bias_ref)
    _conv_inputs(x_ref, wh_ref, wc_ref, u_ref, tm=tm)
    imp = _cmp_attend(scores, vc_ref, bias_ref, ov_ref, oc_ref)
    _conv_outputs(x_ref, wb_ref, wz_ref, cw_ref, ya_ref, u_ref, tm=tm)
    _cmp_select(imp, (flat // batch) % nq, sel_ref, tq=tq, n_sel=n_sel, n_top=n_top)


def _conv_cmp(xb, w, layer, conv_w, hb, kcv, bias_c, ov, batch, seq, tm, tc, tq):
    m = xb.shape[0]
    nct = D_CONV // tc
    ni = m // tm
    rows = seq // CMP_STRIDE
    n_sel = seq // SEL_BLOCK
    n_top = min(N_SELECT, n_sel)
    nq = seq // tq
    g_n = NSA_KV_GROUPS
    assert n_sel <= LANE and n_sel % SUBLANE == 0 and nct * ni == g_n * nq * batch

    def wspec(part):
        return pl.BlockSpec((None, D_MODEL, tc), lambda j, i: (layer, 0, (EV_A + part * D_CONV) // tc + j))

    flat = lambda j, i: j * ni + i
    grp = lambda j, i: flat(j, i) // (nq * batch)
    qt = lambda j, i: (flat(j, i) // batch) % nq
    bat = lambda j, i: flat(j, i) % batch
    return pl.pallas_call(
        functools.partial(_conv_cmp_kernel, tm=tm, tiles_per_seq=seq // tm, tq=tq, n_sel=n_sel, n_top=n_top,
                          nq=nq, batch=batch),
        out_shape=(jax.ShapeDtypeStruct((m, D_CONV), BF16),
                   jax.ShapeDtypeStruct((m, D_NSA), F32),
                   jax.ShapeDtypeStruct((batch, g_n, seq, LANE), BF16)),
        grid=(nct, ni),
        in_specs=[pl.BlockSpec((tm, D_MODEL), lambda j, i: (i, 0)),
                  wspec(0), wspec(1), wspec(2), wspec(3),
                  pl.BlockSpec((CONV_WIDTH, tc), lambda j, i: (0, j)),
                  pl.BlockSpec((tq, GROUP_W), lambda j, i: (bat(j, i) * nq + qt(j, i), HB_Q // GROUP_W + grp(j, i))),
                  pl.BlockSpec((1, 1, rows, HEAD_DIM), lambda j, i: (bat(j, i), grp(j, i), 0, 0)),
                  pl.BlockSpec((1, 1, rows, HEAD_DIM), lambda j, i: (bat(j, i), g_n + grp(j, i), 0, 0)),
                  pl.BlockSpec((NSA_HPG, tq, rows), lambda j, i: (grp(j, i), qt(j, i), 0)),
                  pl.BlockSpec((n_sel, rows), lambda j, i: (0, 0))],
        out_specs=(pl.BlockSpec((tm, tc), lambda j, i: (i, j)),
                   pl.BlockSpec((tq, GROUP_W), lambda j, i: (bat(j, i) * nq + qt(j, i), grp(j, i))),
                   pl.BlockSpec((1, 1, tq, LANE), lambda j, i: (bat(j, i), grp(j, i), qt(j, i), 0))),
        scratch_shapes=[pltpu.VMEM((tm + SUBLANE, tc), F32)],
        compiler_params=_cparams("arbitrary", "arbitrary"),
        name="conv_cmp",
    )(xb, w, w, w, w, conv_w, hb, kcv, kcv, bias_c, ov)


FLASH_ROWS = 128


def _flash_init(state):
    _, _, m_ref, _, acc_ref = state
    m_ref[...] = jnp.full(m_ref.shape, -3e38, F32)
    acc_ref[...] = jnp.zeros(acc_ref.shape, F32)


def _flash_scores(qa_ref, kdim, k, s_ref, h, tq):
    hr = slice(h * tq, (h + 1) * tq)
    s_ref[hr, :] = _dot_nt(qa_ref[hr, 0:kdim], k)


def _flash_step(qa_ref, bias_ref, kind, v, nxt, state, tq):
    s_ref, p_ref, m_ref, a_ref, acc_ref = state
    tk = s_ref.shape[1]
    for h in range(NSA_HPG):
        hr = slice(h * tq, (h + 1) * tq)
        for r in range(tq // FLASH_ROWS):
            rs = slice(h * tq + r * FLASH_ROWS, h * tq + (r + 1) * FLASH_ROWS)
            s = s_ref[rs, :]
            if kind is not None:
                s = s + bias_ref[0, kind, rs, :]
            m_old = m_ref[rs, :]
            m_new = jnp.maximum(m_old, jnp.max(s, axis=-1, keepdims=True))
            p_ref[rs, :] = jnp.exp2(s - jnp.tile(m_new, (1, tk // LANE))).astype(BF16)
            a_ref[rs, :] = jnp.exp2(m_old - m_new)
            m_ref[rs, :] = m_new
        if nxt is not None:
            _flash_scores(qa_ref, nxt[0], nxt[1], s_ref, h, tq)
        acc_ref[hr, :] = jnp.tile(a_ref[hr, :], (1, 2)) * acc_ref[hr, :] + _dot(p_ref[hr, :], v)


def _selwin_kernel(q_ref, ks_ref, vs_ref, kw_ref, vw_ref, oh_ref, sel_ref, bias_ref, oc_ref, gt_ref, bz_ref,
                   o_ref, qa_ref, s_ref, p_ref, m_ref, a_ref, acc_ref, os_ref, *, tq):
    qi = pl.program_id(2)
    state = (s_ref, p_ref, m_ref, a_ref, acc_ref)
    aug = 2 * HEAD_DIM
    ones = jnp.ones((tq, HEAD_DIM), BF16)
    for j in range(NSA_HPG):
        qa_ref[j * tq:(j + 1) * tq, 0:HEAD_DIM] = q_ref[:, j * HEAD_DIM:(j + 1) * HEAD_DIM]
        qa_ref[j * tq:(j + 1) * tq, HEAD_DIM:aug] = sel_ref[0, 0]

    def rows_of(kt):
        return pl.ds(pl.multiple_of(kt * tq, tq), tq)

    def sel_keys(kt):
        return aug, jnp.concatenate([ks_ref[rows_of(kt), :], oh_ref[rows_of(kt), :]], axis=1)

    def win_keys(kt):
        return HEAD_DIM, kw_ref[rows_of(kt), :]

    def values(v_ref, kt):
        return jnp.concatenate([v_ref[rows_of(kt), :], ones], axis=1)

    _flash_init(state)
    for h in range(NSA_HPG):
        _flash_scores(qa_ref, *sel_keys(0), s_ref, h, tq)

    def far_body(kt, carry):
        _flash_step(qa_ref, bias_ref, None, values(vs_ref, kt), sel_keys(kt + 1), state, tq)
        return carry

    lax.fori_loop(0, jnp.maximum(qi - 1, 0), far_body, 0)

    def near_tiles(first):
        if not first:
            _flash_step(qa_ref, bias_ref, KIND_SUB, values(vs_ref, qi - 1), sel_keys(qi), state, tq)
        _flash_step(qa_ref, bias_ref, KIND_DIAG, values(vs_ref, qi), win_keys(jnp.maximum(qi - 1, 0)), state, tq)
        os_ref[...] = acc_ref[:, 0:HEAD_DIM] / acc_ref[:, HEAD_DIM:aug]
        _flash_init(state)
        if not first:
            _flash_step(qa_ref, bias_ref, KIND_CORNER, values(vw_ref, qi - 1), win_keys(qi), state, tq)
        _flash_step(qa_ref, bias_ref, KIND_DIAG, values(vw_ref, qi), None, state, tq)
        gate = _sigmoid(gt_ref[...])
        for j in range(NSA_HPG):
            hs = slice(j * HEAD_DIM, (j + 1) * HEAD_DIM)
            rs = slice(j * tq, (j + 1) * tq)
            o_w = acc_ref[rs, 0:HEAD_DIM] / acc_ref[rs, HEAD_DIM:aug]
            o = (gate[:, j:j + 1] * oc_ref[:, hs]
                 + gate[:, NSA_HPG + j:NSA_HPG + j + 1] * os_ref[rs, :]
                 + gate[:, 2 * NSA_HPG + j:2 * NSA_HPG + j + 1] * o_w)
            o_ref[:, hs] = (o * _silu(bz_ref[:, hs])).astype(BF16)

    pl.when(qi == 0)(functools.partial(near_tiles, True))
    pl.when(qi > 0)(functools.partial(near_tiles, False))


def _selwin_attn(hb, ha, onehot, sel, bias_t, o_c, batch, seq, tq):
    nq = seq // tq
    rows = NSA_HPG * tq
    assert tq == WINDOW
    once = pl.Buffered(1)
    kv_spec = lambda base: pl.BlockSpec((seq, HEAD_DIM), lambda g, b, i: (b, base // HEAD_DIM + g))
    row_g = lambda g, b, i: (b * nq + i, g)
    return pl.pallas_call(
        functools.partial(_selwin_kernel, tq=tq),
        out_shape=jax.ShapeDtypeStruct((batch * seq, D_NSA), BF16),
        grid=(NSA_KV_GROUPS, batch, nq),
        in_specs=[pl.BlockSpec((tq, GROUP_W), lambda g, b, i: (b * nq + i, HB_Q // GROUP_W + g)),
                  kv_spec(HB_KS), kv_spec(HB_VS), kv_spec(HB_KW), kv_spec(HB_VW),
                  pl.BlockSpec((seq, LANE), lambda g, b, i: (0, 0), pipeline_mode=once),
                  pl.BlockSpec((1, 1, tq, LANE), lambda g, b, i: (b, g, i, 0)),
                  pl.BlockSpec((1, N_BIAS_KINDS, rows, tq), lambda g, b, i: (g, 0, 0, 0), pipeline_mode=once),
                  pl.BlockSpec((tq, GROUP_W), row_g),
                  pl.BlockSpec((tq, LANE), lambda g, b, i: (b * nq + i, HA_GT // LANE + g)),
                  pl.BlockSpec((tq, GROUP_W), lambda g, b, i: (b * nq + i, HA_BZ // GROUP_W + g))],
        out_specs=pl.BlockSpec((tq, GROUP_W), row_g),
        scratch_shapes=[pltpu.VMEM((rows, 2 * HEAD_DIM), BF16),
                        pltpu.VMEM((rows, tq), F32),
                        pltpu.VMEM((rows, tq), BF16),
                        pltpu.VMEM((rows, LANE), F32),
                        pltpu.VMEM((rows, LANE), F32),
                        pltpu.VMEM((rows, 2 * HEAD_DIM), F32),
                        pltpu.VMEM((rows, HEAD_DIM), F32)],
        compiler_params=_cparams("parallel", "parallel", "arbitrary"),
        name="selwin_attn",
    )(hb, hb, hb, hb, hb, onehot, sel, bias_t, o_c, ha, ha)


def _even_tail_weights(w):
    w_g = w[:, EV_GT:EV_BZ].reshape(D_MODEL, N_BRANCH, NSA_KV_GROUPS, NSA_HPG)
    w_g = jnp.transpose(w_g, (0, 2, 1, 3)).reshape(D_MODEL, NSA_KV_GROUPS, N_BRANCH * NSA_HPG)
    w_g = jnp.pad(w_g, ((0, 0), (0, 0), (0, LANE - N_BRANCH * NSA_HPG))).reshape(D_MODEL, NSA_KV_GROUPS * LANE)
    return jnp.concatenate([w[:, EV_BZ:EV_END], w_g], axis=1).astype(BF16)


def _cast_chunk_kernel(w_ref, o_ref):
    o_ref[...] = w_ref[...].astype(BF16)


def _cast_column_chunks(w, cw):
    layers, k, n = w.shape
    return pl.pallas_call(
        _cast_chunk_kernel,
        out_shape=jax.ShapeDtypeStruct((layers, n // cw, k, cw), BF16),
        grid=(layers, n // cw),
        in_specs=[pl.BlockSpec((None, k, cw), lambda l, c: (l, 0, c))],
        out_specs=pl.BlockSpec((None, None, k, cw), lambda l, c: (l, c, 0, 0)),
        compiler_params=_cparams("parallel", "parallel"),
        name="cast_chunks",
    )(w)


def _overlap_matrix(seq):
    rows = seq // CMP_STRIDE
    n_cmp = (seq - CMP_BLOCK) // CMP_STRIDE + 1
    n_sel = seq // SEL_BLOCK
    cstart = np.arange(rows)[None, :] * CMP_STRIDE
    sstart = np.arange(n_sel)[:, None] * SEL_BLOCK
    ov = (cstart < sstart + SEL_BLOCK) & (cstart + CMP_BLOCK > sstart) & (np.arange(rows)[None, :] < n_cmp)
    return jnp.asarray(ov.astype(np.float32), dtype=BF16)


def _block_onehot(seq):
    blk = np.arange(seq)[:, None] // SEL_BLOCK
    return jnp.asarray((blk == np.arange(LANE)[None, :]).astype(np.float32), dtype=BF16)


def kernel(x, rel_bias_table, ln_g, ln_b, ev_w_in, ev_conv_w, ev_cmp_pos, ev_cmp_w1, ev_cmp_w2, ev_w_out,
           od_w_in, od_ln_g, od_ln_b, od_sgu_w, od_sgu_b, od_w_out):
    batch, seq, d = x.shape
    depth = ln_g.shape[0]
    alpha = (2 * depth) ** 0.25
    m = batch * seq
    tq = ATT_TILE
    assert d == D_MODEL and ev_w_in.shape[-1] == EV_END
    assert seq % max(CONV_ROWS, ATT_TILE, CMP_TILE, BIAS_ROWS) == 0 and m % max(ODD_ROWS, OUT_ROWS, NSA_ROWS) == 0

    xf = x.reshape(m, d)
    xb = None
    bias_c = _bias_cmp(rel_bias_table, seq, tb=BIAS_ROWS)
    bias_t = _bias_tiles(rel_bias_table, tq)
    ov = _overlap_matrix(seq)
    onehot = _block_onehot(seq)
    ev_w, ev_wo = ev_w_in.astype(BF16), ev_w_out.astype(BF16)
    od_wo = od_w_out.astype(BF16)
    od_w = _cast_column_chunks(od_w_in, PROJ_CHUNK)

    for layer in range(depth):
        i = layer // 2
        g = ln_g[layer].reshape(1, d)
        b = ln_b[layer].reshape(1, d)
        if layer % 2 == 0:
            w_tail = _even_tail_weights(ev_w[i])
            hb, ha, *cast = _nsa_proj(xf if xb is None else xb, ev_w, i, w_tail, tm=NSA_ROWS)
            xb = cast[0] if cast else xb
            kcv = _compress(ha, ev_cmp_w1[i].astype(BF16), ev_cmp_w2[i].astype(BF16),
                            ev_cmp_pos[i].reshape(2, 1, CMP_BLOCK * HEAD_DIM).astype(BF16), batch, seq)
            y_a, o_c, sel = _conv_cmp(xb, ev_w, i, ev_conv_w[i], hb, kcv, bias_c, ov, batch, seq,
                                      tm=CONV_ROWS, tc=CONV_COLS, tq=CMP_TILE)
            y_b = _selwin_attn(hb, ha, onehot, sel, bias_t, o_c, batch, seq, tq)
            xf, xb = _outproj_ln(y_a, 0, y_b, 0, ev_wo, i, xf, g, b, alpha, OUT_ROWS, "outproj_even")
        else:
            y = _odd_mixer(xb, od_w, i, od_ln_g[i].reshape(1, d), od_ln_b[i].reshape(1, d),
                           od_sgu_w[i], od_sgu_b[i].reshape(SGU_GROUPS, SGU_CHUNK, 1), tm=ODD_ROWS)
            xf, xb = _outproj_ln(y, 0, y, 1, od_wo, i, xf, g, b, alpha, OUT_ROWS, "outproj_odd")
    return xf.reshape(batch, seq, d)
```

```python
import functools
import math

import numpy as np
import jax
import jax.numpy as jnp
from jax import lax
from jax.experimental import pallas as pl
from jax.experimental.pallas import tpu as pltpu

F32 = jnp.float32
BF16 = jnp.bfloat16

D_MODEL = 2048
CONV_WIDTH = 3
D_CONV = 1024
NSA_HEADS = 8
NSA_KV_GROUPS = 2
NSA_HPG = NSA_HEADS // NSA_KV_GROUPS
HEAD_DIM = 128
D_NSA = NSA_HEADS * HEAD_DIM
D_KV = NSA_KV_GROUPS * HEAD_DIM
CMP_BLOCK = 32
CMP_STRIDE = 16
SEL_BLOCK = 64
N_SELECT = 16
WINDOW = 512
N_BRANCH = 3
D_SGU = D_MODEL
SGU_GROUPS = 8
SGU_CHUNK = 128
SGU_GROUP_DIM = D_SGU // SGU_GROUPS
REL_BUCKETS = 32
REL_MAX_DIST = 128
LN_EPS = 1e-5
NEG_INF = -1e30
FORCED_SCORE = 1e9
GROUP_W = NSA_HPG * HEAD_DIM
LOG2E = math.log2(math.e)
Q_SCALE = HEAD_DIM ** -0.5 * LOG2E

LANE = 128
SUBLANE = 8
V7X_VMEM_BYTES = 64 * 1024 * 1024
VMEM_LIMIT = V7X_VMEM_BYTES * 7 // 8

EV_A = 0
EV_Q = 4 * D_CONV
EV_KC = EV_Q + D_NSA
EV_KS = EV_KC + 2 * D_KV
EV_GT = EV_KS + 4 * D_KV
EV_BZ = EV_GT + N_BRANCH * NSA_HEADS
EV_END = EV_BZ + D_NSA

HB_Q, HB_KS, HB_VS, HB_KW, HB_VW = 0, 1024, 1280, 1536, 1792
HA_KC, HA_BZ, HA_GT = 0, 512, 1536
HB_W = 2048
HA_W = HA_GT + NSA_KV_GROUPS * LANE
PROJ_CHUNK = 512

NSA_ROWS = 512
CONV_ROWS, CONV_COLS = 1024, 256
ODD_ROWS = 1024
OUT_ROWS = 512
BIAS_ROWS = 1024
CMP_TILE = 512
ATT_TILE = 512
ROW_SUB = 256
OUT_SUB = 128
KIND_DIAG, KIND_SUB, KIND_CORNER = 0, 1, 2
N_BIAS_KINDS = 3


def _cparams(*sem):
    return pltpu.CompilerParams(dimension_semantics=sem, vmem_limit_bytes=VMEM_LIMIT)


def _sigmoid(x):
    return 1.0 / (1.0 + jnp.exp(-x))


def _silu(x):
    return x * _sigmoid(x)


def _gelu_tanh(x):
    c = math.sqrt(2.0 / math.pi)
    return x * (0.5 * (1.0 + jnp.tanh(c * (x + 0.044715 * (x * x * x)))))


def _dot_nt(a, b):
    return lax.dot_general(a, b, (((1,), (1,)), ((), ())), preferred_element_type=F32)


def _dot(a, b):
    return jnp.dot(a, b, preferred_element_type=F32)


def _layer_norm(z, g, b):
    mu = jnp.mean(z, axis=-1, keepdims=True)
    zc = z - mu
    var = jnp.mean(zc * zc, axis=-1, keepdims=True)
    return zc * lax.rsqrt(var + LN_EPS) * g + b


def _outproj_kernel(y1_ref, y2_ref, w1_ref, w2_ref, x_ref, g_ref, b_ref, o_ref, ob_ref, *, alpha):
    for r in range(x_ref.shape[0] // OUT_SUB):
        rs = slice(r * OUT_SUB, (r + 1) * OUT_SUB)
        y = _dot(y1_ref[rs, :], w1_ref[...]) + _dot(y2_ref[rs, :], w2_ref[...])
        out = _layer_norm(alpha * x_ref[rs, :] + y, g_ref[...], b_ref[...])
        o_ref[rs, :] = out
        ob_ref[rs, :] = out.astype(BF16)


def _outproj_ln(y1, y1_col, y2, y2_col, w_out, layer, x, g, b, alpha, tm, name):
    m, d = x.shape
    kh = w_out.shape[1] // 2
    return pl.pallas_call(
        functools.partial(_outproj_kernel, alpha=alpha),
        out_shape=(jax.ShapeDtypeStruct((m, d), F32), jax.ShapeDtypeStruct((m, d), BF16)),
        grid=(m // tm,),
        in_specs=[pl.BlockSpec((tm, kh), lambda i: (i, y1_col)),
                  pl.BlockSpec((tm, kh), lambda i: (i, y2_col)),
                  pl.BlockSpec((None, kh, d), lambda i: (layer, 0, 0)),
                  pl.BlockSpec((None, kh, d), lambda i: (layer, 1, 0)),
                  pl.BlockSpec((tm, d), lambda i: (i, 0)),
                  pl.BlockSpec((1, d), lambda i: (0, 0)),
                  pl.BlockSpec((1, d), lambda i: (0, 0))],
        out_specs=(pl.BlockSpec((tm, d), lambda i: (i, 0)),
                   pl.BlockSpec((tm, d), lambda i: (i, 0))),
        compiler_params=_cparams("parallel"),
        name=name,
    )(y1, y2, w_out, w_out, x, g, b)


def _odd_kernel(x_ref, wv_ref, wu_ref, wz_ref, g_ref, b_ref, sw_ref, sb_ref, o_ref, vs_ref, mu_ref, rstd_ref,
                *, tm):
    step = pl.program_id(1)
    nchunk = D_SGU // PROJ_CHUNK
    ngrp = PROJ_CHUNK // SGU_GROUP_DIM

    def project_v():
        for r in range(tm // ROW_SUB):
            rs = slice(r * ROW_SUB, (r + 1) * ROW_SUB)
            x = x_ref[rs, :]
            tot = None
            for c in range(nchunk):
                v = _gelu_tanh(_dot(x, wv_ref[c]))
                vs_ref[c, rs, :] = v
                part = jnp.sum(v, axis=-1, keepdims=True)
                tot = part if tot is None else tot + part
            mu = tot * (1.0 / D_SGU)
            sq = None
            for c in range(nchunk):
                vc = vs_ref[c, rs, :] - mu
                part = jnp.sum(vc * vc, axis=-1, keepdims=True)
                sq = part if sq is None else sq + part
            mu_ref[rs, :] = mu
            rstd_ref[rs, :] = lax.rsqrt(sq * (1.0 / D_SGU) + LN_EPS)

    def mix_columns(chunk):
        row = lax.broadcasted_iota(jnp.int32, (SGU_CHUNK, SGU_CHUNK), 0)
        col = lax.broadcasted_iota(jnp.int32, (SGU_CHUNK, SGU_CHUNK), 1)
        wgs = [jnp.where(col <= row, sw_ref[g], 0.0).astype(BF16) for g in range(ngrp)]
        for r in range(tm // ROW_SUB):
            rs = slice(r * ROW_SUB, (r + 1) * ROW_SUB)
            x = x_ref[rs, :]
            u = _gelu_tanh(_dot(x, wu_ref[...]))
            z = _dot(x, wz_ref[...])
            vn = ((vs_ref[chunk, rs, :] - mu_ref[rs, :]) * rstd_ref[rs, :] * g_ref[...] + b_ref[...]).astype(BF16)
            for g in range(ngrp):
                cs = slice(g * SGU_GROUP_DIM, (g + 1) * SGU_GROUP_DIM)
                for c in range(ROW_SUB // SGU_CHUNK):
                    ls = slice(c * SGU_CHUNK, (c + 1) * SGU_CHUNK)
                    os_ = slice(r * ROW_SUB + c * SGU_CHUNK, r * ROW_SUB + (c + 1) * SGU_CHUNK)
                    mixed = _dot(wgs[g], vn[ls, cs]) + sb_ref[g]
                    o_ref[os_, cs] = (u[ls, cs] * mixed * _silu(z[ls, cs])).astype(BF16)

    @pl.when(step == 0)
    def _():
        project_v()
        mix_columns(0)

    @pl.when(step > 0)
    def _():
        mix_columns(step)


def _odd_mixer(xb, w_in, layer, ln_g, ln_b, sgu_w, sgu_b, tm):
    m = xb.shape[0]
    cw = PROJ_CHUNK
    nstep = D_SGU // cw
    gps = cw // SGU_GROUP_DIM
    return pl.pallas_call(
        functools.partial(_odd_kernel, tm=tm),
        out_shape=jax.ShapeDtypeStruct((m, D_SGU), BF16),
        grid=(m // tm, nstep),
        in_specs=[pl.BlockSpec((tm, D_MODEL), lambda i, s: (i, 0)),
                  pl.BlockSpec((None, nstep, D_MODEL, cw), lambda i, s: (layer, 1, 0, 0)),
                  pl.BlockSpec((None, None, D_MODEL, cw), lambda i, s: (layer, s, 0, 0)),
                  pl.BlockSpec((None, None, D_MODEL, cw), lambda i, s: (layer, 2 * nstep + s, 0, 0)),
                  pl.BlockSpec((1, cw), lambda i, s: (0, s)),
                  pl.BlockSpec((1, cw), lambda i, s: (0, s)),
                  pl.BlockSpec((gps, SGU_CHUNK, SGU_CHUNK), lambda i, s: (s, 0, 0)),
                  pl.BlockSpec((gps, SGU_CHUNK, 1), lambda i, s: (s, 0, 0))],
        out_specs=pl.BlockSpec((tm, cw), lambda i, s: (i, s)),
        scratch_shapes=[pltpu.VMEM((nstep, tm, cw), F32),
                        pltpu.VMEM((tm, 1), F32),
                        pltpu.VMEM((tm, 1), F32)],
        compiler_params=_cparams("parallel", "arbitrary"),
        name="odd_mixer",
    )(xb, w_in, w_in, w_in, ln_g, ln_b, sgu_w, sgu_b)


def _conv_carry_reset(u_ref, tiles_per_seq):
    @pl.when(pl.program_id(1) % tiles_per_seq == 0)
    def _():
        u_ref[0:SUBLANE, :] = jnp.zeros((SUBLANE, u_ref.shape[1]), F32)


def _conv_inputs(x_ref, wh_ref, wc_ref, u_ref, *, tm):
    for r in range(tm // ROW_SUB):
        rs = slice(r * ROW_SUB, (r + 1) * ROW_SUB)
        x = x_ref[rs, :]
        u_ref[SUBLANE + r * ROW_SUB:SUBLANE + (r + 1) * ROW_SUB, :] = _dot(x, wc_ref[...]) * _dot(x, wh_ref[...])


def _conv_outputs(x_ref, wb_ref, wz_ref, cw_ref, o_ref, u_ref, *, tm):
    for r in range(tm // ROW_SUB):
        rs = slice(r * ROW_SUB, (r + 1) * ROW_SUB)
        x = x_ref[rs, :]
        conv = cw_ref[CONV_WIDTH - 1:CONV_WIDTH, :] * u_ref[SUBLANE + r * ROW_SUB:SUBLANE + (r + 1) * ROW_SUB, :]
        for k in range(CONV_WIDTH - 1):
            lo = SUBLANE + r * ROW_SUB - (CONV_WIDTH - 1 - k)
            conv = conv + cw_ref[k:k + 1, :] * u_ref[lo:lo + ROW_SUB, :]
        o_ref[rs, :] = (_dot(x, wb_ref[...]) * conv * _silu(_dot(x, wz_ref[...]))).astype(BF16)
    u_ref[0:SUBLANE, :] = u_ref[tm:tm + SUBLANE, :]


def _nsa_proj_kernel(x_ref, wq_ref, wkc_ref, wks_ref, wkw_ref, wt_ref, hb_ref, ha_ref, *xb_ref):
    x = x_ref[...].astype(BF16)
    if xb_ref:
        xb_ref[0][...] = x
    cw = PROJ_CHUNK
    for c in range(D_NSA // cw):
        hb_ref[:, c * cw:(c + 1) * cw] = (_dot(x, wq_ref[:, c * cw:(c + 1) * cw]) * Q_SCALE).astype(BF16)
    hb_ref[:, HB_KS:HB_KS + cw] = _dot(x, wks_ref[...]).astype(BF16)
    hb_ref[:, HB_KW:HB_KW + cw] = _dot(x, wkw_ref[...]).astype(BF16)
    ha_ref[:, HA_KC:HA_KC + cw] = _dot(x, wkc_ref[...])
    for lo in range(0, HA_W - HA_BZ, cw):
        hi = min(lo + cw, HA_W - HA_BZ)
        ha_ref[:, HA_BZ + lo:HA_BZ + hi] = _dot(x, wt_ref[:, lo:hi])


def _nsa_proj(x, w, layer, w_tail, tm):
    m = x.shape[0]
    once = pl.Buffered(1)
    cw = PROJ_CHUNK
    assert EV_Q % D_NSA == 0 and EV_KC % cw == 0 and EV_KS % cw == 0 and 2 * D_KV == cw
    out_shape = [jax.ShapeDtypeStruct((m, HB_W), BF16), jax.ShapeDtypeStruct((m, HA_W), F32)]
    out_specs = [pl.BlockSpec((tm, HB_W), lambda i: (i, 0)), pl.BlockSpec((tm, HA_W), lambda i: (i, 0))]
    if x.dtype != BF16:
        out_shape.append(jax.ShapeDtypeStruct((m, D_MODEL), BF16))
        out_specs.append(pl.BlockSpec((tm, D_MODEL), lambda i: (i, 0)))
    return pl.pallas_call(
        _nsa_proj_kernel,
        out_shape=tuple(out_shape),
        grid=(m // tm,),
        in_specs=[pl.BlockSpec((tm, D_MODEL), lambda i: (i, 0)),
                  pl.BlockSpec((None, D_MODEL, D_NSA), lambda i: (layer, 0, EV_Q // D_NSA), pipeline_mode=once),
                  pl.BlockSpec((None, D_MODEL, cw), lambda i: (layer, 0, EV_KC // cw), pipeline_mode=once),
                  pl.BlockSpec((None, D_MODEL, cw), lambda i: (layer, 0, EV_KS // cw), pipeline_mode=once),
                  pl.BlockSpec((None, D_MODEL, cw), lambda i: (layer, 0, EV_KS // cw + 1), pipeline_mode=once),
                  pl.BlockSpec((D_MODEL, HA_W - HA_BZ), lambda i: (0, 0), pipeline_mode=once)],
        out_specs=tuple(out_specs),
        compiler_params=_cparams("parallel"),
        name="nsa_proj",
    )(x, w, w, w, w, w_tail)


def _compress_kernel(tok_ref, w1_ref, w2_ref, pos_ref, o_ref, b_ref, *, rows):
    half = CMP_STRIDE * HEAD_DIM
    x2 = jnp.concatenate(
        [tok_ref[pl.ds(l, rows, stride=CMP_STRIDE), :] for l in range(CMP_STRIDE)], axis=1).astype(BF16)
    lo = _dot(x2, w1_ref[0, 0:half, :])
    hi = _dot(x2, w1_ref[0, half:2 * half, :])
    b_ref[0:rows, :] = hi
    b_ref[rows:rows + SUBLANE, :] = jnp.zeros((SUBLANE, HEAD_DIM), F32)
    posb = _dot(jnp.broadcast_to(pos_ref[0], (SUBLANE, 2 * half)), w1_ref[0])[0:1, :]
    pre = lo + b_ref[1:rows + 1, :] + posb
    o_ref[0, 0] = _dot(_silu(pre).astype(BF16), w2_ref[0]).astype(BF16)


def _compress(ha, w1, w2, pos, batch, seq):
    rows = seq // CMP_STRIDE
    nkv = 2 * NSA_KV_GROUPS
    kc_block = HA_KC // HEAD_DIM
    return pl.pallas_call(
        functools.partial(_compress_kernel, rows=rows),
        out_shape=jax.ShapeDtypeStruct((batch, nkv, rows, HEAD_DIM), BF16),
        grid=(batch, nkv),
        in_specs=[pl.BlockSpec((seq, HEAD_DIM), lambda b, c: (b, kc_block + c)),
                  pl.BlockSpec((1, CMP_BLOCK * HEAD_DIM, HEAD_DIM), lambda b, c: (c // NSA_KV_GROUPS, 0, 0)),
                  pl.BlockSpec((1, HEAD_DIM, HEAD_DIM), lambda b, c: (c // NSA_KV_GROUPS, 0, 0)),
                  pl.BlockSpec((1, 1, CMP_BLOCK * HEAD_DIM), lambda b, c: (c // NSA_KV_GROUPS, 0, 0))],
        out_specs=pl.BlockSpec((1, 1, rows, HEAD_DIM), lambda b, c: (b, c, 0, 0)),
        scratch_shapes=[pltpu.VMEM((rows + SUBLANE, HEAD_DIM), F32)],
        compiler_params=_cparams("parallel", "arbitrary"),
        name="cmp_blocks",
    )(ha, w1, w2, pos)


def _t5_bucket(dist):
    n = jnp.maximum(dist, 0)
    max_exact = REL_BUCKETS // 2
    large = max_exact + (jnp.log(jnp.maximum(n, 1).astype(F32) / max_exact)
                         / math.log(REL_MAX_DIST / max_exact) * (REL_BUCKETS - max_exact)).astype(jnp.int32)
    large = jnp.minimum(large, REL_BUCKETS - 1)
    return jnp.where(n < max_exact, n, large)


def _table_lookup(dist, tab_ref, head):
    bkt = _t5_bucket(dist)
    acc = jnp.zeros(dist.shape, F32)
    for b in range(REL_BUCKETS):
        acc = jnp.where(bkt == b, tab_ref[b, head], acc)
    return acc


def _rel_bias(dist, valid, tab_ref, head):
    rows, cols = dist.shape
    if cols % LANE or REL_MAX_DIST > LANE:
        return jnp.where(valid, _table_lookup(dist, tab_ref, head), NEG_INF)
    lane_dist = lax.broadcasted_iota(jnp.int32, (SUBLANE, LANE), 1)
    near = jnp.broadcast_to(_table_lookup(lane_dist, tab_ref, head)[0:1, :], (rows, LANE))
    far = _table_lookup(jnp.full((SUBLANE, LANE), REL_MAX_DIST, jnp.int32), tab_ref, head)[0:1, 0:1]
    parts = []
    for c in range(cols // LANE):
        d = dist[:, c * LANE:(c + 1) * LANE]
        g = jnp.take_along_axis(near, jnp.clip(d, 0, LANE - 1), axis=1)
        parts.append(jnp.where(d >= REL_MAX_DIST, far, g))
    return jnp.where(valid, jnp.concatenate(parts, axis=1), NEG_INF)


def _bias_cmp_kernel(tab_ref, o_ref, *, tb, rows, n_cmp):
    head = pl.program_id(0)
    t = pl.program_id(1) * tb + lax.broadcasted_iota(jnp.int32, (tb, rows), 0)
    n = lax.broadcasted_iota(jnp.int32, (tb, rows), 1)
    dist = t - (n * CMP_STRIDE + CMP_BLOCK - 1)
    o_ref[0] = _rel_bias(dist, (dist >= 0) & (n < n_cmp), tab_ref, head) * LOG2E


def _bias_cmp(table, seq, tb):
    rows = seq // CMP_STRIDE
    n_cmp = (seq - CMP_BLOCK) // CMP_STRIDE + 1
    return pl.pallas_call(
        functools.partial(_bias_cmp_kernel, tb=tb, rows=rows, n_cmp=n_cmp),
        out_shape=jax.ShapeDtypeStruct((NSA_HEADS, seq, rows), F32),
        grid=(NSA_HEADS, seq // tb),
        in_specs=[pl.BlockSpec(memory_space=pltpu.SMEM)],
        out_specs=pl.BlockSpec((1, tb, rows), lambda h, i: (h, i, 0)),
        compiler_params=_cparams("parallel", "arbitrary"),
        name="bias_cmp",
    )(table)


def _bias_tiles_kernel(tab_ref, o_ref, *, tq):
    head = pl.program_id(0)
    kind = pl.program_id(1)
    ij = (lax.broadcasted_iota(jnp.int32, (tq, tq), 0) - lax.broadcasted_iota(jnp.int32, (tq, tq), 1))
    dist = jnp.where(kind == KIND_DIAG, ij, jnp.where(kind == KIND_SUB, tq + ij, WINDOW + ij))
    lo = jnp.where(kind == KIND_DIAG, 0, -tq)
    hi = jnp.where(kind == KIND_CORNER, 0, tq)
    far_dist = jnp.full((SUBLANE, LANE), tq + 1, jnp.int32)
    far = _rel_bias(far_dist, far_dist > 0, tab_ref, head)[0:1, 0:1]
    o_ref[0, 0] = (_rel_bias(dist, (ij >= lo) & (ij < hi), tab_ref, head) - far) * LOG2E


def _bias_tiles(table, tq):
    assert WINDOW % tq == 0 and tq + 1 >= REL_MAX_DIST
    return pl.pallas_call(
        functools.partial(_bias_tiles_kernel, tq=tq),
        out_shape=jax.ShapeDtypeStruct((NSA_KV_GROUPS, N_BIAS_KINDS, NSA_HPG * tq, tq), F32),
        grid=(NSA_HEADS, N_BIAS_KINDS),
        in_specs=[pl.BlockSpec(memory_space=pltpu.SMEM)],
        out_specs=pl.BlockSpec((1, 1, tq, tq), lambda h, k: (h // NSA_HPG, k, h % NSA_HPG, 0)),
        compiler_params=_cparams("parallel", "arbitrary"),
        name="bias_tiles",
    )(table)


def _cmp_scores(q_ref, kc_ref, bias_ref):
    kc = kc_ref[0, 0]
    return [_dot_nt(q_ref[:, j * HEAD_DIM:(j + 1) * HEAD_DIM], kc) + bias_ref[j] for j in range(NSA_HPG)]


def _cmp_attend(scores, vc_ref, bias_ref, ov_ref, oc_ref):
    vc = vc_ref[0, 0]
    psum = None
    for j, s in enumerate(scores):
        hs = slice(j * HEAD_DIM, (j + 1) * HEAD_DIM)
        bias = bias_ref[j]
        m = jnp.max(s, axis=-1, keepdims=True)
        e = jnp.exp2(s - m)
        p = e / jnp.sum(e, axis=-1, keepdims=True)
        p = jnp.where(bias > 0.5 * NEG_INF, p, 0.0)
        oc_ref[:, hs] = _dot(p.astype(BF16), vc)
        psum = p if psum is None else psum + p
    ov = ov_ref[...]
    imp = None
    rem = psum
    for _ in range(3):
        piece = rem.astype(BF16)
        part = _dot_nt(ov, piece)
        imp = part if imp is None else imp + part
        rem = rem - piece.astype(F32)
    return imp


def _cmp_select(imp, qi, sel_ref, *, tq, n_sel, n_top):
    t = qi * tq + lax.broadcasted_iota(jnp.int32, (n_sel, tq), 1)
    cur = jnp.right_shift(t, int(math.log2(SEL_BLOCK)))
    blk = lax.broadcasted_iota(jnp.int32, (n_sel, tq), 0)
    forced = (blk == 0) | (blk == cur) | (blk == cur - 1)
    imp = jnp.where(blk > cur, -1.0, jnp.where(forced, FORCED_SCORE, imp))
    groups = [imp[g * SUBLANE:(g + 1) * SUBLANE, :] for g in range(n_sel // SUBLANE)]
    ranks = [jnp.zeros((SUBLANE, tq), jnp.int32) for _ in groups]
    sub = lax.broadcasted_iota(jnp.int32, (SUBLANE, tq), 0)
    for i in range(n_sel):
        row = imp[i:i + 1, :]
        for g, x in enumerate(groups):
            if g * SUBLANE > i:
                ahead = row >= x
            elif (g + 1) * SUBLANE - 1 < i:
                ahead = row > x
            else:
                ahead = (row > x) | ((row == x) & (sub > i - g * SUBLANE))
            ranks[g] = ranks[g] + ahead.astype(jnp.int32)
    sel_t = jnp.where(jnp.concatenate(ranks, axis=0) < n_top, 0.0, NEG_INF)
    if n_sel < LANE:
        sel_t = jnp.concatenate([sel_t, jnp.zeros((LANE - n_sel, tq), F32)], axis=0)
    sel_ref[0, 0] = sel_t.T.astype(BF16)


def _conv_cmp_kernel(x_ref, wh_ref, wb_ref, wc_ref, wz_ref, cw_ref, q_ref, kc_ref, vc_ref, bias_ref, ov_ref,
                     ya_ref, oc_ref, sel_ref, u_ref, *, tm, tiles_per_seq, tq, n_sel, n_top, nq, batch):
    flat = pl.program_id(0) * pl.num_programs(1) + pl.program_id(1)
    _conv_carry_reset(u_ref, tiles_per_seq)
    scores = _cmp_scores(q_ref, kc_ref, bias_ref)
    _conv_inputs(x_ref, wh_ref, wc_ref, u_ref, tm=tm)
    imp = _cmp_attend(scores, vc_ref, bias_ref, ov_ref, oc_ref)
    _conv_outputs(x_ref, wb_ref, wz_ref, cw_ref, ya_ref, u_ref, tm=tm)
    probe = ya_ref[ROW_SUB:ROW_SUB + SUBLANE, 0:LANE].astype(F32)
    sixteen = jnp.full(probe.shape, 16, jnp.uint32)
    bits = lax.bitcast_convert_type(probe, jnp.uint32)
    zero = lax.shift_right_logical(lax.shift_right_logical(bits, sixteen), sixteen).astype(F32)
    imp = imp + jnp.tile(zero, (n_sel // SUBLANE, tq // LANE))
    _cmp_select(imp, (flat // batch) % nq, sel_ref, tq=tq, n_sel=n_sel, n_top=n_top)


def _conv_cmp(xb, w, layer, conv_w, hb, kcv, bias_c, ov, batch, seq, tm, tc, tq):
    m = xb.shape[0]
    nct = D_CONV // tc
    ni = m // tm
    rows = seq // CMP_STRIDE
    n_sel = seq // SEL_BLOCK
    n_top = min(N_SELECT, n_sel)
    nq = seq // tq
    g_n = NSA_KV_GROUPS
    assert n_sel <= LANE and n_sel % SUBLANE == 0 and nct * ni == g_n * nq * batch

    def wspec(part):
        return pl.BlockSpec((None, D_MODEL, tc), lambda j, i: (layer, 0, (EV_A + part * D_CONV) // tc + j))

    flat = lambda j, i: j * ni + i
    grp = lambda j, i: flat(j, i) // (nq * batch)
    qt = lambda j, i: (flat(j, i) // batch) % nq
    bat = lambda j, i: flat(j, i) % batch
    return pl.pallas_call(
        functools.partial(_conv_cmp_kernel, tm=tm, tiles_per_seq=seq // tm, tq=tq, n_sel=n_sel, n_top=n_top,
                          nq=nq, batch=batch),
        out_shape=(jax.ShapeDtypeStruct((m, D_CONV), BF16),
                   jax.ShapeDtypeStruct((m, D_NSA), F32),
                   jax.ShapeDtypeStruct((batch, g_n, seq, LANE), BF16)),
        grid=(nct, ni),
        in_specs=[pl.BlockSpec((tm, D_MODEL), lambda j, i: (i, 0)),
                  wspec(0), wspec(1), wspec(2), wspec(3),
                  pl.BlockSpec((CONV_WIDTH, tc), lambda j, i: (0, j)),
                  pl.BlockSpec((tq, GROUP_W), lambda j, i: (bat(j, i) * nq + qt(j, i), HB_Q // GROUP_W + grp(j, i))),
                  pl.BlockSpec((1, 1, rows, HEAD_DIM), lambda j, i: (bat(j, i), grp(j, i), 0, 0)),
                  pl.BlockSpec((1, 1, rows, HEAD_DIM), lambda j, i: (bat(j, i), g_n + grp(j, i), 0, 0)),
                  pl.BlockSpec((NSA_HPG, tq, rows), lambda j, i: (grp(j, i), qt(j, i), 0)),
                  pl.BlockSpec((n_sel, rows), lambda j, i: (0, 0))],
        out_specs=(pl.BlockSpec((tm, tc), lambda j, i: (i, j)),
                   pl.BlockSpec((tq, GROUP_W), lambda j, i: (bat(j, i) * nq + qt(j, i), grp(j, i))),
                   pl.BlockSpec((1, 1, tq, LANE), lambda j, i: (bat(j, i), grp(j, i), qt(j, i), 0))),
        scratch_shapes=[pltpu.VMEM((tm + SUBLANE, tc), F32)],
        compiler_params=_cparams("arbitrary", "arbitrary"),
        name="conv_cmp",
    )(xb, w, w, w, w, conv_w, hb, kcv, kcv, bias_c, ov)


FLASH_ROWS = 128


def _flash_init(state):
    _, _, m_ref, _, acc_ref = state
    m_ref[...] = jnp.full(m_ref.shape, -3e38, F32)
    acc_ref[...] = jnp.zeros(acc_ref.shape, F32)


def _flash_scores(qa_ref, kdim, k, s_ref, h, tq):
    hr = slice(h * tq, (h + 1) * tq)
    s_ref[hr, :] = _dot_nt(qa_ref[hr, 0:kdim], k)


def _flash_step(qa_ref, bias_ref, kind, v, nxt, state, tq):
    s_ref, p_ref, m_ref, a_ref, acc_ref = state
    tk = s_ref.shape[1]
    for h in range(NSA_HPG):
        hr = slice(h * tq, (h + 1) * tq)
        for r in range(tq // FLASH_ROWS):
            rs = slice(h * tq + r * FLASH_ROWS, h * tq + (r + 1) * FLASH_ROWS)
            s = s_ref[rs, :]
            if kind is not None:
                s = s + bias_ref[0, kind, rs, :]
            m_old = m_ref[rs, :]
            m_new = jnp.maximum(m_old, jnp.max(s, axis=-1, keepdims=True))
            p_ref[rs, :] = jnp.exp2(s - jnp.tile(m_new, (1, tk // LANE))).astype(BF16)
            a_ref[rs, :] = jnp.exp2(m_old - m_new)
            m_ref[rs, :] = m_new
        if nxt is not None:
            _flash_scores(qa_ref, nxt[0], nxt[1], s_ref, h, tq)
        acc_ref[hr, :] = jnp.tile(a_ref[hr, :], (1, 2)) * acc_ref[hr, :] + _dot(p_ref[hr, :], v)


def _selwin_kernel(q_ref, ks_ref, vs_ref, kw_ref, vw_ref, oh_ref, sel_ref, bias_ref, oc_ref, gt_ref, bz_ref,
                   o_ref, qa_ref, s_ref, p_ref, m_ref, a_ref, acc_ref, os_ref, *, tq):
    qi = pl.program_id(2)
    state = (s_ref, p_ref, m_ref, a_ref, acc_ref)
    aug = 2 * HEAD_DIM
    ones = jnp.ones((tq, HEAD_DIM), BF16)
    for j in range(NSA_HPG):
        qa_ref[j * tq:(j + 1) * tq, 0:HEAD_DIM] = q_ref[:, j * HEAD_DIM:(j + 1) * HEAD_DIM]
        qa_ref[j * tq:(j + 1) * tq, HEAD_DIM:aug] = sel_ref[0, 0]

    def rows_of(kt):
        return pl.ds(pl.multiple_of(kt * tq, tq), tq)

    def sel_keys(kt):
        return aug, jnp.concatenate([ks_ref[rows_of(kt), :], oh_ref[rows_of(kt), :]], axis=1)

    def win_keys(kt):
        return HEAD_DIM, kw_ref[rows_of(kt), :]

    def values(v_ref, kt):
        return jnp.concatenate([v_ref[rows_of(kt), :], ones], axis=1)

    _flash_init(state)
    for h in range(NSA_HPG):
        _flash_scores(qa_ref, *sel_keys(0), s_ref, h, tq)

    def far_body(kt, carry):
        _flash_step(qa_ref, bias_ref, None, values(vs_ref, kt), sel_keys(kt + 1), state, tq)
        return carry

    lax.fori_loop(0, jnp.maximum(qi - 1, 0), far_body, 0)

    def near_tiles(first):
        if not first:
            _flash_step(qa_ref, bias_ref, KIND_SUB, values(vs_ref, qi - 1), sel_keys(qi), state, tq)
        _flash_step(qa_ref, bias_ref, KIND_DIAG, values(vs_ref, qi), win_keys(jnp.maximum(qi - 1, 0)), state, tq)
        os_ref[...] = acc_ref[:, 0:HEAD_DIM] / acc_ref[:, HEAD_DIM:aug]
        _flash_init(state)
        if not first:
            _flash_step(qa_ref, bias_ref, KIND_CORNER, values(vw_ref, qi - 1), win_keys(qi), state, tq)
        _flash_step(qa_ref, bias_ref, KIND_DIAG, values(vw_ref, qi), None, state, tq)
        gate = _sigmoid(gt_ref[...])
        for j in range(NSA_HPG):
            hs = slice(j * HEAD_DIM, (j + 1) * HEAD_DIM)
            rs = slice(j * tq, (j + 1) * tq)
            o_w = acc_ref[rs, 0:HEAD_DIM] / acc_ref[rs, HEAD_DIM:aug]
            o = (gate[:, j:j + 1] * oc_ref[:, hs]
                 + gate[:, NSA_HPG + j:NSA_HPG + j + 1] * os_ref[rs, :]
                 + gate[:, 2 * NSA_HPG + j:2 * NSA_HPG + j + 1] * o_w)
            o_ref[:, hs] = (o * _silu(bz_ref[:, hs])).astype(BF16)

    pl.when(qi == 0)(functools.partial(near_tiles, True))
    pl.when(qi > 0)(functools.partial(near_tiles, False))


def _selwin_attn(hb, ha, onehot, sel, bias_t, o_c, batch, seq, tq):
    nq = seq // tq
    rows = NSA_HPG * tq
    assert tq == WINDOW
    once = pl.Buffered(1)
    kv_spec = lambda base: pl.BlockSpec((seq, HEAD_DIM), lambda g, b, i: (b, base // HEAD_DIM + g))
    row_g = lambda g, b, i: (b * nq + i, g)
    return pl.pallas_call(
        functools.partial(_selwin_kernel, tq=tq),
        out_shape=jax.ShapeDtypeStruct((batch * seq, D_NSA), BF16),
        grid=(NSA_KV_GROUPS, batch, nq),
        in_specs=[pl.BlockSpec((tq, GROUP_W), lambda g, b, i: (b * nq + i, HB_Q // GROUP_W + g)),
                  kv_spec(HB_KS), kv_spec(HB_VS), kv_spec(HB_KW), kv_spec(HB_VW),
                  pl.BlockSpec((seq, LANE), lambda g, b, i: (0, 0), pipeline_mode=once),
                  pl.BlockSpec((1, 1, tq, LANE), lambda g, b, i: (b, g, i, 0)),
                  pl.BlockSpec((1, N_BIAS_KINDS, rows, tq), lambda g, b, i: (g, 0, 0, 0), pipeline_mode=once),
                  pl.BlockSpec((tq, GROUP_W), row_g),
                  pl.BlockSpec((tq, LANE), lambda g, b, i: (b * nq + i, HA_GT // LANE + g)),
                  pl.BlockSpec((tq, GROUP_W), lambda g, b, i: (b * nq + i, HA_BZ // GROUP_W + g))],
        out_specs=pl.BlockSpec((tq, GROUP_W), row_g),
        scratch_shapes=[pltpu.VMEM((rows, 2 * HEAD_DIM), BF16),
                        pltpu.VMEM((rows, tq), F32),
                        pltpu.VMEM((rows, tq), BF16),
                        pltpu.VMEM((rows, LANE), F32),
                        pltpu.VMEM((rows, LANE), F32),
                        pltpu.VMEM((rows, 2 * HEAD_DIM), F32),
                        pltpu.VMEM((rows, HEAD_DIM), F32)],
        compiler_params=_cparams("parallel", "parallel", "arbitrary"),
        name="selwin_attn",
    )(hb, hb, hb, hb, hb, onehot, sel, bias_t, o_c, ha, ha)


def _even_tail_weights(w):
    w_g = w[:, EV_GT:EV_BZ].reshape(D_MODEL, N_BRANCH, NSA_KV_GROUPS, NSA_HPG)
    w_g = jnp.transpose(w_g, (0, 2, 1, 3)).reshape(D_MODEL, NSA_KV_GROUPS, N_BRANCH * NSA_HPG)
    w_g = jnp.pad(w_g, ((0, 0), (0, 0), (0, LANE - N_BRANCH * NSA_HPG))).reshape(D_MODEL, NSA_KV_GROUPS * LANE)
    return jnp.concatenate([w[:, EV_BZ:EV_END], w_g], axis=1).astype(BF16)


def _cast_chunk_kernel(w_ref, o_ref):
    o_ref[...] = w_ref[...].astype(BF16)


def _cast_column_chunks(w, cw):
    layers, k, n = w.shape
    return pl.pallas_call(
        _cast_chunk_kernel,
        out_shape=jax.ShapeDtypeStruct((layers, n // cw, k, cw), BF16),
        grid=(layers, n // cw),
        in_specs=[pl.BlockSpec((None, k, cw), lambda l, c: (l, 0, c))],
        out_specs=pl.BlockSpec((None, None, k, cw), lambda l, c: (l, c, 0, 0)),
        compiler_params=_cparams("parallel", "parallel"),
        name="cast_chunks",
    )(w)


def _overlap_matrix(seq):
    rows = seq // CMP_STRIDE
    n_cmp = (seq - CMP_BLOCK) // CMP_STRIDE + 1
    n_sel = seq // SEL_BLOCK
    cstart = np.arange(rows)[None, :] * CMP_STRIDE
    sstart = np.arange(n_sel)[:, None] * SEL_BLOCK
    ov = (cstart < sstart + SEL_BLOCK) & (cstart + CMP_BLOCK > sstart) & (np.arange(rows)[None, :] < n_cmp)
    return jnp.asarray(ov.astype(np.float32), dtype=BF16)


def _block_onehot(seq):
    blk = np.arange(seq)[:, None] // SEL_BLOCK
    return jnp.asarray((blk == np.arange(LANE)[None, :]).astype(np.float32), dtype=BF16)


def kernel(x, rel_bias_table, ln_g, ln_b, ev_w_in, ev_conv_w, ev_cmp_pos, ev_cmp_w1, ev_cmp_w2, ev_w_out,
           od_w_in, od_ln_g, od_ln_b, od_sgu_w, od_sgu_b, od_w_out):
    batch, seq, d = x.shape
    depth = ln_g.shape[0]
    alpha = (2 * depth) ** 0.25
    m = batch * seq
    tq = ATT_TILE
    assert d == D_MODEL and ev_w_in.shape[-1] == EV_END
    assert seq % max(CONV_ROWS, ATT_TILE, CMP_TILE, BIAS_ROWS) == 0 and m % max(ODD_ROWS, OUT_ROWS, NSA_ROWS) == 0

    xf = x.reshape(m, d)
    xb = None
    bias_c = _bias_cmp(rel_bias_table, seq, tb=BIAS_ROWS)
    bias_t = _bias_tiles(rel_bias_table, tq)
    ov = _overlap_matrix(seq)
    onehot = _block_onehot(seq)
    ev_w, ev_wo = ev_w_in.astype(BF16), ev_w_out.astype(BF16)
    od_wo = od_w_out.astype(BF16)
    od_w = _cast_column_chunks(od_w_in, PROJ_CHUNK)

    for layer in range(depth):
        i = layer // 2
        g = ln_g[layer].reshape(1, d)
        b = ln_b[layer].reshape(1, d)
        if layer % 2 == 0:
            w_tail = _even_tail_weights(ev_w[i])
            hb, ha, *cast = _nsa_proj(xf if xb is None else xb, ev_w, i, w_tail, tm=NSA_ROWS)
            xb = cast[0] if cast else xb
            kcv = _compress(ha, ev_cmp_w1[i].astype(BF16), ev_cmp_w2[i].astype(BF16),
                            ev_cmp_pos[i].reshape(2, 1, CMP_BLOCK * HEAD_DIM).astype(BF16), batch, seq)
            y_a, o_c, sel = _conv_cmp(xb, ev_w, i, ev_conv_w[i], hb, kcv, bias_c, ov, batch, seq,
                                      tm=CONV_ROWS, tc=CONV_COLS, tq=CMP_TILE)
            y_b = _selwin_attn(hb, ha, onehot, sel, bias_t, o_c, batch, seq, tq)
            xf, xb = _outproj_ln(y_a, 0, y_b, 0, ev_wo, i, xf, g, b, alpha, OUT_ROWS, "outproj_even")
        else:
            y = _odd_mixer(xb, od_w, i, od_ln_g[i].reshape(1, d), od_ln_b[i].reshape(1, d),
                           od_sgu_w[i], od_sgu_b[i].reshape(SGU_GROUPS, SGU_CHUNK, 1), tm=ODD_ROWS)
            xf, xb = _outproj_ln(y, 0, y, 1, od_wo, i, xf, g, b, alpha, OUT_ROWS, "outproj_odd")
    return xf.reshape(batch, seq, d)
```
